```python
import math
import jax
import jax.numpy as jnp
from jax import lax
import numpy as np

D_MODEL = 1024
BATCH = 8
SEQ = 4096
DEPTH = 2

CTX_LEN = 256
GRID_W = 64
EPS = 1e-6
MOD_CHUNKS = 6

DIFF_HEADS = 4
DIFF_QK = 32
DIFF_V = 2 * DIFF_QK
DIFF_W = DIFF_HEADS * DIFF_V
ROPE_BASE = 10000.0
Q_BLOCK = 128

SSM_HEADS = 8
SSM_P = 64
SSM_GROUPS = 2
SSM_HPG = SSM_HEADS // SSM_GROUPS
SSM_N = 64
SSM_W = SSM_HEADS * SSM_P
SSM_XBC = SSM_W + 2 * SSM_GROUPS * SSM_N
SSD_CHUNK = 128
CONV_K = 5

GDN_HEADS = 4
GDN_DK = 64
GDN_DV = 64
GDN_QKV = GDN_HEADS * (2 * GDN_DK + GDN_DV)
GDN_W = GDN_HEADS * GDN_DV
GDN_CHUNK = 64

D_MIX = DIFF_W + SSM_W + GDN_W
SPLIT_SIZES = (3 * DIFF_W, SSM_W, SSM_XBC, SSM_HEADS, GDN_QKV, GDN_W, 2 * GDN_HEADS, 2 * GDN_HEADS)
IN_DIM = sum(SPLIT_SIZES)

N_EGROUPS = 4
EXPERTS_PER_GROUP = 8
N_EXPERTS = N_EGROUPS * EXPERTS_PER_GROUP
TOP_K = 2
D_EXPERT = 512
MOE_BLOCK = 256

kernel_name = 'hybrid_diffusion_trunk'


def rmsnorm(x, w):
    x32 = x.astype(jnp.float32)
    y = x32 * lax.rsqrt(jnp.mean(x32 * x32, axis=-1, keepdims=True) + EPS)
    return y.astype(x.dtype) * w


def l2norm(x):
    x32 = x.astype(jnp.float32)
    return x32 * lax.rsqrt(jnp.sum(x32 * x32, axis=-1, keepdims=True) + EPS)


def modulate(x, shift, scale):
    return x * (1.0 + scale) + shift


def flip(t):
    return jnp.flip(t, axis=1)


def centred_dwconv(x, w):
    pad = CONV_K // 2
    return lax.conv_general_dilated(
        x, w[:, None, :].astype(x.dtype), window_strides=(1,), padding=[(pad, pad)],
        dimension_numbers=('NWC', 'WIO', 'NWC'), feature_group_count=x.shape[-1])


def axial_rope_tables(rows):
    row = jnp.repeat(jnp.arange(rows, dtype=jnp.float32), GRID_W)
    col = jnp.tile(jnp.arange(GRID_W, dtype=jnp.float32), rows)
    half = DIFF_QK // 2
    inv = ROPE_BASE ** (-jnp.arange(0, half, 2, dtype=jnp.float32) / half)
    ang_r = row[:, None] * inv
    ang_c = col[:, None] * inv
    ang = jnp.concatenate([ang_r, ang_r, ang_c, ang_c], axis=-1)
    return jnp.cos(ang), jnp.sin(ang)


def apply_axial_rope(x, cos, sin):
    a1, a2, b1, b2 = jnp.split(x, 4, axis=-1)
    rot = jnp.concatenate([-a2, a1, -b2, b1], axis=-1)
    cs = cos[None, :, None, None, :]
    sn = sin[None, :, None, None, :]
    return (x * cs + rot * sn).astype(x.dtype)


def diff_attention(hc, hl, cos, sin, lam_init, with_ctx, qn_w, kn_w, lq1, lk1, lq2, lk2, out_w):
    def qkv(h):
        b, n = h.shape[:2]
        q, k, v = jnp.split(h, 3, axis=-1)
        q = rmsnorm(q.reshape(b, n, DIFF_HEADS, 2, DIFF_QK), qn_w)
        k = rmsnorm(k.reshape(b, n, DIFF_HEADS, 2, DIFF_QK), kn_w)
        return q, k, v.reshape(b, n, DIFF_HEADS, DIFF_V)

    qc, kc, vc = qkv(hc)
    ql, kl, vl = qkv(hl)
    ql = apply_axial_rope(ql, cos, sin)
    kl = apply_axial_rope(kl, cos, sin)
    lam = (jnp.exp(jnp.sum(lq1 * lk1).astype(jnp.float32))
           - jnp.exp(jnp.sum(lq2 * lk2).astype(jnp.float32)) + lam_init)
    scale = DIFF_QK ** -0.5

    def attend(q, k, v):
        s = jnp.einsum('bqhmd,bkhmd->bhmqk', q, k, preferred_element_type=jnp.float32) * scale
        p = jax.nn.softmax(s, axis=-1)
        a = p[:, :, 0] - lam * p[:, :, 1]
        return jnp.einsum('bhqk,bkhv->bqhv', a.astype(v.dtype), v)

    def finish(o):
        b, n = o.shape[:2]
        return (rmsnorm(o, out_w) * (1.0 - lam_init)).reshape(b, n, DIFF_W)

    k_all = jnp.concatenate([kc, kl], axis=1)
    v_all = jnp.concatenate([vc, vl], axis=1)
    b, n = hl.shape[:2]
    nb = n // Q_BLOCK
    qb = jnp.moveaxis(ql.reshape(b, nb, Q_BLOCK, DIFF_HEADS, 2, DIFF_QK), 1, 0)
    ol = lax.map(lambda q: attend(q, k_all, v_all), qb)
    ol = finish(jnp.moveaxis(ol, 0, 1).reshape(b, n, DIFF_HEADS, DIFF_V))
    oc = finish(attend(qc, kc, vc)) if with_ctx else None
    return oc, ol


def ssd_chunked(xs, dt, a, bm, cm, h0):
    b, n = xs.shape[:2]
    nc = n // SSD_CHUNK
    xdt = (xs * dt[..., None]).reshape(b, nc, SSD_CHUNK, SSM_GROUPS, SSM_HPG, SSM_P)
    acum = jnp.cumsum((dt * a).reshape(b, nc, SSD_CHUNK, SSM_GROUPS, SSM_HPG), axis=2)
    bc = bm.reshape(b, nc, SSD_CHUNK, SSM_GROUPS, SSM_N)
    cc = cm.reshape(b, nc, SSD_CHUNK, SSM_GROUPS, SSM_N)
    incl = jnp.tril(jnp.ones((SSD_CHUNK, SSD_CHUNK), bool))[:, :, None, None]
    seg = acum[:, :, :, None] - acum[:, :, None, :]
    lmat = jnp.exp(jnp.where(incl, seg, -jnp.inf))
    cb = jnp.einsum('bclgn,bcsgn->bclsg', cc, bc)
    y_diag = jnp.einsum('bclsg,bclsgr,bcsgrp->bclgrp', cb, lmat, xdt)
    states = jnp.einsum('bclgn,bclgr,bclgrp->bcgrpn', bc, jnp.exp(acum[:, :, -1:] - acum), xdt)
    chunk_decay = jnp.exp(acum[:, :, -1])

    def step(h, inp):
        st, dec = inp
        return h * dec[..., None, None] + st, h

    h_final, h_in = lax.scan(step, h0, (jnp.moveaxis(states, 1, 0), jnp.moveaxis(chunk_decay, 1, 0)))
    y_off = jnp.einsum('bclgn,cbgrpn,bclgr->bclgrp', cc, h_in, jnp.exp(acum))
    return (y_diag + y_off).reshape(b, n, SSM_GROUPS, SSM_HPG, SSM_P), h_final


def ssm_mixer(zc, xbc_c, dt_c, zl, xbc_l, dt_l, with_ctx, conv_w, conv_b, dt_bias, a_log, d_skip, norm_w):
    def prep(xbc, dt):
        xbc = jax.nn.silu(centred_dwconv(xbc, conv_w) + conv_b)
        b, n = xbc.shape[:2]
        xs, bm, cm = jnp.split(xbc, [SSM_W, SSM_W + SSM_GROUPS * SSM_N], axis=-1)
        xs = xs.reshape(b, n, SSM_GROUPS, SSM_HPG, SSM_P).astype(jnp.float32)
        bm = bm.reshape(b, n, SSM_GROUPS, SSM_N).astype(jnp.float32)
        cm = cm.reshape(b, n, SSM_GROUPS, SSM_N).astype(jnp.float32)
        dt = dt.astype(jnp.float32)
        dts = [jax.nn.softplus(dt + dt_bias[d].astype(jnp.float32)).reshape(b, n, SSM_GROUPS, SSM_HPG)
               for d in range(2)]
        return xs, bm, cm, dts

    a = -jnp.exp(a_log.astype(jnp.float32)).reshape(2, SSM_GROUPS, SSM_HPG)
    xc, bc, cc, dtc = prep(xbc_c, dt_c)
    xl, bl, cl, dtl = prep(xbc_l, dt_l)
    h0 = jnp.zeros((xl.shape[0], SSM_GROUPS, SSM_HPG, SSM_P, SSM_N), jnp.float32)
    yc_f, hc_f = ssd_chunked(xc, dtc[0], a[0], bc, cc, h0)
    yc_b, hc_b = ssd_chunked(flip(xc), flip(dtc[1]), a[1], flip(bc), flip(cc), h0)
    yl_f, _ = ssd_chunked(xl, dtl[0], a[0], bl, cl, hc_f)
    yl_b, _ = ssd_chunked(flip(xl), flip(dtl[1]), a[1], flip(bl), flip(cl), hc_b)
    d_h = d_skip.astype(jnp.float32).reshape(SSM_GROUPS, SSM_HPG)[..., None]

    def finish(y_f, y_b, xs, z):
        b, n = z.shape[:2]
        y = (y_f + flip(y_b) + d_h * xs).reshape(b, n, SSM_W).astype(z.dtype) * jax.nn.silu(z)
        y = rmsnorm(y.reshape(b, n, SSM_GROUPS, SSM_W // SSM_GROUPS), norm_w.reshape(SSM_GROUPS, -1))
        return y.reshape(b, n, SSM_W)

    oc = finish(yc_f, yc_b, xc, zc) if with_ctx else None
    return oc, finish(yl_f, yl_b, xl, zl)


def gated_delta_chunked(q, k, v, g, beta, s0):
    b, n = q.shape[:2]
    nc = n // GDN_CHUNK

    def chunks(t):
        return jnp.moveaxis(t.reshape((b, nc, GDN_CHUNK) + t.shape[2:]), 2, 3)

    q, k, v, g, beta = (chunks(t) for t in (q, k, v, g, beta))
    gc = jnp.cumsum(g, axis=-1)
    kb = k * beta[..., None]
    vb = v * beta[..., None]
    incl = jnp.tril(jnp.ones((GDN_CHUNK, GDN_CHUNK), bool))
    strict = jnp.tril(jnp.ones((GDN_CHUNK, GDN_CHUNK), bool), -1)
    dec = jnp.exp(jnp.where(incl, gc[..., :, None] - gc[..., None, :], -jnp.inf))
    a_strict = jnp.where(strict, jnp.einsum('bchld,bchsd->bchls', kb, k) * dec, 0.0)
    eye = jnp.eye(GDN_CHUNK, dtype=a_strict.dtype)
    t_inv = lax.linalg.triangular_solve(a_strict + eye, jnp.broadcast_to(eye, a_strict.shape),
                                        left_side=True, lower=True)
    u = jnp.einsum('bchls,bchsv->bchlv', t_inv, vb)
    w = jnp.einsum('bchls,bchsd->bchld', t_inv, kb * jnp.exp(gc)[..., None])
    qk = jnp.where(incl, jnp.einsum('bchld,bchsd->bchls', q, k) * dec, 0.0)
    q_dec = q * jnp.exp(gc)[..., None]
    k_to_end = k * jnp.exp(gc[..., -1:] - gc)[..., None]
    g_end = jnp.exp(gc[..., -1])

    def step(state, inp):
        qd, kd, u_c, w_c, qk_c, ge = inp
        v_new = u_c - jnp.einsum('bhld,bhdv->bhlv', w_c, state)
        o = jnp.einsum('bhld,bhdv->bhlv', qd, state) + jnp.einsum('bhls,bhsv->bhlv', qk_c, v_new)
        state = state * ge[..., None, None] + jnp.einsum('bhld,bhlv->bhdv', kd, v_new)
        return state, o

    xs = tuple(jnp.moveaxis(t, 1, 0) for t in (q_dec, k_to_end, u, w, qk, g_end))
    s_final, o = lax.scan(step, s0, xs)
    o = jnp.moveaxis(jnp.moveaxis(o, 0, 1), 3, 2).reshape(b, n, GDN_HEADS, GDN_DV)
    return o, s_final


def gdn_mixer(qkv_c, gate_c, beta_c, a_c, qkv_l, gate_l, beta_l, a_l, with_ctx, conv_w, dt_bias, a_log, norm_w):
    def prep(qkv, beta, a):
        qkv = jax.nn.silu(centred_dwconv(qkv, conv_w))
        b, n = qkv.shape[:2]
        q, k, v = jnp.split(qkv, [GDN_HEADS * GDN_DK, 2 * GDN_HEADS * GDN_DK], axis=-1)
        q = l2norm(q.reshape(b, n, GDN_HEADS, GDN_DK)) * (GDN_DK ** -0.5)
        k = l2norm(k.reshape(b, n, GDN_HEADS, GDN_DK))
        v = v.reshape(b, n, GDN_HEADS, GDN_DV).astype(jnp.float32)
        beta = jax.nn.sigmoid(beta.astype(jnp.float32)).reshape(b, n, 2, GDN_HEADS)
        a = a.astype(jnp.float32).reshape(b, n, 2, GDN_HEADS)
        g = -jnp.exp(a_log.astype(jnp.float32)) * jax.nn.softplus(a + dt_bias.astype(jnp.float32))
        return q, k, v, g, beta

    qc, kc, vc, gcx, bcx = prep(qkv_c, beta_c, a_c)
    ql, kl, vl, glx, blx = prep(qkv_l, beta_l, a_l)
    s0 = jnp.zeros((ql.shape[0], GDN_HEADS, GDN_DK, GDN_DV), jnp.float32)
    oc_f, sc_f = gated_delta_chunked(qc, kc, vc, gcx[:, :, 0], bcx[:, :, 0], s0)
    oc_b, sc_b = gated_delta_chunked(flip(qc), flip(kc), flip(vc), flip(gcx[:, :, 1]), flip(bcx[:, :, 1]), s0)
    ol_f, _ = gated_delta_chunked(ql, kl, vl, glx[:, :, 0], blx[:, :, 0], sc_f)
    ol_b, _ = gated_delta_chunked(flip(ql), flip(kl), flip(vl), flip(glx[:, :, 1]), flip(blx[:, :, 1]), sc_b)

    def finish(o_f, o_b, gate):
        b, n = gate.shape[:2]
        o = rmsnorm(o_f + flip(o_b), norm_w) * jax.nn.silu(gate.reshape(b, n, GDN_HEADS, GDN_DV))
        return o.reshape(b, n, GDN_W)

    oc = finish(oc_f, oc_b, gate_c) if with_ctx else None
    return oc, finish(ol_f, ol_b, gate_l)


def token_mixers(hc, hl, cos, sin, lam_init, with_ctx,
                 diff_qn_w, diff_kn_w, diff_lq1, diff_lk1, diff_lq2, diff_lk2, diff_norm_w,
                 ssm_conv_w, ssm_conv_b, ssm_dt_bias, ssm_a_log, ssm_d, ssm_norm_w,
                 gdn_conv_w, gdn_dt_bias, gdn_a_log, gdn_norm_w):
    idx = [int(i) for i in np.cumsum(SPLIT_SIZES)[:-1]]
    pc = jnp.split(hc, idx, axis=-1)
    pl = jnp.split(hl, idx, axis=-1)
    ac, al = diff_attention(pc[0], pl[0], cos, sin, lam_init, with_ctx, diff_qn_w, diff_kn_w,
                            diff_lq1, diff_lk1, diff_lq2, diff_lk2, diff_norm_w)
    sc, sl = ssm_mixer(pc[1], pc[2], pc[3], pl[1], pl[2], pl[3], with_ctx, ssm_conv_w, ssm_conv_b,
                       ssm_dt_bias, ssm_a_log, ssm_d, ssm_norm_w)
    gc, gl = gdn_mixer(pc[4], pc[5], pc[6], pc[7], pl[4], pl[5], pl[6], pl[7], with_ctx, gdn_conv_w,
                       gdn_dt_bias, gdn_a_log, gdn_norm_w)
    dt = hl.dtype
    ml = jnp.concatenate([al.astype(dt), sl.astype(dt), gl.astype(dt)], axis=-1)
    mc = jnp.concatenate([ac.astype(dt), sc.astype(dt), gc.astype(dt)], axis=-1) if with_ctx else None
    return mc, ml


def routed_experts(t, experts, gates, w_gate, w_up, w_down):
    n, d = t.shape
    n_assign = n * TOP_K
    flat_e = experts.reshape(n_assign)
    order = jnp.argsort(flat_e)
    sorted_e = flat_e[order]
    counts = jax.ops.segment_sum(jnp.ones((n_assign,), jnp.int32), flat_e, num_segments=N_EXPERTS)
    starts = jnp.cumsum(counts) - counts
    padded = (counts + MOE_BLOCK - 1) // MOE_BLOCK * MOE_BLOCK
    pad_ends = jnp.cumsum(padded)
    pad_starts = pad_ends - padded
    dest = pad_starts[sorted_e] + jnp.arange(n_assign, dtype=jnp.int32) - starts[sorted_e]
    n_blocks = -(-n_assign // MOE_BLOCK) + N_EXPERTS
    buf = jnp.zeros((n_blocks * MOE_BLOCK, d), t.dtype).at[dest].set(t[order // TOP_K])
    block_start = jnp.arange(n_blocks, dtype=jnp.int32) * MOE_BLOCK
    block_e = jnp.minimum(jnp.searchsorted(pad_ends, block_start, side='right'), N_EXPERTS - 1)

    def expert_block(args):
        xb, e = args
        hb = jax.nn.silu(xb @ w_gate[e]) * (xb @ w_up[e])
        return hb @ w_down[e]

    out = lax.map(expert_block, (buf.reshape(n_blocks, MOE_BLOCK, d), block_e)).reshape(-1, d)
    y_assign = jnp.zeros((n_assign, d), out.dtype).at[order].set(out[dest])
    return jnp.sum(y_assign.reshape(n, TOP_K, d) * gates[..., None].astype(out.dtype), axis=1)


def hier_moe(t, router_g_w, router_g_b, router_e_w, router_e_b, w_gate, w_up, w_down):
    n = t.shape[0]
    grp_prob = jax.nn.softmax(jnp.dot(t, router_g_w, preferred_element_type=jnp.float32)
                              + router_g_b.astype(jnp.float32), axis=-1)
    p_grp, grp = lax.top_k(grp_prob, 1)
    e_logits = (jnp.dot(t, router_e_w, preferred_element_type=jnp.float32)
                + router_e_b.astype(jnp.float32)).reshape(n, N_EGROUPS, EXPERTS_PER_GROUP)
    sel = e_logits[jnp.arange(n), grp[:, 0]]
    p_top, idx = lax.top_k(jax.nn.softmax(sel, axis=-1), TOP_K)
    gates = p_grp * p_top / jnp.sum(p_top, axis=-1, keepdims=True)
    experts = grp * EXPERTS_PER_GROUP + idx
    return routed_experts(t, experts, gates, w_gate, w_up, w_down)


def setup_inputs(seed: int = 0) -> dict:
    key = jax.random.key(seed)
    ks = iter(jax.random.split(key, 48))

    def nrm(shape, scale):
        return jax.random.normal(next(ks), shape, jnp.float32) * scale

    def gain(shape):
        return 1.0 + nrm(shape, 0.05)

    def unif(shape, lo, hi):
        return jax.random.uniform(next(ks), shape, jnp.float32, lo, hi)

    L = DEPTH
    ssm_dt = jnp.exp(unif((L, 2, SSM_HEADS), math.log(1e-3), math.log(1e-1)))
    gdn_dt = jnp.exp(unif((L, 2, GDN_HEADS), math.log(1e-3), math.log(1e-1)))
    return {
        'x': nrm((BATCH, SEQ, D_MODEL), 1.0),
        'c': nrm((BATCH, D_MODEL), 1.0),
        'ctx': nrm((BATCH, CTX_LEN, D_MODEL), 1.0),
        'c_ctx': nrm((D_MODEL,), 1.0),
        'w_mod': nrm((L, D_MODEL, MOD_CHUNKS * D_MODEL), 0.5 * D_MODEL ** -0.5),
        'b_mod': nrm((L, MOD_CHUNKS * D_MODEL), 0.02),
        'norm1_w': gain((L, D_MODEL)),
        'norm2_w': gain((L, D_MODEL)),
        'w_in': nrm((L, D_MODEL, IN_DIM), D_MODEL ** -0.5),
        'w_out': nrm((L, D_MIX, D_MODEL), D_MIX ** -0.5),
        'diff_qn_w': gain((L, DIFF_QK)),
        'diff_kn_w': gain((L, DIFF_QK)),
        'diff_lq1': nrm((L, DIFF_QK), 0.1),
        'diff_lk1': nrm((L, DIFF_QK), 0.1),
        'diff_lq2': nrm((L, DIFF_QK), 0.1),
        'diff_lk2': nrm((L, DIFF_QK), 0.1),
        'diff_norm_w': gain((L, DIFF_V)),
        'ssm_conv_w': nrm((L, CONV_K, SSM_XBC), CONV_K ** -0.5),
        'ssm_conv_b': nrm((L, SSM_XBC), 0.02),
        'ssm_dt_bias': ssm_dt + jnp.log(-jnp.expm1(-ssm_dt)),
        'ssm_a_log': jnp.log(unif((L, 2, SSM_HEADS), 1.0, 16.0)),
        'ssm_d': gain((L, SSM_HEADS)),
        'ssm_norm_w': gain((L, SSM_W)),
        'gdn_conv_w': nrm((L, CONV_K, GDN_QKV), CONV_K ** -0.5),
        'gdn_dt_bias': gdn_dt + jnp.log(-jnp.expm1(-gdn_dt)),
        'gdn_a_log': jnp.log(unif((L, 2, GDN_HEADS), 1.0, 16.0)),
        'gdn_norm_w': gain((L, GDN_DV)),
        'router_g_w': nrm((L, D_MODEL, N_EGROUPS), D_MODEL ** -0.5),
        'router_g_b': nrm((L, N_EGROUPS), 0.01),
        'router_e_w': nrm((L, D_MODEL, N_EXPERTS), D_MODEL ** -0.5),
        'router_e_b': nrm((L, N_EXPERTS), 0.01),
        'exp_w_gate': nrm((L, N_EXPERTS, D_MODEL, D_EXPERT), D_MODEL ** -0.5),
        'exp_w_up': nrm((L, N_EXPERTS, D_MODEL, D_EXPERT), D_MODEL ** -0.5),
        'exp_w_down': nrm((L, N_EXPERTS, D_EXPERT, D_MODEL), D_EXPERT ** -0.5),
    }


def reference(x, c, ctx, c_ctx, w_mod, b_mod, norm1_w, norm2_w, w_in, w_out,
              diff_qn_w, diff_kn_w, diff_lq1, diff_lk1, diff_lq2, diff_lk2, diff_norm_w,
              ssm_conv_w, ssm_conv_b, ssm_dt_bias, ssm_a_log, ssm_d, ssm_norm_w,
              gdn_conv_w, gdn_dt_bias, gdn_a_log, gdn_norm_w,
              router_g_w, router_g_b, router_e_w, router_e_b, exp_w_gate, exp_w_up, exp_w_down):
    rows = x.shape[1] // GRID_W
    cos, sin = axial_rope_tables(rows)
    xl, xc = x, ctx
    for l in range(DEPTH):
        last = l == DEPTH - 1
        lam_init = 0.8 - 0.6 * math.exp(-0.3 * l)
        mod_l = [m[:, None, :] for m in jnp.split(jax.nn.silu(c) @ w_mod[l] + b_mod[l], MOD_CHUNKS, axis=-1)]
        mod_c = jnp.split(jax.nn.silu(c_ctx) @ w_mod[l] + b_mod[l], MOD_CHUNKS, axis=-1)
        hl = modulate(rmsnorm(xl, norm1_w[l]), mod_l[0], mod_l[1]) @ w_in[l]
        hc = modulate(rmsnorm(xc, norm1_w[l]), mod_c[0], mod_c[1]) @ w_in[l]
        mc, ml = token_mixers(hc, hl, cos, sin, lam_init, not last,
                              diff_qn_w[l], diff_kn_w[l], diff_lq1[l], diff_lk1[l], diff_lq2[l], diff_lk2[l],
                              diff_norm_w[l], ssm_conv_w[l], ssm_conv_b[l], ssm_dt_bias[l], ssm_a_log[l],
                              ssm_d[l], ssm_norm_w[l], gdn_conv_w[l], gdn_dt_bias[l], gdn_a_log[l], gdn_norm_w[l])
        xl = xl + mod_l[2] * (ml @ w_out[l])
        fl = modulate(rmsnorm(xl, norm2_w[l]), mod_l[3], mod_l[4]).reshape(-1, D_MODEL)
        moe_args = (router_g_w[l], router_g_b[l], router_e_w[l], router_e_b[l],
                    exp_w_gate[l], exp_w_up[l], exp_w_down[l])
        if last:
            yl = hier_moe(fl, *moe_args)
        else:
            xc = xc + mod_c[2] * (mc @ w_out[l])
            fc = modulate(rmsnorm(xc, norm2_w[l]), mod_c[3], mod_c[4]).reshape(-1, D_MODEL)
            y = hier_moe(jnp.concatenate([fl, fc], axis=0), *moe_args)
            yl = y[:fl.shape[0]]
            xc = xc + mod_c[5] * y[fl.shape[0]:].reshape(xc.shape)
        xl = xl + mod_l[5] * yl.reshape(xl.shape)
    return xl
```

```python
import math
import jax
import jax.numpy as jnp
from jax import lax
import numpy as np
from jax.experimental import pallas as pl
from jax.experimental.pallas import tpu as pltpu

D_MODEL = 1024
BATCH = 8
SEQ = 4096
DEPTH = 2
CTX_LEN = 256
GRID_W = 64
EPS = 1e-6
MOD_CHUNKS = 6

DIFF_HEADS = 4
DIFF_QK = 32
DIFF_V = 2 * DIFF_QK
DIFF_W = DIFF_HEADS * DIFF_V
ROPE_BASE = 10000.0
Q_BLOCK = 128

SSM_HEADS = 8
SSM_P = 64
SSM_GROUPS = 2
SSM_HPG = SSM_HEADS // SSM_GROUPS
SSM_N = 64
SSM_W = SSM_HEADS * SSM_P
SSM_XBC = SSM_W + 2 * SSM_GROUPS * SSM_N
SSD_CHUNK = 128
CONV_K = 5

GDN_HEADS = 4
GDN_DK = 64
GDN_DV = 64
GDN_QKV = GDN_HEADS * (2 * GDN_DK + GDN_DV)
GDN_W = GDN_HEADS * GDN_DV
GDN_CHUNK = 64

D_MIX = DIFF_W + SSM_W + GDN_W
SPLIT_SIZES = (3 * DIFF_W, SSM_W, SSM_XBC, SSM_HEADS, GDN_QKV, GDN_W, 2 * GDN_HEADS, 2 * GDN_HEADS)
IN_DIM = sum(SPLIT_SIZES)

N_EGROUPS = 4
EXPERTS_PER_GROUP = 8
N_EXPERTS = N_EGROUPS * EXPERTS_PER_GROUP
TOP_K = 2
D_EXPERT = 512
MOE_BLOCK = 256


def rmsnorm(x, w):
    x32 = x.astype(jnp.float32)
    y = x32 * lax.rsqrt(jnp.mean(x32 * x32, axis=-1, keepdims=True) + EPS)
    return y.astype(x.dtype) * w


def l2norm(x):
    x32 = x.astype(jnp.float32)
    return x32 * lax.rsqrt(jnp.sum(x32 * x32, axis=-1, keepdims=True) + EPS)


def modulate(x, shift, scale):
    return x * (1.0 + scale) + shift


def flip(t):
    return jnp.flip(t, axis=1)


def centred_dwconv(x, w):
    pad = CONV_K // 2
    return lax.conv_general_dilated(
        x, w[:, None, :].astype(x.dtype), window_strides=(1,), padding=[(pad, pad)],
        dimension_numbers=('NWC', 'WIO', 'NWC'), feature_group_count=x.shape[-1])


def axial_rope_tables(rows):
    row = jnp.repeat(jnp.arange(rows, dtype=jnp.float32), GRID_W)
    col = jnp.tile(jnp.arange(GRID_W, dtype=jnp.float32), rows)
    half = DIFF_QK // 2
    inv = ROPE_BASE ** (-jnp.arange(0, half, 2, dtype=jnp.float32) / half)
    ang_r = row[:, None] * inv
    ang_c = col[:, None] * inv
    ang = jnp.concatenate([ang_r, ang_r, ang_c, ang_c], axis=-1)
    return jnp.cos(ang), jnp.sin(ang)


def apply_axial_rope(x, cos, sin):
    a1, a2, b1, b2 = jnp.split(x, 4, axis=-1)
    rot = jnp.concatenate([-a2, a1, -b2, b1], axis=-1)
    cs = cos[None, :, None, None, :]
    sn = sin[None, :, None, None, :]
    return (x * cs + rot * sn).astype(x.dtype)


def diff_attention(hc, hl, cos, sin, lam_init, with_ctx, qn_w, kn_w, lq1, lk1, lq2, lk2, out_w):
    def qkv(h):
        b, n = h.shape[:2]
        q, k, v = jnp.split(h, 3, axis=-1)
        q = rmsnorm(q.reshape(b, n, DIFF_HEADS, 2, DIFF_QK), qn_w)
        k = rmsnorm(k.reshape(b, n, DIFF_HEADS, 2, DIFF_QK), kn_w)
        return q, k, v.reshape(b, n, DIFF_HEADS, DIFF_V)

    qc, kc, vc = qkv(hc)
    ql, kl, vl = qkv(hl)
    ql = apply_axial_rope(ql, cos, sin)
    kl = apply_axial_rope(kl, cos, sin)
    lam = (jnp.exp(jnp.sum(lq1 * lk1).astype(jnp.float32))
           - jnp.exp(jnp.sum(lq2 * lk2).astype(jnp.float32)) + lam_init)
    scale = DIFF_QK ** -0.5

    def attend(q, k, v):
        s = jnp.einsum('bqhmd,bkhmd->bhmqk', q, k, preferred_element_type=jnp.float32) * scale
        p = jax.nn.softmax(s, axis=-1)
        a = p[:, :, 0] - lam * p[:, :, 1]
        return jnp.einsum('bhqk,bkhv->bqhv', a.astype(v.dtype), v)

    def finish(o):
        b, n = o.shape[:2]
        return (rmsnorm(o, out_w) * (1.0 - lam_init)).reshape(b, n, DIFF_W)

    k_all = jnp.concatenate([kc, kl], axis=1)
    v_all = jnp.concatenate([vc, vl], axis=1)
    b, n = hl.shape[:2]
    nb = n // Q_BLOCK
    qb = jnp.moveaxis(ql.reshape(b, nb, Q_BLOCK, DIFF_HEADS, 2, DIFF_QK), 1, 0)
    ol = lax.map(lambda q: attend(q, k_all, v_all), qb)
    ol = finish(jnp.moveaxis(ol, 0, 1).reshape(b, n, DIFF_HEADS, DIFF_V))
    oc = finish(attend(qc, kc, vc)) if with_ctx else None
    return oc, ol


def ssd_chunked(xs, dt, a, bm, cm, h0):
    b, n = xs.shape[:2]
    nc = n // SSD_CHUNK
    xdt = (xs * dt[..., None]).reshape(b, nc, SSD_CHUNK, SSM_GROUPS, SSM_HPG, SSM_P)
    acum = jnp.cumsum((dt * a).reshape(b, nc, SSD_CHUNK, SSM_GROUPS, SSM_HPG), axis=2)
    bc = bm.reshape(b, nc, SSD_CHUNK, SSM_GROUPS, SSM_N)
    cc = cm.reshape(b, nc, SSD_CHUNK, SSM_GROUPS, SSM_N)
    incl = jnp.tril(jnp.ones((SSD_CHUNK, SSD_CHUNK), bool))[:, :, None, None]
    seg = acum[:, :, :, None] - acum[:, :, None, :]
    lmat = jnp.exp(jnp.where(incl, seg, -jnp.inf))
    cb = jnp.einsum('bclgn,bcsgn->bclsg', cc, bc)
    y_diag = jnp.einsum('bclsg,bclsgr,bcsgrp->bclgrp', cb, lmat, xdt)
    states = jnp.einsum('bclgn,bclgr,bclgrp->bcgrpn', bc, jnp.exp(acum[:, :, -1:] - acum), xdt)
    chunk_decay = jnp.exp(acum[:, :, -1])

    def step(h, inp):
        st, dec = inp
        return h * dec[..., None, None] + st, h

    h_final, h_in = lax.scan(step, h0, (jnp.moveaxis(states, 1, 0), jnp.moveaxis(chunk_decay, 1, 0)))
    y_off = jnp.einsum('bclgn,cbgrpn,bclgr->bclgrp', cc, h_in, jnp.exp(acum))
    return (y_diag + y_off).reshape(b, n, SSM_GROUPS, SSM_HPG, SSM_P), h_final


def ssm_mixer(zc, xbc_c, dt_c, zl, xbc_l, dt_l, with_ctx, conv_w, conv_b, dt_bias, a_log, d_skip, norm_w):
    def prep(xbc, dt):
        xbc = jax.nn.silu(centred_dwconv(xbc, conv_w) + conv_b)
        b, n = xbc.shape[:2]
        xs, bm, cm = jnp.split(xbc, [SSM_W, SSM_W + SSM_GROUPS * SSM_N], axis=-1)
        xs = xs.reshape(b, n, SSM_GROUPS, SSM_HPG, SSM_P).astype(jnp.float32)
        bm = bm.reshape(b, n, SSM_GROUPS, SSM_N).astype(jnp.float32)
        cm = cm.reshape(b, n, SSM_GROUPS, SSM_N).astype(jnp.float32)
        dt = dt.astype(jnp.float32)
        dts = [jax.nn.softplus(dt + dt_bias[d].astype(jnp.float32)).reshape(b, n, SSM_GROUPS, SSM_HPG)
               for d in range(2)]
        return xs, bm, cm, dts

    a = -jnp.exp(a_log.astype(jnp.float32)).reshape(2, SSM_GROUPS, SSM_HPG)
    xc, bc, cc, dtc = prep(xbc_c, dt_c)
    xl, bl, cl, dtl = prep(xbc_l, dt_l)
    h0 = jnp.zeros((xl.shape[0], SSM_GROUPS, SSM_HPG, SSM_P, SSM_N), jnp.float32)
    yc_f, hc_f = ssd_chunked(xc, dtc[0], a[0], bc, cc, h0)
    yc_b, hc_b = ssd_chunked(flip(xc), flip(dtc[1]), a[1], flip(bc), flip(cc), h0)
    yl_f, _ = ssd_chunked(xl, dtl[0], a[0], bl, cl, hc_f)
    yl_b, _ = ssd_chunked(flip(xl), flip(dtl[1]), a[1], flip(bl), flip(cl), hc_b)
    d_h = d_skip.astype(jnp.float32).reshape(SSM_GROUPS, SSM_HPG)[..., None]

    def finish(y_f, y_b, xs, z):
        b, n = z.shape[:2]
        y = (y_f + flip(y_b) + d_h * xs).reshape(b, n, SSM_W).astype(z.dtype) * jax.nn.silu(z)
        y = rmsnorm(y.reshape(b, n, SSM_GROUPS, SSM_W // SSM_GROUPS), norm_w.reshape(SSM_GROUPS, -1))
        return y.reshape(b, n, SSM_W)

    oc = finish(yc_f, yc_b, xc, zc) if with_ctx else None
    return oc, finish(yl_f, yl_b, xl, zl)


def gated_delta_chunked(q, k, v, g, beta, s0):
    b, n = q.shape[:2]
    nc = n // GDN_CHUNK

    def chunks(t):
        return jnp.moveaxis(t.reshape((b, nc, GDN_CHUNK) + t.shape[2:]), 2, 3)

    q, k, v, g, beta = (chunks(t) for t in (q, k, v, g, beta))
    gc = jnp.cumsum(g, axis=-1)
    kb = k * beta[..., None]
    vb = v * beta[..., None]
    incl = jnp.tril(jnp.ones((GDN_CHUNK, GDN_CHUNK), bool))
    strict = jnp.tril(jnp.ones((GDN_CHUNK, GDN_CHUNK), bool), -1)
    dec = jnp.exp(jnp.where(incl, gc[..., :, None] - gc[..., None, :], -jnp.inf))
    a_strict = jnp.where(strict, jnp.einsum('bchld,bchsd->bchls', kb, k) * dec, 0.0)
    eye = jnp.eye(GDN_CHUNK, dtype=a_strict.dtype)
    t_inv = lax.linalg.triangular_solve(a_strict + eye, jnp.broadcast_to(eye, a_strict.shape),
                                        left_side=True, lower=True)
    u = jnp.einsum('bchls,bchsv->bchlv', t_inv, vb)
    w = jnp.einsum('bchls,bchsd->bchld', t_inv, kb * jnp.exp(gc)[..., None])
    qk = jnp.where(incl, jnp.einsum('bchld,bchsd->bchls', q, k) * dec, 0.0)
    q_dec = q * jnp.exp(gc)[..., None]
    k_to_end = k * jnp.exp(gc[..., -1:] - gc)[..., None]
    g_end = jnp.exp(gc[..., -1])

    def step(state, inp):
        qd, kd, u_c, w_c, qk_c, ge = inp
        v_new = u_c - jnp.einsum('bhld,bhdv->bhlv', w_c, state)
        o = jnp.einsum('bhld,bhdv->bhlv', qd, state) + jnp.einsum('bhls,bhsv->bhlv', qk_c, v_new)
        state = state * ge[..., None, None] + jnp.einsum('bhld,bhlv->bhdv', kd, v_new)
        return state, o

    xs = tuple(jnp.moveaxis(t, 1, 0) for t in (q_dec, k_to_end, u, w, qk, g_end))
    s_final, o = lax.scan(step, s0, xs)
    o = jnp.moveaxis(jnp.moveaxis(o, 0, 1), 3, 2).reshape(b, n, GDN_HEADS, GDN_DV)
    return o, s_final


def gdn_mixer(qkv_c, gate_c, beta_c, a_c, qkv_l, gate_l, beta_l, a_l, with_ctx, conv_w, dt_bias, a_log, norm_w):
    def prep(qkv, beta, a):
        qkv = jax.nn.silu(centred_dwconv(qkv, conv_w))
        b, n = qkv.shape[:2]
        q, k, v = jnp.split(qkv, [GDN_HEADS * GDN_DK, 2 * GDN_HEADS * GDN_DK], axis=-1)
        q = l2norm(q.reshape(b, n, GDN_HEADS, GDN_DK)) * (GDN_DK ** -0.5)
        k = l2norm(k.reshape(b, n, GDN_HEADS, GDN_DK))
        v = v.reshape(b, n, GDN_HEADS, GDN_DV).astype(jnp.float32)
        beta = jax.nn.sigmoid(beta.astype(jnp.float32)).reshape(b, n, 2, GDN_HEADS)
        a = a.astype(jnp.float32).reshape(b, n, 2, GDN_HEADS)
        g = -jnp.exp(a_log.astype(jnp.float32)) * jax.nn.softplus(a + dt_bias.astype(jnp.float32))
        return q, k, v, g, beta

    qc, kc, vc, gcx, bcx = prep(qkv_c, beta_c, a_c)
    ql, kl, vl, glx, blx = prep(qkv_l, beta_l, a_l)
    s0 = jnp.zeros((ql.shape[0], GDN_HEADS, GDN_DK, GDN_DV), jnp.float32)
    oc_f, sc_f = gated_delta_chunked(qc, kc, vc, gcx[:, :, 0], bcx[:, :, 0], s0)
    oc_b, sc_b = gated_delta_chunked(flip(qc), flip(kc), flip(vc), flip(gcx[:, :, 1]), flip(bcx[:, :, 1]), s0)
    ol_f, _ = gated_delta_chunked(ql, kl, vl, glx[:, :, 0], blx[:, :, 0], sc_f)
    ol_b, _ = gated_delta_chunked(flip(ql), flip(kl), flip(vl), flip(glx[:, :, 1]), flip(blx[:, :, 1]), sc_b)

    def finish(o_f, o_b, gate):
        b, n = gate.shape[:2]
        o = rmsnorm(o_f + flip(o_b), norm_w) * jax.nn.silu(gate.reshape(b, n, GDN_HEADS, GDN_DV))
        return o.reshape(b, n, GDN_W)

    oc = finish(oc_f, oc_b, gate_c) if with_ctx else None
    return oc, finish(ol_f, ol_b, gate_l)


def token_mixers(hc, hl, cos, sin, lam_init, with_ctx,
                 diff_qn_w, diff_kn_w, diff_lq1, diff_lk1, diff_lq2, diff_lk2, diff_norm_w,
                 ssm_conv_w, ssm_conv_b, ssm_dt_bias, ssm_a_log, ssm_d, ssm_norm_w,
                 gdn_conv_w, gdn_dt_bias, gdn_a_log, gdn_norm_w):
    idx = [int(i) for i in np.cumsum(SPLIT_SIZES)[:-1]]
    pc = jnp.split(hc, idx, axis=-1)
    pl_ = jnp.split(hl, idx, axis=-1)
    ac, al = diff_attention(pc[0], pl_[0], cos, sin, lam_init, with_ctx, diff_qn_w, diff_kn_w,
                            diff_lq1, diff_lk1, diff_lq2, diff_lk2, diff_norm_w)
    sc, sl = ssm_mixer(pc[1], pc[2], pc[3], pl_[1], pl_[2], pl_[3], with_ctx, ssm_conv_w, ssm_conv_b,
                       ssm_dt_bias, ssm_a_log, ssm_d, ssm_norm_w)
    gc, gl = gdn_mixer(pc[4], pc[5], pc[6], pc[7], pl_[4], pl_[5], pl_[6], pl_[7], with_ctx, gdn_conv_w,
                       gdn_dt_bias, gdn_a_log, gdn_norm_w)
    dt = hl.dtype
    ml = jnp.concatenate([al.astype(dt), sl.astype(dt), gl.astype(dt)], axis=-1)
    mc = jnp.concatenate([ac.astype(dt), sc.astype(dt), gc.astype(dt)], axis=-1) if with_ctx else None
    return mc, ml


def routed_experts(t, experts, gates, w_gate, w_up, w_down):
    n, d = t.shape
    n_assign = n * TOP_K
    flat_e = experts.reshape(n_assign)
    order = jnp.argsort(flat_e)
    sorted_e = flat_e[order]
    counts = jax.ops.segment_sum(jnp.ones((n_assign,), jnp.int32), flat_e, num_segments=N_EXPERTS)
    starts = jnp.cumsum(counts) - counts
    padded = (counts + MOE_BLOCK - 1) // MOE_BLOCK * MOE_BLOCK
    pad_ends = jnp.cumsum(padded)
    pad_starts = pad_ends - padded
    dest = pad_starts[sorted_e] + jnp.arange(n_assign, dtype=jnp.int32) - starts[sorted_e]
    n_blocks = -(-n_assign // MOE_BLOCK) + N_EXPERTS
    buf = jnp.zeros((n_blocks * MOE_BLOCK, d), t.dtype).at[dest].set(t[order // TOP_K])
    block_start = jnp.arange(n_blocks, dtype=jnp.int32) * MOE_BLOCK
    block_e = jnp.minimum(jnp.searchsorted(pad_ends, block_start, side='right'), N_EXPERTS - 1)

    def expert_block(args):
        xb, e = args
        hb = jax.nn.silu(xb @ w_gate[e]) * (xb @ w_up[e])
        return hb @ w_down[e]

    out = lax.map(expert_block, (buf.reshape(n_blocks, MOE_BLOCK, d), block_e)).reshape(-1, d)
    y_assign = jnp.zeros((n_assign, d), out.dtype).at[order].set(out[dest])
    return jnp.sum(y_assign.reshape(n, TOP_K, d) * gates[..., None].astype(out.dtype), axis=1)


def hier_moe(t, router_g_w, router_g_b, router_e_w, router_e_b, w_gate, w_up, w_down):
    n = t.shape[0]
    grp_prob = jax.nn.softmax(jnp.dot(t, router_g_w, preferred_element_type=jnp.float32)
                              + router_g_b.astype(jnp.float32), axis=-1)
    p_grp, grp = lax.top_k(grp_prob, 1)
    e_logits = (jnp.dot(t, router_e_w, preferred_element_type=jnp.float32)
                + router_e_b.astype(jnp.float32)).reshape(n, N_EGROUPS, EXPERTS_PER_GROUP)
    sel = e_logits[jnp.arange(n), grp[:, 0]]
    p_top, idx = lax.top_k(jax.nn.softmax(sel, axis=-1), TOP_K)
    gates = p_grp * p_top / jnp.sum(p_top, axis=-1, keepdims=True)
    experts = grp * EXPERTS_PER_GROUP + idx
    return routed_experts(t, experts, gates, w_gate, w_up, w_down)


def _matmul_kernel(a_ref, b_ref, o_ref):
    o_ref[...] = jnp.dot(a_ref[...].astype(jnp.bfloat16), b_ref[...],
                         preferred_element_type=jnp.float32)


def _matmul(a, b, tm, tn):
    m, k = a.shape
    n = b.shape[1]
    return pl.pallas_call(
        _matmul_kernel,
        grid=(m // tm, n // tn),
        in_specs=[pl.BlockSpec((tm, k), lambda i, j: (i, 0)),
                  pl.BlockSpec((k, tn), lambda i, j: (0, j))],
        out_specs=pl.BlockSpec((tm, tn), lambda i, j: (i, j)),
        out_shape=jax.ShapeDtypeStruct((m, n), jnp.float32),
        compiler_params=pltpu.CompilerParams(dimension_semantics=("parallel", "parallel")),
    )(a, b)


def _dense(a, w, tm=512):
    lead = a.shape[:-1]
    k = a.shape[-1]
    n = w.shape[1]
    a2 = a.reshape(-1, k)
    m = a2.shape[0]
    m_pad = -(-m // 8) * 8
    if m_pad != m:
        a2 = jnp.pad(a2, ((0, m_pad - m), (0, 0)))
    tm = min(tm, m_pad)
    tn = 512
    n_pad = -(-n // tn) * tn
    wb = jnp.pad(w, ((0, 0), (0, n_pad - n))).astype(jnp.bfloat16)
    out = _matmul(a2, wb, tm, tn)[:m, :n]
    return out.reshape(lead + (n,))


def kernel(x, c, ctx, c_ctx, w_mod, b_mod, norm1_w, norm2_w, w_in, w_out,
           diff_qn_w, diff_kn_w, diff_lq1, diff_lk1, diff_lq2, diff_lk2, diff_norm_w,
           ssm_conv_w, ssm_conv_b, ssm_dt_bias, ssm_a_log, ssm_d, ssm_norm_w,
           gdn_conv_w, gdn_dt_bias, gdn_a_log, gdn_norm_w,
           router_g_w, router_g_b, router_e_w, router_e_b, exp_w_gate, exp_w_up, exp_w_down):
    rows = x.shape[1] // GRID_W
    cos, sin = axial_rope_tables(rows)
    xl, xc = x, ctx
    for l in range(DEPTH):
        last = l == DEPTH - 1
        lam_init = 0.8 - 0.6 * math.exp(-0.3 * l)
        cc = jnp.concatenate([c, c_ctx[None, :]], axis=0)
        mod = _dense(jax.nn.silu(cc), w_mod[l]) + b_mod[l]
        mod_l = [m[:, None, :] for m in jnp.split(mod[:BATCH], MOD_CHUNKS, axis=-1)]
        mod_c = jnp.split(mod[BATCH], MOD_CHUNKS, axis=-1)
        hl = _dense(modulate(rmsnorm(xl, norm1_w[l]), mod_l[0], mod_l[1]), w_in[l])
        hc = _dense(modulate(rmsnorm(xc, norm1_w[l]), mod_c[0], mod_c[1]), w_in[l])
        mc, ml = token_mixers(hc, hl, cos, sin, lam_init, not last,
                              diff_qn_w[l], diff_kn_w[l], diff_lq1[l], diff_lk1[l], diff_lq2[l], diff_lk2[l],
                              diff_norm_w[l], ssm_conv_w[l], ssm_conv_b[l], ssm_dt_bias[l], ssm_a_log[l],
                              ssm_d[l], ssm_norm_w[l], gdn_conv_w[l], gdn_dt_bias[l], gdn_a_log[l], gdn_norm_w[l])
        xl = xl + mod_l[2] * _dense(ml, w_out[l])
        fl = modulate(rmsnorm(xl, norm2_w[l]), mod_l[3], mod_l[4]).reshape(-1, D_MODEL)
        moe_args = (router_g_w[l], router_g_b[l], router_e_w[l], router_e_b[l],
                    exp_w_gate[l], exp_w_up[l], exp_w_down[l])
        if last:
            yl = hier_moe(fl, *moe_args)
        else:
            xc = xc + mod_c[2] * _dense(mc, w_out[l])
            fc = modulate(rmsnorm(xc, norm2_w[l]), mod_c[3], mod_c[4]).reshape(-1, D_MODEL)
            y = hier_moe(jnp.concatenate([fl, fc], axis=0), *moe_args)
            yl = y[:fl.shape[0]]
            xc = xc + mod_c[5] * y[fl.shape[0]:].reshape(xc.shape)
        xl = xl + mod_l[5] * yl.reshape(xl.shape)
    return xl
```

```python
import functools
import math
import jax
import jax.numpy as jnp
from jax import lax
import numpy as np
from jax.experimental import pallas as pl
from jax.experimental.pallas import tpu as pltpu

D_MODEL = 1024
BATCH = 8
SEQ = 4096
DEPTH = 2
CTX_LEN = 256
GRID_W = 64
EPS = 1e-6
MOD_CHUNKS = 6

DIFF_HEADS = 4
DIFF_QK = 32
DIFF_V = 2 * DIFF_QK
DIFF_W = DIFF_HEADS * DIFF_V
ROPE_BASE = 10000.0
Q_BLOCK = 128

SSM_HEADS = 8
SSM_P = 64
SSM_GROUPS = 2
SSM_HPG = SSM_HEADS // SSM_GROUPS
SSM_N = 64
SSM_W = SSM_HEADS * SSM_P
SSM_XBC = SSM_W + 2 * SSM_GROUPS * SSM_N
SSD_CHUNK = 128
CONV_K = 5

GDN_HEADS = 4
GDN_DK = 64
GDN_DV = 64
GDN_QKV = GDN_HEADS * (2 * GDN_DK + GDN_DV)
GDN_W = GDN_HEADS * GDN_DV
GDN_CHUNK = 64

D_MIX = DIFF_W + SSM_W + GDN_W
SPLIT_SIZES = (3 * DIFF_W, SSM_W, SSM_XBC, SSM_HEADS, GDN_QKV, GDN_W, 2 * GDN_HEADS, 2 * GDN_HEADS)
IN_DIM = sum(SPLIT_SIZES)

N_EGROUPS = 4
EXPERTS_PER_GROUP = 8
N_EXPERTS = N_EGROUPS * EXPERTS_PER_GROUP
TOP_K = 2
D_EXPERT = 512
MOE_BLOCK = 256


def rmsnorm(x, w):
    x32 = x.astype(jnp.float32)
    y = x32 * lax.rsqrt(jnp.mean(x32 * x32, axis=-1, keepdims=True) + EPS)
    return y.astype(x.dtype) * w


def l2norm(x):
    x32 = x.astype(jnp.float32)
    return x32 * lax.rsqrt(jnp.sum(x32 * x32, axis=-1, keepdims=True) + EPS)


def modulate(x, shift, scale):
    return x * (1.0 + scale) + shift


def flip(t):
    return jnp.flip(t, axis=1)


def centred_dwconv(x, w):
    pad = CONV_K // 2
    return lax.conv_general_dilated(
        x, w[:, None, :].astype(x.dtype), window_strides=(1,), padding=[(pad, pad)],
        dimension_numbers=('NWC', 'WIO', 'NWC'), feature_group_count=x.shape[-1])


def axial_rope_tables(rows):
    row = jnp.repeat(jnp.arange(rows, dtype=jnp.float32), GRID_W)
    col = jnp.tile(jnp.arange(GRID_W, dtype=jnp.float32), rows)
    half = DIFF_QK // 2
    inv = ROPE_BASE ** (-jnp.arange(0, half, 2, dtype=jnp.float32) / half)
    ang_r = row[:, None] * inv
    ang_c = col[:, None] * inv
    ang = jnp.concatenate([ang_r, ang_r, ang_c, ang_c], axis=-1)
    return jnp.cos(ang), jnp.sin(ang)


def apply_axial_rope(x, cos, sin):
    a1, a2, b1, b2 = jnp.split(x, 4, axis=-1)
    rot = jnp.concatenate([-a2, a1, -b2, b1], axis=-1)
    cs = cos[None, :, None, None, :]
    sn = sin[None, :, None, None, :]
    return (x * cs + rot * sn).astype(x.dtype)


TOK_TILE = 256
LOG2E = 1.4426950408889634


def _group_mean_sq(x, g_ref):
    x2 = x * x
    hi = x2.astype(jnp.bfloat16)
    lo = (x2 - hi.astype(jnp.float32)).astype(jnp.bfloat16)
    g = g_ref[...]
    return (jnp.dot(hi, g, preferred_element_type=jnp.float32)
            + jnp.dot(lo, g, preferred_element_type=jnp.float32))


def _attn_prep_kernel(qkv_ref, cos_ref, sin_ref, wq_ref, wk_ref, g_ref, q_out, kt_out, v_out):
    x = qkv_ref[0]
    cos = cos_ref[...]
    sin = sin_ref[...]
    lane = lax.broadcasted_iota(jnp.int32, cos.shape, 1)
    lo_half = (lane % 16) < 8

    def norm_rope(t, w):
        y = t * lax.rsqrt(_group_mean_sq(t, g_ref) + EPS) * w
        rot = jnp.where(lo_half, pltpu.roll(y, DIFF_W - 8, 1), pltpu.roll(y, 8, 1))
        return y * cos + rot * sin

    q = norm_rope(x[:, 0:DIFF_W], wq_ref[...]) * (DIFF_QK ** -0.5 * LOG2E)
    k = norm_rope(x[:, DIFF_W:2 * DIFF_W], wk_ref[...])
    q_out[0] = q
    kt_out[0] = k.T.astype(jnp.bfloat16)
    v_out[0] = x[:, 2 * DIFF_W:3 * DIFF_W].astype(jnp.bfloat16)


def _attn_prep(qkv, cos_t, sin_t, wq, wk, gmat):
    b, n, _ = qkv.shape
    nt = n // TOK_TILE
    return pl.pallas_call(
        _attn_prep_kernel,
        grid=(b, nt),
        in_specs=[pl.BlockSpec((1, TOK_TILE, 3 * DIFF_W), lambda i, j: (i, j, 0)),
                  pl.BlockSpec((TOK_TILE, DIFF_W), lambda i, j: (j, 0)),
                  pl.BlockSpec((TOK_TILE, DIFF_W), lambda i, j: (j, 0)),
                  pl.BlockSpec((1, DIFF_W), lambda i, j: (0, 0)),
                  pl.BlockSpec((1, DIFF_W), lambda i, j: (0, 0)),
                  pl.BlockSpec((DIFF_W, DIFF_W), lambda i, j: (0, 0))],
        out_specs=[pl.BlockSpec((1, TOK_TILE, DIFF_W), lambda i, j: (i, j, 0)),
                   pl.BlockSpec((1, DIFF_W, TOK_TILE), lambda i, j: (i, 0, j)),
                   pl.BlockSpec((1, TOK_TILE, DIFF_W), lambda i, j: (i, j, 0))],
        out_shape=[jax.ShapeDtypeStruct((b, n, DIFF_W), jnp.float32),
                   jax.ShapeDtypeStruct((b, DIFF_W, n), jnp.bfloat16),
                   jax.ShapeDtypeStruct((b, n, DIFF_W), jnp.bfloat16)],
        compiler_params=pltpu.CompilerParams(dimension_semantics=("parallel", "parallel")),
        name="attn_prep",
    )(qkv, cos_t, sin_t, wq, wk, gmat)


def _attn_kernel(lam_ref, q_ref, kt_ref, v_ref, wo_ref, g_ref, o_ref, s_ref, p_ref, acc_ref, *, n_keys, out_scale):
    tile = pl.program_id(1)
    lam = lam_ref[0]
    lane = lax.broadcasted_iota(jnp.int32, (TOK_TILE, DIFF_W), 1)

    def attend(nk):
        n_chunks = nk // TOK_TILE
        q = q_ref[0]
        acc_ref[...] = jnp.zeros_like(acc_ref)

        def unit(u, carry):
            qm = jnp.where((lane >= u * DIFF_QK) & (lane < (u + 1) * DIFF_QK), q, 0.0).astype(jnp.bfloat16)
            mx = None
            for c in range(n_chunks):
                s = jnp.dot(qm, kt_ref[0, :, c * TOK_TILE:(c + 1) * TOK_TILE],
                            preferred_element_type=jnp.float32)
                s_ref[:, c * TOK_TILE:(c + 1) * TOK_TILE] = s
                sm = jnp.maximum(s[:, :128], s[:, 128:])
                mx = sm if mx is None else jnp.maximum(mx, sm)
            m = jnp.max(mx, axis=-1, keepdims=True)
            ls = None
            for c in range(n_chunks):
                e = jnp.exp2(s_ref[:, c * TOK_TILE:(c + 1) * TOK_TILE] - m)
                p_ref[:, c * TOK_TILE:(c + 1) * TOK_TILE] = e.astype(jnp.bfloat16)
                es = e[:, :128] + e[:, 128:]
                ls = es if ls is None else ls + es
            l = jnp.sum(ls, axis=-1, keepdims=True)
            pv = jnp.dot(p_ref[:, :nk], v_ref[0, :nk, :], preferred_element_type=jnp.float32)
            coef = jnp.where(u % 2 == 0, 1.0, -lam) / l
            head = u // 2
            in_head = (lane >= head * DIFF_V) & (lane < (head + 1) * DIFF_V)
            acc_ref[...] += jnp.where(in_head, pv * coef, 0.0)
            return carry

        lax.fori_loop(0, 2 * DIFF_HEADS, unit, 0)
        o = acc_ref[...]
        y = o * lax.rsqrt(_group_mean_sq(o, g_ref) + EPS)
        o_ref[0] = y * wo_ref[...] * out_scale

    @pl.when(tile == 0)
    def _():
        attend(TOK_TILE)

    @pl.when(tile > 0)
    def _():
        attend(n_keys)


def _attention(lam, q, kt, v, wo, gmat, out_scale):
    b, n, _ = q.shape
    nt = n // TOK_TILE
    kern = functools.partial(_attn_kernel, n_keys=n, out_scale=out_scale)
    return pl.pallas_call(
        kern,
        grid=(b, nt),
        in_specs=[pl.BlockSpec(memory_space=pltpu.SMEM),
                  pl.BlockSpec((1, TOK_TILE, DIFF_W), lambda i, j: (i, j, 0)),
                  pl.BlockSpec((1, DIFF_W, n), lambda i, j: (i, 0, 0)),
                  pl.BlockSpec((1, n, DIFF_W), lambda i, j: (i, 0, 0)),
                  pl.BlockSpec((1, DIFF_W), lambda i, j: (0, 0)),
                  pl.BlockSpec((DIFF_W, DIFF_W), lambda i, j: (0, 0))],
        out_specs=pl.BlockSpec((1, TOK_TILE, DIFF_W), lambda i, j: (i, j, 0)),
        out_shape=jax.ShapeDtypeStruct((b, n, DIFF_W), jnp.float32),
        scratch_shapes=[pltpu.VMEM((TOK_TILE, n), jnp.float32),
                        pltpu.VMEM((TOK_TILE, n), jnp.bfloat16),
                        pltpu.VMEM((TOK_TILE, DIFF_W), jnp.float32)],
        compiler_params=pltpu.CompilerParams(dimension_semantics=("parallel", "arbitrary"),
                                             vmem_limit_bytes=48 * 1024 * 1024),
        name="diff_attn",
    )(lam, q, kt, v, wo, gmat)


def _group_avg_matrix(width, group):
    idx = np.arange(width) // group
    return jnp.asarray((idx[:, None] == idx[None, :]).astype(np.float32) / group, jnp.bfloat16)


def _rope_tables(n_latent):
    rows = n_latent // GRID_W
    row = np.repeat(np.arange(rows, dtype=np.float32), GRID_W)
    col = np.tile(np.arange(GRID_W, dtype=np.float32), rows)
    half = DIFF_QK // 2
    inv = (ROPE_BASE ** (-np.arange(0, half, 2, dtype=np.float32) / half)).astype(np.float32)
    ang = np.concatenate([row[:, None] * inv, row[:, None] * inv, col[:, None] * inv, col[:, None] * inv], axis=-1)
    cos = np.concatenate([np.ones((TOK_TILE, DIFF_QK), np.float32), np.cos(ang)], axis=0)
    sin = np.concatenate([np.zeros((TOK_TILE, DIFF_QK), np.float32), np.sin(ang)], axis=0)
    sign = np.where((np.arange(DIFF_QK) % 16) < 8, -1.0, 1.0).astype(np.float32)
    reps = DIFF_W // DIFF_QK
    return jnp.asarray(np.tile(cos, (1, reps))), jnp.asarray(np.tile(sin * sign, (1, reps)))


def diff_attention_pallas(qkv, lam_init, qn_w, kn_w, lq1, lk1, lq2, lk2, out_w):
    n = qkv.shape[1]
    cos_t, sin_t = _rope_tables(n - TOK_TILE)
    g32 = _group_avg_matrix(DIFF_W, DIFF_QK)
    g64 = _group_avg_matrix(DIFF_W, DIFF_V)
    reps = DIFF_W // DIFF_QK
    q, kt, v = _attn_prep(qkv, cos_t, sin_t, jnp.tile(qn_w, reps)[None, :], jnp.tile(kn_w, reps)[None, :], g32)
    lam = (jnp.exp(jnp.sum(lq1 * lk1)) - jnp.exp(jnp.sum(lq2 * lk2)) + lam_init).reshape(1).astype(jnp.float32)
    return _attention(lam, q, kt, v, jnp.tile(out_w, DIFF_HEADS)[None, :], g64, 1.0 - lam_init)


def ssd_chunked(xs, dt, a, bm, cm, h0):
    b, n = xs.shape[:2]
    nc = n // SSD_CHUNK
    xdt = (xs * dt[..., None]).reshape(b, nc, SSD_CHUNK, SSM_GROUPS, SSM_HPG, SSM_P)
    acum = jnp.cumsum((dt * a).reshape(b, nc, SSD_CHUNK, SSM_GROUPS, SSM_HPG), axis=2)
    bc = bm.reshape(b, nc, SSD_CHUNK, SSM_GROUPS, SSM_N)
    cc = cm.reshape(b, nc, SSD_CHUNK, SSM_GROUPS, SSM_N)
    incl = jnp.tril(jnp.ones((SSD_CHUNK, SSD_CHUNK), bool))[:, :, None, None]
    seg = acum[:, :, :, None] - acum[:, :, None, :]
    lmat = jnp.exp(jnp.where(incl, seg, -jnp.inf))
    cb = jnp.einsum('bclgn,bcsgn->bclsg', cc, bc)
    y_diag = jnp.einsum('bclsg,bclsgr,bcsgrp->bclgrp', cb, lmat, xdt)
    states = jnp.einsum('bclgn,bclgr,bclgrp->bcgrpn', bc, jnp.exp(acum[:, :, -1:] - acum), xdt)
    chunk_decay = jnp.exp(acum[:, :, -1])

    def step(h, inp):
        st, dec = inp
        return h * dec[..., None, None] + st, h

    h_final, h_in = lax.scan(step, h0, (jnp.moveaxis(states, 1, 0), jnp.moveaxis(chunk_decay, 1, 0)))
    y_off = jnp.einsum('bclgn,cbgrpn,bclgr->bclgrp', cc, h_in, jnp.exp(acum))
    return (y_diag + y_off).reshape(b, n, SSM_GROUPS, SSM_HPG, SSM_P), h_final


def ssm_mixer(zc, xbc_c, dt_c, zl, xbc_l, dt_l, with_ctx, conv_w, conv_b, dt_bias, a_log, d_skip, norm_w):
    def prep(xbc, dt):
        xbc = jax.nn.silu(centred_dwconv(xbc, conv_w) + conv_b)
        b, n = xbc.shape[:2]
        xs, bm, cm = jnp.split(xbc, [SSM_W, SSM_W + SSM_GROUPS * SSM_N], axis=-1)
        xs = xs.reshape(b, n, SSM_GROUPS, SSM_HPG, SSM_P).astype(jnp.float32)
        bm = bm.reshape(b, n, SSM_GROUPS, SSM_N).astype(jnp.float32)
        cm = cm.reshape(b, n, SSM_GROUPS, SSM_N).astype(jnp.float32)
        dt = dt.astype(jnp.float32)
        dts = [jax.nn.softplus(dt + dt_bias[d].astype(jnp.float32)).reshape(b, n, SSM_GROUPS, SSM_HPG)
               for d in range(2)]
        return xs, bm, cm, dts

    a = -jnp.exp(a_log.astype(jnp.float32)).reshape(2, SSM_GROUPS, SSM_HPG)
    xc, bc, cc, dtc = prep(xbc_c, dt_c)
    xl, bl, cl, dtl = prep(xbc_l, dt_l)
    h0 = jnp.zeros((xl.shape[0], SSM_GROUPS, SSM_HPG, SSM_P, SSM_N), jnp.float32)
    yc_f, hc_f = ssd_chunked(xc, dtc[0], a[0], bc, cc, h0)
    yc_b, hc_b = ssd_chunked(flip(xc), flip(dtc[1]), a[1], flip(bc), flip(cc), h0)
    yl_f, _ = ssd_chunked(xl, dtl[0], a[0], bl, cl, hc_f)
    yl_b, _ = ssd_chunked(flip(xl), flip(dtl[1]), a[1], flip(bl), flip(cl), hc_b)
    d_h = d_skip.astype(jnp.float32).reshape(SSM_GROUPS, SSM_HPG)[..., None]

    def finish(y_f, y_b, xs, z):
        b, n = z.shape[:2]
        y = (y_f + flip(y_b) + d_h * xs).reshape(b, n, SSM_W).astype(z.dtype) * jax.nn.silu(z)
        y = rmsnorm(y.reshape(b, n, SSM_GROUPS, SSM_W // SSM_GROUPS), norm_w.reshape(SSM_GROUPS, -1))
        return y.reshape(b, n, SSM_W)

    oc = finish(yc_f, yc_b, xc, zc) if with_ctx else None
    return oc, finish(yl_f, yl_b, xl, zl)


def gated_delta_chunked(q, k, v, g, beta, s0):
    b, n = q.shape[:2]
    nc = n // GDN_CHUNK

    def chunks(t):
        return jnp.moveaxis(t.reshape((b, nc, GDN_CHUNK) + t.shape[2:]), 2, 3)

    q, k, v, g, beta = (chunks(t) for t in (q, k, v, g, beta))
    gc = jnp.cumsum(g, axis=-1)
    kb = k * beta[..., None]
    vb = v * beta[..., None]
    incl = jnp.tril(jnp.ones((GDN_CHUNK, GDN_CHUNK), bool))
    strict = jnp.tril(jnp.ones((GDN_CHUNK, GDN_CHUNK), bool), -1)
    dec = jnp.exp(jnp.where(incl, gc[..., :, None] - gc[..., None, :], -jnp.inf))
    a_strict = jnp.where(strict, jnp.einsum('bchld,bchsd->bchls', kb, k) * dec, 0.0)
    eye = jnp.eye(GDN_CHUNK, dtype=a_strict.dtype)
    t_inv = lax.linalg.triangular_solve(a_strict + eye, jnp.broadcast_to(eye, a_strict.shape),
                                        left_side=True, lower=True)
    u = jnp.einsum('bchls,bchsv->bchlv', t_inv, vb)
    w = jnp.einsum('bchls,bchsd->bchld', t_inv, kb * jnp.exp(gc)[..., None])
    qk = jnp.where(incl, jnp.einsum('bchld,bchsd->bchls', q, k) * dec, 0.0)
    q_dec = q * jnp.exp(gc)[..., None]
    k_to_end = k * jnp.exp(gc[..., -1:] - gc)[..., None]
    g_end = jnp.exp(gc[..., -1])

    def step(state, inp):
        qd, kd, u_c, w_c, qk_c, ge = inp
        v_new = u_c - jnp.einsum('bhld,bhdv->bhlv', w_c, state)
        o = jnp.einsum('bhld,bhdv->bhlv', qd, state) + jnp.einsum('bhls,bhsv->bhlv', qk_c, v_new)
        state = state * ge[..., None, None] + jnp.einsum('bhld,bhlv->bhdv', kd, v_new)
        return state, o

    xs = tuple(jnp.moveaxis(t, 1, 0) for t in (q_dec, k_to_end, u, w, qk, g_end))
    s_final, o = lax.scan(step, s0, xs)
    o = jnp.moveaxis(jnp.moveaxis(o, 0, 1), 3, 2).reshape(b, n, GDN_HEADS, GDN_DV)
    return o, s_final


def gdn_mixer(qkv_c, gate_c, beta_c, a_c, qkv_l, gate_l, beta_l, a_l, with_ctx, conv_w, dt_bias, a_log, norm_w):
    def prep(qkv, beta, a):
        qkv = jax.nn.silu(centred_dwconv(qkv, conv_w))
        b, n = qkv.shape[:2]
        q, k, v = jnp.split(qkv, [GDN_HEADS * GDN_DK, 2 * GDN_HEADS * GDN_DK], axis=-1)
        q = l2norm(q.reshape(b, n, GDN_HEADS, GDN_DK)) * (GDN_DK ** -0.5)
        k = l2norm(k.reshape(b, n, GDN_HEADS, GDN_DK))
        v = v.reshape(b, n, GDN_HEADS, GDN_DV).astype(jnp.float32)
        beta = jax.nn.sigmoid(beta.astype(jnp.float32)).reshape(b, n, 2, GDN_HEADS)
        a = a.astype(jnp.float32).reshape(b, n, 2, GDN_HEADS)
        g = -jnp.exp(a_log.astype(jnp.float32)) * jax.nn.softplus(a + dt_bias.astype(jnp.float32))
        return q, k, v, g, beta

    qc, kc, vc, gcx, bcx = prep(qkv_c, beta_c, a_c)
    ql, kl, vl, glx, blx = prep(qkv_l, beta_l, a_l)
    s0 = jnp.zeros((ql.shape[0], GDN_HEADS, GDN_DK, GDN_DV), jnp.float32)
    oc_f, sc_f = gated_delta_chunked(qc, kc, vc, gcx[:, :, 0], bcx[:, :, 0], s0)
    oc_b, sc_b = gated_delta_chunked(flip(qc), flip(kc), flip(vc), flip(gcx[:, :, 1]), flip(bcx[:, :, 1]), s0)
    ol_f, _ = gated_delta_chunked(ql, kl, vl, glx[:, :, 0], blx[:, :, 0], sc_f)
    ol_b, _ = gated_delta_chunked(flip(ql), flip(kl), flip(vl), flip(glx[:, :, 1]), flip(blx[:, :, 1]), sc_b)

    def finish(o_f, o_b, gate):
        b, n = gate.shape[:2]
        o = rmsnorm(o_f + flip(o_b), norm_w) * jax.nn.silu(gate.reshape(b, n, GDN_HEADS, GDN_DV))
        return o.reshape(b, n, GDN_W)

    oc = finish(oc_f, oc_b, gate_c) if with_ctx else None
    return oc, finish(ol_f, ol_b, gate_l)


def token_mixers(hc, hl, cos, sin, lam_init, with_ctx,
                 diff_qn_w, diff_kn_w, diff_lq1, diff_lk1, diff_lq2, diff_lk2, diff_norm_w,
                 ssm_conv_w, ssm_conv_b, ssm_dt_bias, ssm_a_log, ssm_d, ssm_norm_w,
                 gdn_conv_w, gdn_dt_bias, gdn_a_log, gdn_norm_w):
    idx = [int(i) for i in np.cumsum(SPLIT_SIZES)[:-1]]
    pc = jnp.split(hc, idx, axis=-1)
    pl_ = jnp.split(hl, idx, axis=-1)
    att = diff_attention_pallas(jnp.concatenate([pc[0], pl_[0]], axis=1), lam_init, diff_qn_w, diff_kn_w,
                                diff_lq1, diff_lk1, diff_lq2, diff_lk2, diff_norm_w)
    ac, al = att[:, :CTX_LEN], att[:, CTX_LEN:]
    sc, sl = ssm_mixer(pc[1], pc[2], pc[3], pl_[1], pl_[2], pl_[3], with_ctx, ssm_conv_w, ssm_conv_b,
                       ssm_dt_bias, ssm_a_log, ssm_d, ssm_norm_w)
    gc, gl = gdn_mixer(pc[4], pc[5], pc[6], pc[7], pl_[4], pl_[5], pl_[6], pl_[7], with_ctx, gdn_conv_w,
                       gdn_dt_bias, gdn_a_log, gdn_norm_w)
    dt = hl.dtype
    ml = jnp.concatenate([al.astype(dt), sl.astype(dt), gl.astype(dt)], axis=-1)
    mc = jnp.concatenate([ac.astype(dt), sc.astype(dt), gc.astype(dt)], axis=-1) if with_ctx else None
    return mc, ml


def routed_experts(t, experts, gates, w_gate, w_up, w_down):
    n, d = t.shape
    n_assign = n * TOP_K
    flat_e = experts.reshape(n_assign)
    order = jnp.argsort(flat_e)
    sorted_e = flat_e[order]
    counts = jax.ops.segment_sum(jnp.ones((n_assign,), jnp.int32), flat_e, num_segments=N_EXPERTS)
    starts = jnp.cumsum(counts) - counts
    padded = (counts + MOE_BLOCK - 1) // MOE_BLOCK * MOE_BLOCK
    pad_ends = jnp.cumsum(padded)
    pad_starts = pad_ends - padded
    dest = pad_starts[sorted_e] + jnp.arange(n_assign, dtype=jnp.int32) - starts[sorted_e]
    n_blocks = -(-n_assign // MOE_BLOCK) + N_EXPERTS
    buf = jnp.zeros((n_blocks * MOE_BLOCK, d), t.dtype).at[dest].set(t[order // TOP_K])
    block_start = jnp.arange(n_blocks, dtype=jnp.int32) * MOE_BLOCK
    block_e = jnp.minimum(jnp.searchsorted(pad_ends, block_start, side='right'), N_EXPERTS - 1)

    def expert_block(args):
        xb, e = args
        hb = jax.nn.silu(xb @ w_gate[e]) * (xb @ w_up[e])
        return hb @ w_down[e]

    out = lax.map(expert_block, (buf.reshape(n_blocks, MOE_BLOCK, d), block_e)).reshape(-1, d)
    y_assign = jnp.zeros((n_assign, d), out.dtype).at[order].set(out[dest])
    return jnp.sum(y_assign.reshape(n, TOP_K, d) * gates[..., None].astype(out.dtype), axis=1)


def hier_moe(t, router_g_w, router_g_b, router_e_w, router_e_b, w_gate, w_up, w_down):
    n = t.shape[0]
    grp_prob = jax.nn.softmax(jnp.dot(t, router_g_w, preferred_element_type=jnp.float32)
                              + router_g_b.astype(jnp.float32), axis=-1)
    p_grp, grp = lax.top_k(grp_prob, 1)
    e_logits = (jnp.dot(t, router_e_w, preferred_element_type=jnp.float32)
                + router_e_b.astype(jnp.float32)).reshape(n, N_EGROUPS, EXPERTS_PER_GROUP)
    sel = e_logits[jnp.arange(n), grp[:, 0]]
    p_top, idx = lax.top_k(jax.nn.softmax(sel, axis=-1), TOP_K)
    gates = p_grp * p_top / jnp.sum(p_top, axis=-1, keepdims=True)
    experts = grp * EXPERTS_PER_GROUP + idx
    return routed_experts(t, experts, gates, w_gate, w_up, w_down)


def _matmul_kernel(a_ref, b_ref, o_ref):
    o_ref[...] = jnp.dot(a_ref[...].astype(jnp.bfloat16), b_ref[...],
                         preferred_element_type=jnp.float32)


def _matmul(a, b, tm, tn):
    m, k = a.shape
    n = b.shape[1]
    return pl.pallas_call(
        _matmul_kernel,
        grid=(m // tm, n // tn),
        in_specs=[pl.BlockSpec((tm, k), lambda i, j: (i, 0)),
                  pl.BlockSpec((k, tn), lambda i, j: (0, j))],
        out_specs=pl.BlockSpec((tm, tn), lambda i, j: (i, j)),
        out_shape=jax.ShapeDtypeStruct((m, n), jnp.float32),
        compiler_params=pltpu.CompilerParams(dimension_semantics=("parallel", "parallel")),
    )(a, b)


def _dense(a, w, tm=512):
    lead = a.shape[:-1]
    k = a.shape[-1]
    n = w.shape[1]
    a2 = a.reshape(-1, k)
    m = a2.shape[0]
    m_pad = -(-m // 8) * 8
    if m_pad != m:
        a2 = jnp.pad(a2, ((0, m_pad - m), (0, 0)))
    tm = min(tm, m_pad)
    tn = 512
    n_pad = -(-n // tn) * tn
    wb = jnp.pad(w, ((0, 0), (0, n_pad - n))).astype(jnp.bfloat16)
    out = _matmul(a2, wb, tm, tn)[:m, :n]
    return out.reshape(lead + (n,))


def kernel(x, c, ctx, c_ctx, w_mod, b_mod, norm1_w, norm2_w, w_in, w_out,
           diff_qn_w, diff_kn_w, diff_lq1, diff_lk1, diff_lq2, diff_lk2, diff_norm_w,
           ssm_conv_w, ssm_conv_b, ssm_dt_bias, ssm_a_log, ssm_d, ssm_norm_w,
           gdn_conv_w, gdn_dt_bias, gdn_a_log, gdn_norm_w,
           router_g_w, router_g_b, router_e_w, router_e_b, exp_w_gate, exp_w_up, exp_w_down):
    rows = x.shape[1] // GRID_W
    cos, sin = axial_rope_tables(rows)
    xl, xc = x, ctx
    for l in range(DEPTH):
        last = l == DEPTH - 1
        lam_init = 0.8 - 0.6 * math.exp(-0.3 * l)
        cc = jnp.concatenate([c, c_ctx[None, :]], axis=0)
        mod = _dense(jax.nn.silu(cc), w_mod[l]) + b_mod[l]
        mod_l = [m[:, None, :] for m in jnp.split(mod[:BATCH], MOD_CHUNKS, axis=-1)]
        mod_c = jnp.split(mod[BATCH], MOD_CHUNKS, axis=-1)
        hl = _dense(modulate(rmsnorm(xl, norm1_w[l]), mod_l[0], mod_l[1]), w_in[l])
        hc = _dense(modulate(rmsnorm(xc, norm1_w[l]), mod_c[0], mod_c[1]), w_in[l])
        mc, ml = token_mixers(hc, hl, cos, sin, lam_init, not last,
                              diff_qn_w[l], diff_kn_w[l], diff_lq1[l], diff_lk1[l], diff_lq2[l], diff_lk2[l],
                              diff_norm_w[l], ssm_conv_w[l], ssm_conv_b[l], ssm_dt_bias[l], ssm_a_log[l],
                              ssm_d[l], ssm_norm_w[l], gdn_conv_w[l], gdn_dt_bias[l], gdn_a_log[l], gdn_norm_w[l])
        xl = xl + mod_l[2] * _dense(ml, w_out[l])
        fl = modulate(rmsnorm(xl, norm2_w[l]), mod_l[3], mod_l[4]).reshape(-1, D_MODEL)
        moe_args = (router_g_w[l], router_g_b[l], router_e_w[l], router_e_b[l],
                    exp_w_gate[l], exp_w_up[l], exp_w_down[l])
        if last:
            yl = hier_moe(fl, *moe_args)
        else:
            xc = xc + mod_c[2] * _dense(mc, w_out[l])
            fc = modulate(rmsnorm(xc, norm2_w[l]), mod_c[3], mod_c[4]).reshape(-1, D_MODEL)
            y = hier_moe(jnp.concatenate([fl, fc], axis=0), *moe_args)
            yl = y[:fl.shape[0]]
            xc = xc + mod_c[5] * y[fl.shape[0]:].reshape(xc.shape)
        xl = xl + mod_l[5] * yl.reshape(xl.shape)
    return xl
```

```python
import functools
import math
import jax
import jax.numpy as jnp
from jax import lax
import numpy as np
from jax.experimental import pallas as pl
from jax.experimental.pallas import tpu as pltpu

D_MODEL = 1024
DEPTH = 2
CTX_LEN = 256
GRID_W = 64
EPS = 1e-6
MOD_CHUNKS = 6

DIFF_HEADS = 4
DIFF_QK = 32
DIFF_V = 2 * DIFF_QK
DIFF_W = DIFF_HEADS * DIFF_V
ROPE_BASE = 10000.0

SSM_HEADS = 8
SSM_P = 64
SSM_GROUPS = 2
SSM_N = 64
SSM_W = SSM_HEADS * SSM_P
SSM_XBC = SSM_W + 2 * SSM_GROUPS * SSM_N
CONV_K = 5

GDN_HEADS = 4
GDN_DK = 64
GDN_DV = 64
GDN_QKV = GDN_HEADS * (2 * GDN_DK + GDN_DV)
GDN_W = GDN_HEADS * GDN_DV

D_MIX = DIFF_W + SSM_W + GDN_W
SPLIT_SIZES = (3 * DIFF_W, SSM_W, SSM_XBC, SSM_HEADS, GDN_QKV, GDN_W, 2 * GDN_HEADS, 2 * GDN_HEADS)

N_EGROUPS = 4
EXPERTS_PER_GROUP = 8
N_EXPERTS = N_EGROUPS * EXPERTS_PER_GROUP
TOP_K = 2
D_EXPERT = 512
MOE_BLOCK = 256

TOK_TILE = 256
LOG2E = 1.4426950408889634
HALO = 8
SMALL_W = 128
L_DT = 0
L_SCUM = 16
L_BETA = 32
L_GCUM = 40


def rmsnorm(x, w):
    x32 = x.astype(jnp.float32)
    y = x32 * lax.rsqrt(jnp.mean(x32 * x32, axis=-1, keepdims=True) + EPS)
    return y.astype(x.dtype) * w


def modulate(x, shift, scale):
    return x * (1.0 + scale) + shift


def _bf16_terms(x, n):
    out = []
    for _ in range(n):
        t = x.astype(jnp.bfloat16)
        out.append(t)
        x = x - t.astype(jnp.float32)
    return out


def _group_sum_sq(x, g_ref):
    hi, lo = _bf16_terms(x * x, 2)
    g = g_ref[...]
    return (jnp.dot(hi, g, preferred_element_type=jnp.float32)
            + jnp.dot(lo, g, preferred_element_type=jnp.float32))


def _group_matrix(width, group, value):
    idx = np.arange(width) // group
    return jnp.asarray((idx[:, None] == idx[None, :]).astype(np.float32) * value, jnp.bfloat16)


def _attn_prep_kernel(qkv_ref, cos_ref, sin_ref, wq_ref, wk_ref, g_ref, q_out, kt_out, v_out):
    x = qkv_ref[0]
    cos = cos_ref[...]
    sin = sin_ref[...]
    lane = lax.broadcasted_iota(jnp.int32, cos.shape, 1)
    lo_half = (lane % 16) < 8

    def norm_rope(t, w):
        y = t * lax.rsqrt(_group_sum_sq(t, g_ref) + EPS) * w
        rot = jnp.where(lo_half, pltpu.roll(y, DIFF_W - 8, 1), pltpu.roll(y, 8, 1))
        return y * cos + rot * sin

    q = norm_rope(x[:, 0:DIFF_W], wq_ref[...]) * (DIFF_QK ** -0.5 * LOG2E)
    k = norm_rope(x[:, DIFF_W:2 * DIFF_W], wk_ref[...])
    q_out[0] = q
    kt_out[0] = k.T.astype(jnp.bfloat16)
    v_out[0] = x[:, 2 * DIFF_W:3 * DIFF_W].astype(jnp.bfloat16)


def _attn_prep(qkv, cos_t, sin_t, wq, wk, gmat):
    b, n, _ = qkv.shape
    nt = n // TOK_TILE
    return pl.pallas_call(
        _attn_prep_kernel,
        grid=(b, nt),
        in_specs=[pl.BlockSpec((1, TOK_TILE, 3 * DIFF_W), lambda i, j: (i, j, 0)),
                  pl.BlockSpec((TOK_TILE, DIFF_W), lambda i, j: (j, 0)),
                  pl.BlockSpec((TOK_TILE, DIFF_W), lambda i, j: (j, 0)),
                  pl.BlockSpec((1, DIFF_W), lambda i, j: (0, 0)),
                  pl.BlockSpec((1, DIFF_W), lambda i, j: (0, 0)),
                  pl.BlockSpec((DIFF_W, DIFF_W), lambda i, j: (0, 0))],
        out_specs=[pl.BlockSpec((1, TOK_TILE, DIFF_W), lambda i, j: (i, j, 0)),
                   pl.BlockSpec((1, DIFF_W, TOK_TILE), lambda i, j: (i, 0, j)),
                   pl.BlockSpec((1, TOK_TILE, DIFF_W), lambda i, j: (i, j, 0))],
        out_shape=[jax.ShapeDtypeStruct((b, n, DIFF_W), jnp.float32),
                   jax.ShapeDtypeStruct((b, DIFF_W, n), jnp.bfloat16),
                   jax.ShapeDtypeStruct((b, n, DIFF_W), jnp.bfloat16)],
        compiler_params=pltpu.CompilerParams(dimension_semantics=("parallel", "parallel")),
        name="attn_prep",
    )(qkv, cos_t, sin_t, wq, wk, gmat)


def _attn_kernel(lam_ref, q_ref, kt_ref, v_ref, wo_ref, g_ref, o_ref, s_ref, p_ref, acc_ref, *, n_keys, out_scale):
    tile = pl.program_id(1)
    lam = lam_ref[0]
    lane = lax.broadcasted_iota(jnp.int32, (TOK_TILE, DIFF_W), 1)

    def attend(nk):
        n_chunks = nk // TOK_TILE
        q = q_ref[0]
        acc_ref[...] = jnp.zeros_like(acc_ref)

        def unit(u, carry):
            qm = jnp.where((lane >= u * DIFF_QK) & (lane < (u + 1) * DIFF_QK), q, 0.0).astype(jnp.bfloat16)
            mx = None
            for c in range(n_chunks):
                s = jnp.dot(qm, kt_ref[0, :, c * TOK_TILE:(c + 1) * TOK_TILE],
                            preferred_element_type=jnp.float32)
                s_ref[:, c * TOK_TILE:(c + 1) * TOK_TILE] = s
                sm = jnp.maximum(s[:, :128], s[:, 128:])
                mx = sm if mx is None else jnp.maximum(mx, sm)
            m = jnp.max(mx, axis=-1, keepdims=True)
            ls = None
            for c in range(n_chunks):
                e = jnp.exp2(s_ref[:, c * TOK_TILE:(c + 1) * TOK_TILE] - m)
                p_ref[:, c * TOK_TILE:(c + 1) * TOK_TILE] = e.astype(jnp.bfloat16)
                es = e[:, :128] + e[:, 128:]
                ls = es if ls is None else ls + es
            l = jnp.sum(ls, axis=-1, keepdims=True)
            pv = jnp.dot(p_ref[:, :nk], v_ref[0, :nk, :], preferred_element_type=jnp.float32)
            coef = jnp.where(u % 2 == 0, 1.0, -lam) / l
            head = u // 2
            in_head = (lane >= head * DIFF_V) & (lane < (head + 1) * DIFF_V)
            acc_ref[...] += jnp.where(in_head, pv * coef, 0.0)
            return carry

        lax.fori_loop(0, 2 * DIFF_HEADS, unit, 0)
        o = acc_ref[...]
        y = o * lax.rsqrt(_group_sum_sq(o, g_ref) + EPS)
        o_ref[0] = y * wo_ref[...] * out_scale

    @pl.when(tile == 0)
    def _():
        attend(TOK_TILE)

    @pl.when(tile > 0)
    def _():
        attend(n_keys)


def _attention(lam, q, kt, v, wo, gmat, out_scale):
    b, n, _ = q.shape
    nt = n // TOK_TILE
    kern = functools.partial(_attn_kernel, n_keys=n, out_scale=out_scale)
    return pl.pallas_call(
        kern,
        grid=(b, nt),
        in_specs=[pl.BlockSpec(memory_space=pltpu.SMEM),
                  pl.BlockSpec((1, TOK_TILE, DIFF_W), lambda i, j: (i, j, 0)),
                  pl.BlockSpec((1, DIFF_W, n), lambda i, j: (i, 0, 0)),
                  pl.BlockSpec((1, n, DIFF_W), lambda i, j: (i, 0, 0)),
                  pl.BlockSpec((1, DIFF_W), lambda i, j: (0, 0)),
                  pl.BlockSpec((DIFF_W, DIFF_W), lambda i, j: (0, 0))],
        out_specs=pl.BlockSpec((1, TOK_TILE, DIFF_W), lambda i, j: (i, j, 0)),
        out_shape=jax.ShapeDtypeStruct((b, n, DIFF_W), jnp.float32),
        scratch_shapes=[pltpu.VMEM((TOK_TILE, n), jnp.float32),
                        pltpu.VMEM((TOK_TILE, n), jnp.bfloat16),
                        pltpu.VMEM((TOK_TILE, DIFF_W), jnp.float32)],
        compiler_params=pltpu.CompilerParams(dimension_semantics=("parallel", "arbitrary"),
                                             vmem_limit_bytes=48 * 1024 * 1024),
        name="diff_attn",
    )(lam, q, kt, v, wo, gmat)


def _rope_tables(n_latent):
    rows = n_latent // GRID_W
    row = np.repeat(np.arange(rows, dtype=np.float32), GRID_W)
    col = np.tile(np.arange(GRID_W, dtype=np.float32), rows)
    half = DIFF_QK // 2
    inv = (ROPE_BASE ** (-np.arange(0, half, 2, dtype=np.float32) / half)).astype(np.float32)
    ang = np.concatenate([row[:, None] * inv, row[:, None] * inv, col[:, None] * inv, col[:, None] * inv], axis=-1)
    cos = np.concatenate([np.ones((TOK_TILE, DIFF_QK), np.float32), np.cos(ang)], axis=0)
    sin = np.concatenate([np.zeros((TOK_TILE, DIFF_QK), np.float32), np.sin(ang)], axis=0)
    sign = np.where((np.arange(DIFF_QK) % 16) < 8, -1.0, 1.0).astype(np.float32)
    reps = DIFF_W // DIFF_QK
    return jnp.asarray(np.tile(cos, (1, reps))), jnp.asarray(np.tile(sin * sign, (1, reps)))


def diff_attention_pallas(qkv, lam_init, qn_w, kn_w, lq1, lk1, lq2, lk2, out_w):
    n = qkv.shape[1]
    cos_t, sin_t = _rope_tables(n - TOK_TILE)
    g32 = _group_matrix(DIFF_W, DIFF_QK, 1.0 / DIFF_QK)
    g64 = _group_matrix(DIFF_W, DIFF_V, 1.0 / DIFF_V)
    reps = DIFF_W // DIFF_QK
    q, kt, v = _attn_prep(qkv, cos_t, sin_t, jnp.tile(qn_w, reps)[None, :], jnp.tile(kn_w, reps)[None, :], g32)
    lam = (jnp.exp(jnp.sum(lq1 * lk1)) - jnp.exp(jnp.sum(lq2 * lk2)) + lam_init).reshape(1).astype(jnp.float32)
    return _attention(lam, q, kt, v, jnp.tile(out_w, DIFF_HEADS)[None, :], g64, 1.0 - lam_init)


def _dwconv5(cur_ref, prev_ref, next_ref, w_ref, ext_ref, has_prev, has_next):
    t = TOK_TILE
    ext_ref[0:HALO, :] = jnp.where(has_prev, prev_ref[0], 0.0)
    ext_ref[HALO:HALO + t, :] = cur_ref[0]
    ext_ref[HALO + t:2 * HALO + t, :] = jnp.where(has_next, next_ref[0], 0.0)
    acc = None
    for j in range(CONV_K):
        term = ext_ref[pl.ds(HALO - CONV_K // 2 + j, t), :] * w_ref[j:j + 1, :]
        acc = term if acc is None else acc + term
    return acc


def _silu(x):
    return x * jax.nn.sigmoid(x)


def _mixer_prep_kernel(xbc_ref, xbc_p, xbc_n, gq_ref, gq_p, gq_n, sm_ref, sw_ref, sb_ref, gw_ref,
                       bias_ref, scale_ref, ones_ref,
                       xs_out, cm_out, bt_out, q_out, k_out, kt_out, v_out, cols_out, rows_out, ext_ref):
    tile = pl.program_id(1)
    nt = pl.num_programs(1)
    has_prev = tile >= 2
    has_next = (tile >= 1) & (tile < nt - 1)

    u = _silu(_dwconv5(xbc_ref, xbc_p, xbc_n, sw_ref, ext_ref, has_prev, has_next) + sb_ref[...])
    xs_out[0] = u[:, :SSM_W]
    bt_out[0] = u[:, SSM_W:SSM_W + SSM_GROUPS * SSM_N].T.astype(jnp.bfloat16)
    cm_out[0] = u[:, SSM_W + SSM_GROUPS * SSM_N:]

    g = _silu(_dwconv5(gq_ref, gq_p, gq_n, gw_ref, ext_ref, has_prev, has_next))

    def l2n(t):
        return t * lax.rsqrt(_group_sum_sq(t, ones_ref) + EPS)

    q_out[0] = l2n(g[:, :GDN_W]) * (GDN_DK ** -0.5)
    k = l2n(g[:, GDN_W:2 * GDN_W])
    k_out[0] = k
    kt_out[0] = k.T.astype(jnp.bfloat16)
    v_out[0] = g[:, 2 * GDN_W:]

    sm = sm_ref[0] + bias_ref[...]
    sp = jnp.maximum(sm, 0.0) + jnp.log1p(jnp.exp(-jnp.abs(sm)))
    sg = jax.nn.sigmoid(sm)
    vals = sp * scale_ref[...]
    ri = lax.broadcasted_iota(jnp.int32, (TOK_TILE, TOK_TILE), 0)
    ci = lax.broadcasted_iota(jnp.int32, (TOK_TILE, TOK_TILE), 1)
    tri_pre = jnp.where(ri >= ci, 1.0, 0.0).astype(jnp.bfloat16)
    tri_suf = jnp.where(ri <= ci, 1.0, 0.0).astype(jnp.bfloat16)
    pre = None
    suf = None
    for term in _bf16_terms(vals, 3):
        a = jnp.dot(tri_pre, term, preferred_element_type=jnp.float32)
        b = jnp.dot(tri_suf, term, preferred_element_type=jnp.float32)
        pre = a if pre is None else pre + a
        suf = b if suf is None else suf + b
    lane = lax.broadcasted_iota(jnp.int32, (TOK_TILE, SMALL_W), 1)
    bwd_lane = ((lane >= L_SCUM + SSM_HEADS) & (lane < L_BETA)) | (lane >= L_GCUM + GDN_HEADS)
    cum = jnp.where(bwd_lane, suf, pre)
    cols = jnp.where(lane < L_SCUM, sp, jnp.where((lane >= L_BETA) & (lane < L_GCUM), sg, cum))
    cols_out[0] = cols
    rows_out[0] = cols.T


def _mixer_prep(xbc, gqkv, small, ssm_w, ssm_b, gdn_w, bias_vec, scale_vec):
    b, n, _ = xbc.shape
    nt = n // TOK_TILE
    hb = TOK_TILE // HALO
    last = n // HALO - 1
    ones_blk = _group_matrix(GDN_W, GDN_DK, 1.0)

    def cur(w):
        return pl.BlockSpec((1, TOK_TILE, w), lambda i, j: (i, j, 0))

    def prev(w):
        return pl.BlockSpec((1, HALO, w), lambda i, j: (i, jnp.maximum(j * hb - 1, 0), 0))

    def nxt(w):
        return pl.BlockSpec((1, HALO, w), lambda i, j: (i, jnp.minimum((j + 1) * hb, last), 0))

    def const(shape):
        return pl.BlockSpec(shape, lambda i, j: (0,) * len(shape))

    f32 = jnp.float32
    return pl.pallas_call(
        _mixer_prep_kernel,
        grid=(b, nt),
        in_specs=[cur(SSM_XBC), prev(SSM_XBC), nxt(SSM_XBC), cur(GDN_QKV), prev(GDN_QKV), nxt(GDN_QKV), cur(SMALL_W),
                  const((8, SSM_XBC)), const((1, SSM_XBC)), const((8, GDN_QKV)),
                  const((1, SMALL_W)), const((1, SMALL_W)), const((GDN_W, GDN_W))],
        out_specs=[cur(SSM_W), cur(SSM_GROUPS * SSM_N),
                   pl.BlockSpec((1, SSM_GROUPS * SSM_N, TOK_TILE), lambda i, j: (i, 0, j)),
                   cur(GDN_W), cur(GDN_W),
                   pl.BlockSpec((1, GDN_W, TOK_TILE), lambda i, j: (i, 0, j)),
                   cur(GDN_W), cur(SMALL_W),
                   pl.BlockSpec((1, SMALL_W, TOK_TILE), lambda i, j: (i, 0, j))],
        out_shape=[jax.ShapeDtypeStruct((b, n, SSM_W), f32),
                   jax.ShapeDtypeStruct((b, n, SSM_GROUPS * SSM_N), f32),
                   jax.ShapeDtypeStruct((b, SSM_GROUPS * SSM_N, n), jnp.bfloat16),
                   jax.ShapeDtypeStruct((b, n, GDN_W), f32),
                   jax.ShapeDtypeStruct((b, n, GDN_W), f32),
                   jax.ShapeDtypeStruct((b, GDN_W, n), jnp.bfloat16),
                   jax.ShapeDtypeStruct((b, n, GDN_W), f32),
                   jax.ShapeDtypeStruct((b, n, SMALL_W), f32),
                   jax.ShapeDtypeStruct((b, SMALL_W, n), f32)],
        scratch_shapes=[pltpu.VMEM((TOK_TILE + 2 * HALO, SSM_XBC), f32)],
        compiler_params=pltpu.CompilerParams(dimension_semantics=("parallel", "parallel")),
        name="mixer_prep",
    )(xbc, xbc, xbc, gqkv, gqkv, gqkv, small, ssm_w, ssm_b, gdn_w, bias_vec, scale_vec, ones_blk)


def _scan_tile(d, s, nt):
    return jnp.where(d == 0, s, jnp.where(s == 0, 0, nt - s))


def _pick(is_f, arr, base, stride, i, axis):
    a, b = base + i, base + stride + i
    if axis == 1:
        return jnp.where(is_f, arr[:, a:a + 1], arr[:, b:b + 1])
    return jnp.where(is_f, arr[a:a + 1, :], arr[b:b + 1, :])


def _order_masks(is_f):
    ri = lax.broadcasted_iota(jnp.int32, (TOK_TILE, TOK_TILE), 0)
    ci = lax.broadcasted_iota(jnp.int32, (TOK_TILE, TOK_TILE), 1)
    diff = (ri - ci) * jnp.where(is_f, 1, -1)
    return diff >= 0, diff > 0


def _ssd_kernel(xs_ref, cm_ref, bt_ref, cols_ref, rows_ref, y_ref, st_ref):
    is_f = pl.program_id(0) == 0

    @pl.when(pl.program_id(2) == 0)
    def _():
        st_ref[...] = jnp.zeros_like(st_ref)

    bf16 = jnp.bfloat16
    f32 = jnp.float32
    t = TOK_TILE
    cols = cols_ref[0]
    rows = rows_ref[0]
    incl, _ = _order_masks(is_f)
    lane128 = lax.broadcasted_iota(jnp.int32, (t, 128), 1)
    cm = cm_ref[0]
    bt = bt_ref[0]
    hpg = SSM_HEADS // SSM_GROUPS
    for g in range(SSM_GROUPS):
        cg = jnp.where((lane128 >= g * SSM_N) & (lane128 < (g + 1) * SSM_N), cm, 0.0).astype(bf16)
        cb = jnp.dot(cg, bt, preferred_element_type=f32)
        st_g = st_ref[:, g * hpg * SSM_P:(g + 1) * hpg * SSM_P]
        yoff = jnp.dot(cg, st_g.astype(bf16), preferred_element_type=f32)
        btg = bt[g * SSM_N:(g + 1) * SSM_N, :].astype(f32)
        for pair in range(hpg // 2):
            xp = xs_ref[0, :, (g * hpg + 2 * pair) * SSM_P:(g * hpg + 2 * pair + 2) * SSM_P]
            acc_y = None
            acc_s = None
            e_col = None
            dec = None
            for half in range(2):
                hd = g * hpg + 2 * pair + half
                cum_col = _pick(is_f, cols, L_SCUM, SSM_HEADS, hd, 1)
                cum_row = _pick(is_f, rows, L_SCUM, SSM_HEADS, hd, 0)
                dt_col = _pick(is_f, cols, L_DT, SSM_HEADS, hd, 1)
                cum_tot = jnp.where(is_f, cum_row[:, t - 1:t], cum_row[:, 0:1])
                lm = jnp.exp(jnp.where(incl, cum_col - cum_row, -jnp.inf))
                m = (cb * lm).astype(bf16)
                in_half = (lane128 >= half * SSM_P) & (lane128 < (half + 1) * SSM_P)
                xm = jnp.where(in_half, xp * dt_col, 0.0).astype(bf16)
                ty = jnp.dot(m, xm, preferred_element_type=f32)
                ts = jnp.dot((btg * jnp.exp(cum_tot - cum_row)).astype(bf16), xm, preferred_element_type=f32)
                acc_y = ty if acc_y is None else acc_y + ty
                acc_s = ts if acc_s is None else acc_s + ts
                ec = jnp.exp(cum_col)
                dc = jnp.exp(cum_tot)
                e_col = ec if e_col is None else jnp.where(in_half, ec, e_col)
                dec = dc if dec is None else jnp.where(in_half[0:1, :], dc, dec)
            lo = pair * 2 * SSM_P
            c0 = g * hpg * SSM_P + lo
            y_ref[0, 0, :, c0:c0 + 2 * SSM_P] = acc_y + yoff[:, lo:lo + 2 * SSM_P] * e_col
            r0 = g * SSM_N
            st_ref[r0:r0 + SSM_N, c0:c0 + 2 * SSM_P] = st_ref[r0:r0 + SSM_N, c0:c0 + 2 * SSM_P] * dec + acc_s


def _ssd_scan(xs, cm, bt, cols, rows):
    b, n, _ = xs.shape
    nt = n // TOK_TILE

    def tok(w):
        return pl.BlockSpec((1, TOK_TILE, w), lambda d, i, s: (i, _scan_tile(d, s, nt), 0))

    def tr(w):
        return pl.BlockSpec((1, w, TOK_TILE), lambda d, i, s: (i, 0, _scan_tile(d, s, nt)))

    return pl.pallas_call(
        _ssd_kernel,
        grid=(2, b, nt),
        in_specs=[tok(SSM_W), tok(SSM_GROUPS * SSM_N), tr(SSM_GROUPS * SSM_N), tok(SMALL_W), tr(SMALL_W)],
        out_specs=pl.BlockSpec((1, 1, TOK_TILE, SSM_W), lambda d, i, s: (d, i, _scan_tile(d, s, nt), 0)),
        out_shape=jax.ShapeDtypeStruct((2, b, n, SSM_W), jnp.float32),
        scratch_shapes=[pltpu.VMEM((SSM_GROUPS * SSM_N, SSM_W), jnp.float32)],
        compiler_params=pltpu.CompilerParams(dimension_semantics=("parallel", "parallel", "arbitrary")),
        name="ssd_scan",
    )(xs, cm, bt, cols, rows)


def _gdn_kernel(q_ref, k_ref, kt_ref, v_ref, cols_ref, rows_ref, o_ref, s_ref):
    is_f = pl.program_id(0) == 0

    @pl.when(pl.program_id(2) == 0)
    def _():
        s_ref[...] = jnp.zeros_like(s_ref)

    bf16 = jnp.bfloat16
    f32 = jnp.float32
    t = TOK_TILE
    cols = cols_ref[0]
    rows = rows_ref[0]
    incl, strict = _order_masks(is_f)
    lane = lax.broadcasted_iota(jnp.int32, (t, GDN_W), 1)
    sub = lax.broadcasted_iota(jnp.int32, (GDN_W, t), 0)
    head_of_lane = [(lane >= h * GDN_DK) & (lane < (h + 1) * GDN_DK) for h in range(GDN_HEADS)]
    head_of_sub = [(sub >= h * GDN_DK) & (sub < (h + 1) * GDN_DK) for h in range(GDN_HEADS)]

    def by_lane(pieces):
        out = pieces[0]
        for h in range(1, GDN_HEADS):
            out = jnp.where(head_of_lane[h] if pieces[h].shape[0] != 1 else head_of_lane[h][0:1, :], pieces[h], out)
        return out

    q = q_ref[0]
    k = k_ref[0]
    v = v_ref[0]
    kt = kt_ref[0]
    gc_col = [_pick(is_f, cols, L_GCUM, GDN_HEADS, h, 1) for h in range(GDN_HEADS)]
    gc_row = [_pick(is_f, rows, L_GCUM, GDN_HEADS, h, 0) for h in range(GDN_HEADS)]
    gc_end = [jnp.where(is_f, r[:, t - 1:t], r[:, 0:1]) for r in gc_row]
    beta_x = by_lane([_pick(is_f, cols, L_BETA, GDN_HEADS, h, 1) + jnp.zeros((t, GDN_W), f32) for h in range(GDN_HEADS)])
    egc = jnp.exp(by_lane([c + jnp.zeros((t, GDN_W), f32) for c in gc_col]))
    kb = k * beta_x
    vb = v * beta_x
    kbg = kb * egc
    qd = q * egc
    ri = lax.broadcasted_iota(jnp.int32, (t, t), 0)
    ci = lax.broadcasted_iota(jnp.int32, (t, t), 1)
    eye = jnp.where(ri == ci, 1.0, 0.0)
    off_levels = []
    size = 1
    while size < t:
        off_levels.append(((ri // (2 * size)) == (ci // (2 * size))) & ((ri // size) != (ci // size)))
        size *= 2

    uw = None
    qkd = []
    for h in range(GDN_HEADS):
        kbm = jnp.where(head_of_lane[h], kb, 0.0).astype(bf16)
        qm = jnp.where(head_of_lane[h], q, 0.0).astype(bf16)
        kk = jnp.dot(kbm, kt, preferred_element_type=f32)
        qk = jnp.dot(qm, kt, preferred_element_type=f32)
        dec = jnp.exp(jnp.where(incl, gc_col[h] - gc_row[h], -jnp.inf))
        a = jnp.where(strict, kk * dec, 0.0)
        qkd.append((qk * dec).astype(bf16))
        x = eye
        for lvl_mask in off_levels:
            xb = x.astype(bf16)
            e = jnp.where(lvl_mask, a, 0.0).astype(bf16)
            x = x - jnp.dot(jnp.dot(xb, e, preferred_element_type=f32).astype(bf16), xb, preferred_element_type=f32)
        rhs = jnp.concatenate([jnp.where(head_of_lane[h], vb, 0.0), jnp.where(head_of_lane[h], kbg, 0.0)], axis=1)
        term = jnp.dot(x.astype(bf16), rhs.astype(bf16), preferred_element_type=f32)
        uw = term if uw is None else uw + term

    s_old = s_ref[...]
    sb = s_old.astype(bf16)
    u = uw[:, :GDN_W]
    w = uw[:, GDN_W:]
    v_new = u - jnp.dot(w.astype(bf16), sb, preferred_element_type=f32)
    o = jnp.dot(qd.astype(bf16), sb, preferred_element_type=f32)
    for h in range(GDN_HEADS):
        o = o + jnp.dot(qkd[h], jnp.where(head_of_lane[h], v_new, 0.0).astype(bf16), preferred_element_type=f32)
    o_ref[0, 0] = o

    mult = jnp.exp(gc_end[0] - gc_row[0]) + jnp.zeros((GDN_W, t), f32)
    for h in range(1, GDN_HEADS):
        mult = jnp.where(head_of_sub[h], jnp.exp(gc_end[h] - gc_row[h]), mult)
    kend_t = (kt.astype(f32) * mult).astype(bf16)
    upd = jnp.dot(kend_t, v_new.astype(bf16), preferred_element_type=f32)
    g_end = jnp.exp(by_lane(gc_end))
    same_head = head_of_sub[0] & head_of_lane[0]
    for h in range(1, GDN_HEADS):
        same_head = same_head | (head_of_sub[h] & head_of_lane[h])
    s_ref[...] = jnp.where(same_head, s_old * g_end + upd, 0.0)


def _gdn_scan(q, k, kt, v, cols, rows):
    b, n, _ = q.shape
    nt = n // TOK_TILE

    def tok(w):
        return pl.BlockSpec((1, TOK_TILE, w), lambda d, i, s: (i, _scan_tile(d, s, nt), 0))

    def tr(w):
        return pl.BlockSpec((1, w, TOK_TILE), lambda d, i, s: (i, 0, _scan_tile(d, s, nt)))

    return pl.pallas_call(
        _gdn_kernel,
        grid=(2, b, nt),
        in_specs=[tok(GDN_W), tok(GDN_W), tr(GDN_W), tok(GDN_W), tok(SMALL_W), tr(SMALL_W)],
        out_specs=pl.BlockSpec((1, 1, TOK_TILE, GDN_W), lambda d, i, s: (d, i, _scan_tile(d, s, nt), 0)),
        out_shape=jax.ShapeDtypeStruct((2, b, n, GDN_W), jnp.float32),
        scratch_shapes=[pltpu.VMEM((GDN_W, GDN_W), jnp.float32)],
        compiler_params=pltpu.CompilerParams(dimension_semantics=("parallel", "parallel", "arbitrary")),
        name="gdn_scan",
    )(q, k, kt, v, cols, rows)


def _small_lane_params(ssm_dt_bias, ssm_a_log, gdn_dt_bias, gdn_a_log):
    f32 = jnp.float32
    z = lambda k: jnp.zeros((k,), f32)
    sb = ssm_dt_bias.astype(f32).reshape(-1)
    bias = jnp.concatenate([sb, sb, z(L_GCUM - L_BETA), gdn_dt_bias.astype(f32).reshape(-1),
                            z(SMALL_W - L_GCUM - 2 * GDN_HEADS)])
    scale = jnp.concatenate([z(L_SCUM), -jnp.exp(ssm_a_log.astype(f32)).reshape(-1), z(L_GCUM - L_BETA),
                             -jnp.exp(gdn_a_log.astype(f32)).reshape(-1), z(SMALL_W - L_GCUM - 2 * GDN_HEADS)])
    return bias[None, :], scale[None, :]


def _permuted_w_in(w_in):
    offs = np.cumsum((0,) + SPLIT_SIZES)
    dt0, beta0, a0 = int(offs[3]), int(offs[6]), int(offs[7])
    dt = list(range(dt0, dt0 + SSM_HEADS))
    small = dt + dt + dt + dt + list(range(beta0, beta0 + 2 * GDN_HEADS)) + list(range(a0, a0 + 2 * GDN_HEADS))
    cols = list(range(0, dt0)) + list(range(int(offs[4]), beta0)) + small
    w = w_in[:, np.asarray(cols)]
    return jnp.pad(w, ((0, 0), (0, SMALL_W - len(small))))


def mixers_pallas(xbc, gqkv, small, ssm_conv_w, ssm_conv_b, ssm_dt_bias, ssm_a_log, gdn_conv_w, gdn_dt_bias, gdn_a_log):
    bias_vec, scale_vec = _small_lane_params(ssm_dt_bias, ssm_a_log, gdn_dt_bias, gdn_a_log)
    pad = ((0, 8 - CONV_K), (0, 0))
    xs, cm, bt, q, k, kt, v, cols, rows = _mixer_prep(
        xbc, gqkv, small, jnp.pad(ssm_conv_w, pad), ssm_conv_b[None, :], jnp.pad(gdn_conv_w, pad), bias_vec, scale_vec)
    y = _ssd_scan(xs, cm, bt, cols, rows)
    o = _gdn_scan(q, k, kt, v, cols, rows)
    return xs, y, o


def _expert_kernel(be_ref, nu_ref, x_ref, wg_ref, wu_ref, wd_ref, o_ref):
    i = pl.program_id(0)

    @pl.when(i < nu_ref[0])
    def _():
        x = x_ref[...]
        g = jnp.dot(x, wg_ref[0], preferred_element_type=jnp.float32)
        u = jnp.dot(x, wu_ref[0], preferred_element_type=jnp.float32)
        h = (_silu(g) * u).astype(jnp.bfloat16)
        o_ref[...] = jnp.dot(h, wd_ref[0], preferred_element_type=jnp.float32)

    @pl.when(i >= nu_ref[0])
    def _():
        o_ref[...] = jnp.zeros_like(o_ref)


def _expert_blocks(block_e, n_used, buf, w_gate, w_up, w_down):
    n_rows, d = buf.shape
    n_blocks = n_rows // MOE_BLOCK
    grid_spec = pltpu.PrefetchScalarGridSpec(
        num_scalar_prefetch=2,
        grid=(n_blocks,),
        in_specs=[pl.BlockSpec((MOE_BLOCK, d), lambda i, be, nu: (i, 0)),
                  pl.BlockSpec((1, d, D_EXPERT), lambda i, be, nu: (be[i], 0, 0)),
                  pl.BlockSpec((1, d, D_EXPERT), lambda i, be, nu: (be[i], 0, 0)),
                  pl.BlockSpec((1, D_EXPERT, d), lambda i, be, nu: (be[i], 0, 0))],
        out_specs=pl.BlockSpec((MOE_BLOCK, d), lambda i, be, nu: (i, 0)),
    )
    return pl.pallas_call(
        _expert_kernel,
        grid_spec=grid_spec,
        out_shape=jax.ShapeDtypeStruct((n_rows, d), jnp.float32),
        compiler_params=pltpu.CompilerParams(dimension_semantics=("arbitrary",)),
        name="moe_experts",
    )(block_e, n_used, buf, w_gate, w_up, w_down)


def routed_experts(t, experts, gates, w_gate, w_up, w_down):
    n, d = t.shape
    n_assign = n * TOP_K
    flat_e = experts.reshape(n_assign)
    order = jnp.argsort(flat_e)
    sorted_e = flat_e[order]
    counts = jax.ops.segment_sum(jnp.ones((n_assign,), jnp.int32), flat_e, num_segments=N_EXPERTS)
    starts = jnp.cumsum(counts) - counts
    padded = (counts + MOE_BLOCK - 1) // MOE_BLOCK * MOE_BLOCK
    pad_ends = jnp.cumsum(padded)
    pad_starts = pad_ends - padded
    dest = pad_starts[sorted_e] + jnp.arange(n_assign, dtype=jnp.int32) - starts[sorted_e]
    n_blocks = -(-n_assign // MOE_BLOCK) + N_EXPERTS
    buf = jnp.zeros((n_blocks * MOE_BLOCK, d), jnp.bfloat16).at[dest].set(t.astype(jnp.bfloat16)[order // TOP_K])
    block_start = jnp.arange(n_blocks, dtype=jnp.int32) * MOE_BLOCK
    block_e = jnp.minimum(jnp.searchsorted(pad_ends, block_start, side='right'), N_EXPERTS - 1).astype(jnp.int32)
    n_used = (pad_ends[-1] // MOE_BLOCK).astype(jnp.int32).reshape(1)
    out = _expert_blocks(block_e, n_used, buf, w_gate, w_up, w_down)
    y_assign = jnp.zeros((n_assign, d), out.dtype).at[order].set(out[dest])
    return jnp.sum(y_assign.reshape(n, TOP_K, d) * gates[..., None].astype(out.dtype), axis=1)


def hier_moe(t, router_g_w, router_g_b, router_e_w, router_e_b, w_gate, w_up, w_down):
    n = t.shape[0]
    grp_prob = jax.nn.softmax(jnp.dot(t, router_g_w, preferred_element_type=jnp.float32)
                              + router_g_b.astype(jnp.float32), axis=-1)
    p_grp, grp = lax.top_k(grp_prob, 1)
    e_logits = (jnp.dot(t, router_e_w, preferred_element_type=jnp.float32)
                + router_e_b.astype(jnp.float32)).reshape(n, N_EGROUPS, EXPERTS_PER_GROUP)
    sel = e_logits[jnp.arange(n), grp[:, 0]]
    p_top, idx = lax.top_k(jax.nn.softmax(sel, axis=-1), TOP_K)
    gates = p_grp * p_top / jnp.sum(p_top, axis=-1, keepdims=True)
    experts = grp * EXPERTS_PER_GROUP + idx
    return routed_experts(t, experts, gates, w_gate, w_up, w_down)


def _matmul_kernel(a_ref, b_ref, o_ref):
    o_ref[...] = jnp.dot(a_ref[...].astype(jnp.bfloat16), b_ref[...],
                         preferred_element_type=jnp.float32)


def _matmul(a, b, tm, tn):
    m, k = a.shape
    n = b.shape[1]
    return pl.pallas_call(
        _matmul_kernel,
        grid=(m // tm, n // tn),
        in_specs=[pl.BlockSpec((tm, k), lambda i, j: (i, 0)),
                  pl.BlockSpec((k, tn), lambda i, j: (0, j))],
        out_specs=pl.BlockSpec((tm, tn), lambda i, j: (i, j)),
        out_shape=jax.ShapeDtypeStruct((m, n), jnp.float32),
        compiler_params=pltpu.CompilerParams(dimension_semantics=("parallel", "parallel")),
        name="dense",
    )(a, b)


def _dense(a, w, tm=512):
    lead = a.shape[:-1]
    k = a.shape[-1]
    n = w.shape[1]
    a2 = a.reshape(-1, k)
    m = a2.shape[0]
    m_pad = -(-m // 8) * 8
    if m_pad != m:
        a2 = jnp.pad(a2, ((0, m_pad - m), (0, 0)))
    tm = min(tm, m_pad)
    n_pad = -(-n // 128) * 128
    tn = next(c for c in (1024, 768, 640, 512, 384, 256, 128) if n_pad % c == 0)
    wb = jnp.pad(w, ((0, 0), (0, n_pad - n))).astype(jnp.bfloat16)
    out = _matmul(a2, wb, tm, tn)[:m, :n]
    return out.reshape(lead + (n,))


def kernel(x, c, ctx, c_ctx, w_mod, b_mod, norm1_w, norm2_w, w_in, w_out,
           diff_qn_w, diff_kn_w, diff_lq1, diff_lk1, diff_lq2, diff_lk2, diff_norm_w,
           ssm_conv_w, ssm_conv_b, ssm_dt_bias, ssm_a_log, ssm_d, ssm_norm_w,
           gdn_conv_w, gdn_dt_bias, gdn_a_log, gdn_norm_w,
           router_g_w, router_g_b, router_e_w, router_e_b, exp_w_gate, exp_w_up, exp_w_down):
    assert ctx.shape[1] == CTX_LEN == TOK_TILE and x.shape[1] % TOK_TILE == 0
    bsz = x.shape[0]
    xs_all = jnp.concatenate([ctx, x], axis=1)
    n_tok = xs_all.shape[1]
    is_lat = (jnp.arange(n_tok) >= CTX_LEN)[None, :, None]
    cc = jnp.concatenate([c, c_ctx[None, :]], axis=0)
    bf16 = jnp.bfloat16
    for l in range(DEPTH):
        last = l == DEPTH - 1
        lam_init = 0.8 - 0.6 * math.exp(-0.3 * l)
        mod = (_dense(jax.nn.silu(cc), w_mod[l]) + b_mod[l]).reshape(bsz + 1, MOD_CHUNKS, D_MODEL)

        def mod_vec(i):
            return jnp.where(is_lat, mod[:bsz, None, i, :], mod[bsz, i, :])

        h = _dense(modulate(rmsnorm(xs_all, norm1_w[l]), mod_vec(0), mod_vec(1)), _permuted_w_in(w_in[l]))
        o0 = 3 * DIFF_W
        o1 = o0 + SSM_W
        o2 = o1 + SSM_XBC
        o3 = o2 + GDN_QKV
        o4 = o3 + GDN_W
        z, gate = h[..., o0:o1], h[..., o3:o4]
        att = diff_attention_pallas(h[..., :o0], lam_init, diff_qn_w[l], diff_kn_w[l],
                                    diff_lq1[l], diff_lk1[l], diff_lq2[l], diff_lk2[l], diff_norm_w[l])
        xs, y, o = mixers_pallas(h[..., o1:o2], h[..., o2:o3], h[..., o4:], ssm_conv_w[l], ssm_conv_b[l],
                                 ssm_dt_bias[l], ssm_a_log[l], gdn_conv_w[l], gdn_dt_bias[l], gdn_a_log[l])
        ys = (y[0] + y[1] + jnp.repeat(ssm_d[l], SSM_P) * xs) * jax.nn.silu(z)
        ys = rmsnorm(ys.reshape(bsz, n_tok, SSM_GROUPS, -1), ssm_norm_w[l].reshape(SSM_GROUPS, -1)).reshape(bsz, n_tok, SSM_W)
        og = rmsnorm((o[0] + o[1]).reshape(bsz, n_tok, GDN_HEADS, GDN_DV), gdn_norm_w[l])
        og = (og * jax.nn.silu(gate.reshape(bsz, n_tok, GDN_HEADS, GDN_DV))).reshape(bsz, n_tok, GDN_W)
        ml = jnp.concatenate([att, ys, og], axis=-1)
        xs_all = xs_all + mod_vec(2) * _dense(ml, w_out[l])
        f = modulate(rmsnorm(xs_all, norm2_w[l]), mod_vec(3), mod_vec(4))
        moe_w = (router_g_w[l], router_g_b[l], router_e_w[l], router_e_b[l],
                 exp_w_gate[l].astype(bf16), exp_w_up[l].astype(bf16), exp_w_down[l].astype(bf16))
        if last:
            xl = xs_all[:, CTX_LEN:]
            yl = hier_moe(f[:, CTX_LEN:].reshape(-1, D_MODEL), *moe_w)
            return xl + mod[:bsz, None, 5, :] * yl.reshape(xl.shape)
        ym = hier_moe(f.reshape(-1, D_MODEL), *moe_w)
        xs_all = xs_all + mod_vec(5) * ym.reshape(xs_all.shape)
```

```python
import functools
import math
import jax
import jax.numpy as jnp
from jax import lax
import numpy as np
from jax.experimental import pallas as pl
from jax.experimental.pallas import tpu as pltpu

D_MODEL = 1024
DEPTH = 2
CTX_LEN = 256
GRID_W = 64
EPS = 1e-6
MOD_CHUNKS = 6

DIFF_HEADS = 4
DIFF_QK = 32
DIFF_V = 2 * DIFF_QK
DIFF_W = DIFF_HEADS * DIFF_V
ROPE_BASE = 10000.0

SSM_HEADS = 8
SSM_P = 64
SSM_GROUPS = 2
SSM_N = 64
SSM_W = SSM_HEADS * SSM_P
SSM_XBC = SSM_W + 2 * SSM_GROUPS * SSM_N
CONV_K = 5

GDN_HEADS = 4
GDN_DK = 64
GDN_DV = 64
GDN_QKV = GDN_HEADS * (2 * GDN_DK + GDN_DV)
GDN_W = GDN_HEADS * GDN_DV

D_MIX = DIFF_W + SSM_W + GDN_W
SPLIT_SIZES = (3 * DIFF_W, SSM_W, SSM_XBC, SSM_HEADS, GDN_QKV, GDN_W, 2 * GDN_HEADS, 2 * GDN_HEADS)

N_EGROUPS = 4
EXPERTS_PER_GROUP = 8
N_EXPERTS = N_EGROUPS * EXPERTS_PER_GROUP
TOP_K = 2
D_EXPERT = 512
MOE_BLOCK = 256

TOK_TILE = 256
LOG2E = 1.4426950408889634
PV_SEG_CHUNKS = 4
HALO = 8
SMALL_W = 128
L_DT = 0
L_SCUM = 16
L_BETA = 32
L_GCUM = 40


def rmsnorm(x, w):
    x32 = x.astype(jnp.float32)
    y = x32 * lax.rsqrt(jnp.mean(x32 * x32, axis=-1, keepdims=True) + EPS)
    return y.astype(x.dtype) * w


def modulate(x, shift, scale):
    return x * (1.0 + scale) + shift


def _bf16_terms(x, n):
    out = []
    for _ in range(n):
        t = x.astype(jnp.bfloat16)
        out.append(t)
        x = x - t.astype(jnp.float32)
    return out


def _group_sum_sq(x, g_ref):
    hi, lo = _bf16_terms(x * x, 2)
    g = g_ref[...]
    return (jnp.dot(hi, g, preferred_element_type=jnp.float32)
            + jnp.dot(lo, g, preferred_element_type=jnp.float32))


def _group_matrix(width, group, value):
    idx = np.arange(width) // group
    return jnp.asarray((idx[:, None] == idx[None, :]).astype(np.float32) * value, jnp.bfloat16)


def _attn_prep_kernel(qkv_ref, cos_ref, sin_ref, wq_ref, wk_ref, g_ref, q_out, kt_out, v_out):
    x = qkv_ref[0]
    cos = cos_ref[...]
    sin = sin_ref[...]
    lane = lax.broadcasted_iota(jnp.int32, cos.shape, 1)
    lo_half = (lane % 16) < 8

    def norm_rope(t, w):
        y = t * lax.rsqrt(_group_sum_sq(t, g_ref) + EPS) * w
        rot = jnp.where(lo_half, pltpu.roll(y, DIFF_W - 8, 1), pltpu.roll(y, 8, 1))
        return y * cos + rot * sin

    q = norm_rope(x[:, 0:DIFF_W], wq_ref[...]) * (DIFF_QK ** -0.5 * LOG2E)
    k = norm_rope(x[:, DIFF_W:2 * DIFF_W], wk_ref[...])
    q_out[0] = q
    kt_out[0] = k.T.astype(jnp.bfloat16)
    v_out[0] = x[:, 2 * DIFF_W:3 * DIFF_W].astype(jnp.bfloat16)


def _attn_prep(qkv, cos_t, sin_t, wq, wk, gmat):
    b, n, _ = qkv.shape
    nt = n // TOK_TILE
    return pl.pallas_call(
        _attn_prep_kernel,
        grid=(b, nt),
        in_specs=[pl.BlockSpec((1, TOK_TILE, 3 * DIFF_W), lambda i, j: (i, j, 0)),
                  pl.BlockSpec((TOK_TILE, DIFF_W), lambda i, j: (j, 0)),
                  pl.BlockSpec((TOK_TILE, DIFF_W), lambda i, j: (j, 0)),
                  pl.BlockSpec((1, DIFF_W), lambda i, j: (0, 0)),
                  pl.BlockSpec((1, DIFF_W), lambda i, j: (0, 0)),
                  pl.BlockSpec((DIFF_W, DIFF_W), lambda i, j: (0, 0))],
        out_specs=[pl.BlockSpec((1, TOK_TILE, DIFF_W), lambda i, j: (i, j, 0)),
                   pl.BlockSpec((1, DIFF_W, TOK_TILE), lambda i, j: (i, 0, j)),
                   pl.BlockSpec((1, TOK_TILE, DIFF_W), lambda i, j: (i, j, 0))],
        out_shape=[jax.ShapeDtypeStruct((b, n, DIFF_W), jnp.float32),
                   jax.ShapeDtypeStruct((b, DIFF_W, n), jnp.bfloat16),
                   jax.ShapeDtypeStruct((b, n, DIFF_W), jnp.bfloat16)],
        compiler_params=pltpu.CompilerParams(dimension_semantics=("parallel", "parallel")),
        name="attn_prep",
    )(qkv, cos_t, sin_t, wq, wk, gmat)


def _attn_kernel(lam_ref, q_ref, kt_ref, v_ref, wo_ref, g_ref, o_ref, s_ref, p_ref, acc_ref, *, n_keys, out_scale):
    tile = pl.program_id(1)
    lam = lam_ref[0]
    lane = lax.broadcasted_iota(jnp.int32, (TOK_TILE, DIFF_W), 1)

    def attend(nk):
        n_chunks = nk // TOK_TILE
        q = q_ref[0]
        acc_ref[...] = jnp.zeros_like(acc_ref)

        def unit(head, carry):
            qm = jnp.concatenate(
                [jnp.where((lane >= (2 * head + mp) * DIFF_QK) & (lane < (2 * head + mp + 1) * DIFF_QK), q, 0.0)
                 for mp in range(2)], axis=0).astype(jnp.bfloat16)
            mx = None
            for c in range(n_chunks):
                s = jnp.dot(qm, kt_ref[0, :, c * TOK_TILE:(c + 1) * TOK_TILE],
                            preferred_element_type=jnp.float32)
                s_ref[:, c * TOK_TILE:(c + 1) * TOK_TILE] = s
                sm = jnp.maximum(s[:, :128], s[:, 128:])
                mx = sm if mx is None else jnp.maximum(mx, sm)
            m = jnp.max(mx, axis=-1, keepdims=True)
            ls = None
            pv = None
            seg_edges = sorted(set(range(n_chunks % PV_SEG_CHUNKS, n_chunks, PV_SEG_CHUNKS)) | {0, n_chunks})
            for c0, c1 in zip(seg_edges[:-1], seg_edges[1:]):
                for c in range(c0, c1):
                    e = jnp.exp2(s_ref[:, c * TOK_TILE:(c + 1) * TOK_TILE] - m)
                    p_ref[:, c * TOK_TILE:(c + 1) * TOK_TILE] = e.astype(jnp.bfloat16)
                    es = e[:, :128] + e[:, 128:]
                    ls = es if ls is None else ls + es
                part = jnp.dot(p_ref[:, c0 * TOK_TILE:c1 * TOK_TILE], v_ref[0, c0 * TOK_TILE:c1 * TOK_TILE, :],
                               preferred_element_type=jnp.float32)
                pv = part if pv is None else pv + part
            pv = pv / jnp.sum(ls, axis=-1, keepdims=True)
            in_head = (lane >= head * DIFF_V) & (lane < (head + 1) * DIFF_V)
            acc_ref[...] += jnp.where(in_head, pv[:TOK_TILE] - lam * pv[TOK_TILE:], 0.0)
            return carry

        lax.fori_loop(0, DIFF_HEADS, unit, 0)
        o = acc_ref[...]
        y = o * lax.rsqrt(_group_sum_sq(o, g_ref) + EPS)
        o_ref[0] = y * wo_ref[...] * out_scale

    @pl.when(tile == 0)
    def _():
        attend(TOK_TILE)

    @pl.when(tile > 0)
    def _():
        attend(n_keys)


def _attention(lam, q, kt, v, wo, gmat, out_scale):
    b, n, _ = q.shape
    nt = n // TOK_TILE
    kern = functools.partial(_attn_kernel, n_keys=n, out_scale=out_scale)
    return pl.pallas_call(
        kern,
        grid=(b, nt),
        in_specs=[pl.BlockSpec(memory_space=pltpu.SMEM),
                  pl.BlockSpec((1, TOK_TILE, DIFF_W), lambda i, j: (i, j, 0)),
                  pl.BlockSpec((1, DIFF_W, n), lambda i, j: (i, 0, 0)),
                  pl.BlockSpec((1, n, DIFF_W), lambda i, j: (i, 0, 0)),
                  pl.BlockSpec((1, DIFF_W), lambda i, j: (0, 0)),
                  pl.BlockSpec((DIFF_W, DIFF_W), lambda i, j: (0, 0))],
        out_specs=pl.BlockSpec((1, TOK_TILE, DIFF_W), lambda i, j: (i, j, 0)),
        out_shape=jax.ShapeDtypeStruct((b, n, DIFF_W), jnp.float32),
        scratch_shapes=[pltpu.VMEM((2 * TOK_TILE, n), jnp.float32),
                        pltpu.VMEM((2 * TOK_TILE, n), jnp.bfloat16),
                        pltpu.VMEM((TOK_TILE, DIFF_W), jnp.float32)],
        compiler_params=pltpu.CompilerParams(dimension_semantics=("parallel", "arbitrary"),
                                             vmem_limit_bytes=48 * 1024 * 1024),
        name="diff_attn",
    )(lam, q, kt, v, wo, gmat)


def _rope_tables(n_latent):
    rows = n_latent // GRID_W
    row = np.repeat(np.arange(rows, dtype=np.float32), GRID_W)
    col = np.tile(np.arange(GRID_W, dtype=np.float32), rows)
    half = DIFF_QK // 2
    inv = (ROPE_BASE ** (-np.arange(0, half, 2, dtype=np.float32) / half)).astype(np.float32)
    ang = np.concatenate([row[:, None] * inv, row[:, None] * inv, col[:, None] * inv, col[:, None] * inv], axis=-1)
    cos = np.concatenate([np.ones((TOK_TILE, DIFF_QK), np.float32), np.cos(ang)], axis=0)
    sin = np.concatenate([np.zeros((TOK_TILE, DIFF_QK), np.float32), np.sin(ang)], axis=0)
    sign = np.where((np.arange(DIFF_QK) % 16) < 8, -1.0, 1.0).astype(np.float32)
    reps = DIFF_W // DIFF_QK
    return jnp.asarray(np.tile(cos, (1, reps))), jnp.asarray(np.tile(sin * sign, (1, reps)))


def diff_attention_pallas(qkv, lam_init, qn_w, kn_w, lq1, lk1, lq2, lk2, out_w):
    n = qkv.shape[1]
    cos_t, sin_t = _rope_tables(n - TOK_TILE)
    g32 = _group_matrix(DIFF_W, DIFF_QK, 1.0 / DIFF_QK)
    g64 = _group_matrix(DIFF_W, DIFF_V, 1.0 / DIFF_V)
    reps = DIFF_W // DIFF_QK
    q, kt, v = _attn_prep(qkv, cos_t, sin_t, jnp.tile(qn_w, reps)[None, :], jnp.tile(kn_w, reps)[None, :], g32)
    lam = (jnp.exp(jnp.sum(lq1 * lk1)) - jnp.exp(jnp.sum(lq2 * lk2)) + lam_init).reshape(1).astype(jnp.float32)
    return _attention(lam, q, kt, v, jnp.tile(out_w, DIFF_HEADS)[None, :], g64, 1.0 - lam_init)


def _dwconv5(cur_ref, prev_ref, next_ref, w_ref, ext_ref, has_prev, has_next):
    t = TOK_TILE
    ext_ref[0:HALO, :] = jnp.where(has_prev, prev_ref[0], 0.0)
    ext_ref[HALO:HALO + t, :] = cur_ref[0]
    ext_ref[HALO + t:2 * HALO + t, :] = jnp.where(has_next, next_ref[0], 0.0)
    acc = None
    for j in range(CONV_K):
        term = ext_ref[pl.ds(HALO - CONV_K // 2 + j, t), :] * w_ref[j:j + 1, :]
        acc = term if acc is None else acc + term
    return acc


def _silu(x):
    return x * jax.nn.sigmoid(x)


def _mixer_prep_kernel(xbc_ref, xbc_p, xbc_n, gq_ref, gq_p, gq_n, sm_ref, sw_ref, sb_ref, gw_ref,
                       bias_ref, scale_ref, ones_ref,
                       xs_out, cm_out, bt_out, q_out, k_out, kt_out, v_out, cols_out, rows_out, ext_ref):
    tile = pl.program_id(1)
    nt = pl.num_programs(1)
    has_prev = tile >= 2
    has_next = (tile >= 1) & (tile < nt - 1)

    u = _silu(_dwconv5(xbc_ref, xbc_p, xbc_n, sw_ref, ext_ref, has_prev, has_next) + sb_ref[...])
    xs_out[0] = u[:, :SSM_W]
    bt_out[0] = u[:, SSM_W:SSM_W + SSM_GROUPS * SSM_N].T.astype(jnp.bfloat16)
    cm_out[0] = u[:, SSM_W + SSM_GROUPS * SSM_N:]

    g = _silu(_dwconv5(gq_ref, gq_p, gq_n, gw_ref, ext_ref, has_prev, has_next))

    def l2n(t):
        return t * lax.rsqrt(_group_sum_sq(t, ones_ref) + EPS)

    q_out[0] = l2n(g[:, :GDN_W]) * (GDN_DK ** -0.5)
    k = l2n(g[:, GDN_W:2 * GDN_W])
    k_out[0] = k
    kt_out[0] = k.T.astype(jnp.bfloat16)
    v_out[0] = g[:, 2 * GDN_W:]

    sm = sm_ref[0] + bias_ref[...]
    sp = jnp.maximum(sm, 0.0) + jnp.log1p(jnp.exp(-jnp.abs(sm)))
    sg = jax.nn.sigmoid(sm)
    vals = sp * scale_ref[...]
    ri = lax.broadcasted_iota(jnp.int32, (TOK_TILE, TOK_TILE), 0)
    ci = lax.broadcasted_iota(jnp.int32, (TOK_TILE, TOK_TILE), 1)
    tri_pre = jnp.where(ri >= ci, 1.0, 0.0).astype(jnp.bfloat16)
    tri_suf = jnp.where(ri <= ci, 1.0, 0.0).astype(jnp.bfloat16)
    pre = None
    suf = None
    for term in _bf16_terms(vals, 3):
        a = jnp.dot(tri_pre, term, preferred_element_type=jnp.float32)
        b = jnp.dot(tri_suf, term, preferred_element_type=jnp.float32)
        pre = a if pre is None else pre + a
        suf = b if suf is None else suf + b
    lane = lax.broadcasted_iota(jnp.int32, (TOK_TILE, SMALL_W), 1)
    bwd_lane = ((lane >= L_SCUM + SSM_HEADS) & (lane < L_BETA)) | (lane >= L_GCUM + GDN_HEADS)
    cum = jnp.where(bwd_lane, suf, pre)
    cols = jnp.where(lane < L_SCUM, sp, jnp.where((lane >= L_BETA) & (lane < L_GCUM), sg, cum))
    cols_out[0] = cols
    rows_out[0] = cols.T


def _mixer_prep(xbc, gqkv, small, ssm_w, ssm_b, gdn_w, bias_vec, scale_vec):
    b, n, _ = xbc.shape
    nt = n // TOK_TILE
    hb = TOK_TILE // HALO
    last = n // HALO - 1
    ones_blk = _group_matrix(GDN_W, GDN_DK, 1.0)

    def cur(w):
        return pl.BlockSpec((1, TOK_TILE, w), lambda i, j: (i, j, 0))

    def prev(w):
        return pl.BlockSpec((1, HALO, w), lambda i, j: (i, jnp.maximum(j * hb - 1, 0), 0))

    def nxt(w):
        return pl.BlockSpec((1, HALO, w), lambda i, j: (i, jnp.minimum((j + 1) * hb, last), 0))

    def const(shape):
        return pl.BlockSpec(shape, lambda i, j: (0,) * len(shape))

    f32 = jnp.float32
    return pl.pallas_call(
        _mixer_prep_kernel,
        grid=(b, nt),
        in_specs=[cur(SSM_XBC), prev(SSM_XBC), nxt(SSM_XBC), cur(GDN_QKV), prev(GDN_QKV), nxt(GDN_QKV), cur(SMALL_W),
                  const((8, SSM_XBC)), const((1, SSM_XBC)), const((8, GDN_QKV)),
                  const((1, SMALL_W)), const((1, SMALL_W)), const((GDN_W, GDN_W))],
        out_specs=[cur(SSM_W), cur(SSM_GROUPS * SSM_N),
                   pl.BlockSpec((1, SSM_GROUPS * SSM_N, TOK_TILE), lambda i, j: (i, 0, j)),
                   cur(GDN_W), cur(GDN_W),
                   pl.BlockSpec((1, GDN_W, TOK_TILE), lambda i, j: (i, 0, j)),
                   cur(GDN_W), cur(SMALL_W),
                   pl.BlockSpec((1, SMALL_W, TOK_TILE), lambda i, j: (i, 0, j))],
        out_shape=[jax.ShapeDtypeStruct((b, n, SSM_W), f32),
                   jax.ShapeDtypeStruct((b, n, SSM_GROUPS * SSM_N), f32),
                   jax.ShapeDtypeStruct((b, SSM_GROUPS * SSM_N, n), jnp.bfloat16),
                   jax.ShapeDtypeStruct((b, n, GDN_W), f32),
                   jax.ShapeDtypeStruct((b, n, GDN_W), f32),
                   jax.ShapeDtypeStruct((b, GDN_W, n), jnp.bfloat16),
                   jax.ShapeDtypeStruct((b, n, GDN_W), f32),
                   jax.ShapeDtypeStruct((b, n, SMALL_W), f32),
                   jax.ShapeDtypeStruct((b, SMALL_W, n), f32)],
        scratch_shapes=[pltpu.VMEM((TOK_TILE + 2 * HALO, SSM_XBC), f32)],
        compiler_params=pltpu.CompilerParams(dimension_semantics=("parallel", "parallel")),
        name="mixer_prep",
    )(xbc, xbc, xbc, gqkv, gqkv, gqkv, small, ssm_w, ssm_b, gdn_w, bias_vec, scale_vec, ones_blk)


def _scan_tile(d, s, nt):
    return jnp.where(d == 0, s, jnp.where(s == 0, 0, nt - s))


def _pick(is_f, arr, base, stride, i, axis):
    a, b = base + i, base + stride + i
    if axis == 1:
        return jnp.where(is_f, arr[:, a:a + 1], arr[:, b:b + 1])
    return jnp.where(is_f, arr[a:a + 1, :], arr[b:b + 1, :])


def _order_masks(is_f):
    ri = lax.broadcasted_iota(jnp.int32, (TOK_TILE, TOK_TILE), 0)
    ci = lax.broadcasted_iota(jnp.int32, (TOK_TILE, TOK_TILE), 1)
    diff = (ri - ci) * jnp.where(is_f, 1, -1)
    return diff >= 0, diff > 0


def _ssd_kernel(xs_ref, cm_ref, bt_ref, cols_ref, rows_ref, y_ref, st_ref):
    is_f = pl.program_id(0) == 0

    @pl.when(pl.program_id(2) == 0)
    def _():
        st_ref[...] = jnp.zeros_like(st_ref)

    bf16 = jnp.bfloat16
    f32 = jnp.float32
    t = TOK_TILE
    cols = cols_ref[0]
    rows = rows_ref[0]
    incl, _ = _order_masks(is_f)
    lane128 = lax.broadcasted_iota(jnp.int32, (t, 128), 1)
    cm = cm_ref[0]
    bt = bt_ref[0]
    hpg = SSM_HEADS // SSM_GROUPS
    for g in range(SSM_GROUPS):
        cg = jnp.where((lane128 >= g * SSM_N) & (lane128 < (g + 1) * SSM_N), cm, 0.0).astype(bf16)
        cb = jnp.dot(cg, bt, preferred_element_type=f32)
        st_g = st_ref[:, g * hpg * SSM_P:(g + 1) * hpg * SSM_P]
        yoff = jnp.dot(cg, st_g.astype(bf16), preferred_element_type=f32)
        btg = bt[g * SSM_N:(g + 1) * SSM_N, :].astype(f32)
        for pair in range(hpg // 2):
            xp = xs_ref[0, :, (g * hpg + 2 * pair) * SSM_P:(g * hpg + 2 * pair + 2) * SSM_P]
            acc_y = None
            acc_s = None
            e_col = None
            dec = None
            for half in range(2):
                hd = g * hpg + 2 * pair + half
                cum_col = _pick(is_f, cols, L_SCUM, SSM_HEADS, hd, 1)
                cum_row = _pick(is_f, rows, L_SCUM, SSM_HEADS, hd, 0)
                dt_col = _pick(is_f, cols, L_DT, SSM_HEADS, hd, 1)
                cum_tot = jnp.where(is_f, cum_row[:, t - 1:t], cum_row[:, 0:1])
                lm = jnp.exp(jnp.where(incl, cum_col - cum_row, -jnp.inf))
                m = (cb * lm).astype(bf16)
                in_half = (lane128 >= half * SSM_P) & (lane128 < (half + 1) * SSM_P)
                xm = jnp.where(in_half, xp * dt_col, 0.0).astype(bf16)
                ty = jnp.dot(m, xm, preferred_element_type=f32)
                ts = jnp.dot((btg * jnp.exp(cum_tot - cum_row)).astype(bf16), xm, preferred_element_type=f32)
                acc_y = ty if acc_y is None else acc_y + ty
                acc_s = ts if acc_s is None else acc_s + ts
                ec = jnp.exp(cum_col)
                dc = jnp.exp(cum_tot)
                e_col = ec if e_col is None else jnp.where(in_half, ec, e_col)
                dec = dc if dec is None else jnp.where(in_half[0:1, :], dc, dec)
            lo = pair * 2 * SSM_P
            c0 = g * hpg * SSM_P + lo
            y_ref[0, 0, :, c0:c0 + 2 * SSM_P] = acc_y + yoff[:, lo:lo + 2 * SSM_P] * e_col
            r0 = g * SSM_N
            st_ref[r0:r0 + SSM_N, c0:c0 + 2 * SSM_P] = st_ref[r0:r0 + SSM_N, c0:c0 + 2 * SSM_P] * dec + acc_s


def _ssd_scan(xs, cm, bt, cols, rows):
    b, n, _ = xs.shape
    nt = n // TOK_TILE

    def tok(w):
        return pl.BlockSpec((1, TOK_TILE, w), lambda d, i, s: (i, _scan_tile(d, s, nt), 0))

    def tr(w):
        return pl.BlockSpec((1, w, TOK_TILE), lambda d, i, s: (i, 0, _scan_tile(d, s, nt)))

    return pl.pallas_call(
        _ssd_kernel,
        grid=(2, b, nt),
        in_specs=[tok(SSM_W), tok(SSM_GROUPS * SSM_N), tr(SSM_GROUPS * SSM_N), tok(SMALL_W), tr(SMALL_W)],
        out_specs=pl.BlockSpec((1, 1, TOK_TILE, SSM_W), lambda d, i, s: (d, i, _scan_tile(d, s, nt), 0)),
        out_shape=jax.ShapeDtypeStruct((2, b, n, SSM_W), jnp.float32),
        scratch_shapes=[pltpu.VMEM((SSM_GROUPS * SSM_N, SSM_W), jnp.float32)],
        compiler_params=pltpu.CompilerParams(dimension_semantics=("parallel", "parallel", "arbitrary")),
        name="ssd_scan",
    )(xs, cm, bt, cols, rows)


def _gdn_kernel(q_ref, k_ref, kt_ref, v_ref, cols_ref, rows_ref, o_ref, s_ref):
    is_f = pl.program_id(0) == 0

    @pl.when(pl.program_id(2) == 0)
    def _():
        s_ref[...] = jnp.zeros_like(s_ref)

    bf16 = jnp.bfloat16
    f32 = jnp.float32
    t = TOK_TILE
    cols = cols_ref[0]
    rows = rows_ref[0]
    incl, strict = _order_masks(is_f)
    lane = lax.broadcasted_iota(jnp.int32, (t, GDN_W), 1)
    sub = lax.broadcasted_iota(jnp.int32, (GDN_W, t), 0)
    head_of_lane = [(lane >= h * GDN_DK) & (lane < (h + 1) * GDN_DK) for h in range(GDN_HEADS)]
    head_of_sub = [(sub >= h * GDN_DK) & (sub < (h + 1) * GDN_DK) for h in range(GDN_HEADS)]

    def by_lane(pieces):
        out = pieces[0]
        for h in range(1, GDN_HEADS):
            out = jnp.where(head_of_lane[h] if pieces[h].shape[0] != 1 else head_of_lane[h][0:1, :], pieces[h], out)
        return out

    q = q_ref[0]
    k = k_ref[0]
    v = v_ref[0]
    kt = kt_ref[0]
    gc_col = [_pick(is_f, cols, L_GCUM, GDN_HEADS, h, 1) for h in range(GDN_HEADS)]
    gc_row = [_pick(is_f, rows, L_GCUM, GDN_HEADS, h, 0) for h in range(GDN_HEADS)]
    gc_end = [jnp.where(is_f, r[:, t - 1:t], r[:, 0:1]) for r in gc_row]
    beta_x = by_lane([_pick(is_f, cols, L_BETA, GDN_HEADS, h, 1) + jnp.zeros((t, GDN_W), f32) for h in range(GDN_HEADS)])
    egc = jnp.exp(by_lane([c + jnp.zeros((t, GDN_W), f32) for c in gc_col]))
    kb = k * beta_x
    vb = v * beta_x
    kbg = kb * egc
    qd = q * egc
    ri = lax.broadcasted_iota(jnp.int32, (t, t), 0)
    ci = lax.broadcasted_iota(jnp.int32, (t, t), 1)
    eye = jnp.where(ri == ci, 1.0, 0.0)
    off_levels = []
    size = 1
    while size < t:
        off_levels.append(((ri // (2 * size)) == (ci // (2 * size))) & ((ri // size) != (ci // size)))
        size *= 2

    uw = None
    qkd = []
    for h in range(GDN_HEADS):
        kbm = jnp.where(head_of_lane[h], kb, 0.0).astype(bf16)
        qm = jnp.where(head_of_lane[h], q, 0.0).astype(bf16)
        kk = jnp.dot(kbm, kt, preferred_element_type=f32)
        qk = jnp.dot(qm, kt, preferred_element_type=f32)
        dec = jnp.exp(jnp.where(incl, gc_col[h] - gc_row[h], -jnp.inf))
        a = jnp.where(strict, kk * dec, 0.0)
        qkd.append((qk * dec).astype(bf16))
        x = eye
        for lvl_mask in off_levels:
            xb = x.astype(bf16)
            e = jnp.where(lvl_mask, a, 0.0).astype(bf16)
            x = x - jnp.dot(jnp.dot(xb, e, preferred_element_type=f32).astype(bf16), xb, preferred_element_type=f32)
        rhs = jnp.concatenate([jnp.where(head_of_lane[h], vb, 0.0), jnp.where(head_of_lane[h], kbg, 0.0)], axis=1)
        term = jnp.dot(x.astype(bf16), rhs.astype(bf16), preferred_element_type=f32)
        uw = term if uw is None else uw + term

    s_old = s_ref[...]
    sb = s_old.astype(bf16)
    u = uw[:, :GDN_W]
    w = uw[:, GDN_W:]
    v_new = u - jnp.dot(w.astype(bf16), sb, preferred_element_type=f32)
    o = jnp.dot(qd.astype(bf16), sb, preferred_element_type=f32)
    for h in range(GDN_HEADS):
        o = o + jnp.dot(qkd[h], jnp.where(head_of_lane[h], v_new, 0.0).astype(bf16), preferred_element_type=f32)
    o_ref[0, 0] = o

    mult = jnp.exp(gc_end[0] - gc_row[0]) + jnp.zeros((GDN_W, t), f32)
    for h in range(1, GDN_HEADS):
        mult = jnp.where(head_of_sub[h], jnp.exp(gc_end[h] - gc_row[h]), mult)
    kend_t = (kt.astype(f32) * mult).astype(bf16)
    upd = jnp.dot(kend_t, v_new.astype(bf16), preferred_element_type=f32)
    g_end = jnp.exp(by_lane(gc_end))
    same_head = head_of_sub[0] & head_of_lane[0]
    for h in range(1, GDN_HEADS):
        same_head = same_head | (head_of_sub[h] & head_of_lane[h])
    s_ref[...] = jnp.where(same_head, s_old * g_end + upd, 0.0)


def _gdn_scan(q, k, kt, v, cols, rows):
    b, n, _ = q.shape
    nt = n // TOK_TILE

    def tok(w):
        return pl.BlockSpec((1, TOK_TILE, w), lambda d, i, s: (i, _scan_tile(d, s, nt), 0))

    def tr(w):
        return pl.BlockSpec((1, w, TOK_TILE), lambda d, i, s: (i, 0, _scan_tile(d, s, nt)))

    return pl.pallas_call(
        _gdn_kernel,
        grid=(2, b, nt),
        in_specs=[tok(GDN_W), tok(GDN_W), tr(GDN_W), tok(GDN_W), tok(SMALL_W), tr(SMALL_W)],
        out_specs=pl.BlockSpec((1, 1, TOK_TILE, GDN_W), lambda d, i, s: (d, i, _scan_tile(d, s, nt), 0)),
        out_shape=jax.ShapeDtypeStruct((2, b, n, GDN_W), jnp.float32),
        scratch_shapes=[pltpu.VMEM((GDN_W, GDN_W), jnp.float32)],
        compiler_params=pltpu.CompilerParams(dimension_semantics=("parallel", "parallel", "arbitrary")),
        name="gdn_scan",
    )(q, k, kt, v, cols, rows)


def _small_lane_params(ssm_dt_bias, ssm_a_log, gdn_dt_bias, gdn_a_log):
    f32 = jnp.float32
    z = lambda k: jnp.zeros((k,), f32)
    sb = ssm_dt_bias.astype(f32).reshape(-1)
    bias = jnp.concatenate([sb, sb, z(L_GCUM - L_BETA), gdn_dt_bias.astype(f32).reshape(-1),
                            z(SMALL_W - L_GCUM - 2 * GDN_HEADS)])
    scale = jnp.concatenate([z(L_SCUM), -jnp.exp(ssm_a_log.astype(f32)).reshape(-1), z(L_GCUM - L_BETA),
                             -jnp.exp(gdn_a_log.astype(f32)).reshape(-1), z(SMALL_W - L_GCUM - 2 * GDN_HEADS)])
    return bias[None, :], scale[None, :]


def _permuted_w_in(w_in):
    offs = np.cumsum((0,) + SPLIT_SIZES)
    dt0, beta0, a0 = int(offs[3]), int(offs[6]), int(offs[7])
    dt = list(range(dt0, dt0 + SSM_HEADS))
    small = dt + dt + dt + dt + list(range(beta0, beta0 + 2 * GDN_HEADS)) + list(range(a0, a0 + 2 * GDN_HEADS))
    cols = list(range(0, dt0)) + list(range(int(offs[4]), beta0)) + small
    w = w_in[:, np.asarray(cols)]
    return jnp.pad(w, ((0, 0), (0, SMALL_W - len(small))))


def mixers_pallas(xbc, gqkv, small, ssm_conv_w, ssm_conv_b, ssm_dt_bias, ssm_a_log, gdn_conv_w, gdn_dt_bias, gdn_a_log):
    bias_vec, scale_vec = _small_lane_params(ssm_dt_bias, ssm_a_log, gdn_dt_bias, gdn_a_log)
    pad = ((0, 8 - CONV_K), (0, 0))
    xs, cm, bt, q, k, kt, v, cols, rows = _mixer_prep(
        xbc, gqkv, small, jnp.pad(ssm_conv_w, pad), ssm_conv_b[None, :], jnp.pad(gdn_conv_w, pad), bias_vec, scale_vec)
    y = _ssd_scan(xs, cm, bt, cols, rows)
    o = _gdn_scan(q, k, kt, v, cols, rows)
    return xs, y, o


def _expert_kernel(be_ref, nu_ref, x_ref, wg_ref, wu_ref, wd_ref, o_ref):
    i = pl.program_id(0)

    @pl.when(i < nu_ref[0])
    def _():
        x = x_ref[...]
        g = jnp.dot(x, wg_ref[0], preferred_element_type=jnp.float32)
        u = jnp.dot(x, wu_ref[0], preferred_element_type=jnp.float32)
        h = (_silu(g) * u).astype(jnp.bfloat16)
        o_ref[...] = jnp.dot(h, wd_ref[0], preferred_element_type=jnp.float32)

    @pl.when(i >= nu_ref[0])
    def _():
        o_ref[...] = jnp.zeros_like(o_ref)


def _expert_blocks(block_e, n_used, buf, w_gate, w_up, w_down):
    n_rows, d = buf.shape
    n_blocks = n_rows // MOE_BLOCK
    grid_spec = pltpu.PrefetchScalarGridSpec(
        num_scalar_prefetch=2,
        grid=(n_blocks,),
        in_specs=[pl.BlockSpec((MOE_BLOCK, d), lambda i, be, nu: (i, 0)),
                  pl.BlockSpec((1, d, D_EXPERT), lambda i, be, nu: (be[i], 0, 0)),
                  pl.BlockSpec((1, d, D_EXPERT), lambda i, be, nu: (be[i], 0, 0)),
                  pl.BlockSpec((1, D_EXPERT, d), lambda i, be, nu: (be[i], 0, 0))],
        out_specs=pl.BlockSpec((MOE_BLOCK, d), lambda i, be, nu: (i, 0)),
    )
    return pl.pallas_call(
        _expert_kernel,
        grid_spec=grid_spec,
        out_shape=jax.ShapeDtypeStruct((n_rows, d), jnp.float32),
        compiler_params=pltpu.CompilerParams(dimension_semantics=("arbitrary",)),
        name="moe_experts",
    )(block_e, n_used, buf, w_gate, w_up, w_down)


def routed_experts(t, experts, gates, w_gate, w_up, w_down):
    n, d = t.shape
    n_assign = n * TOP_K
    flat_e = experts.reshape(n_assign)
    order = jnp.argsort(flat_e)
    sorted_e = flat_e[order]
    counts = jax.ops.segment_sum(jnp.ones((n_assign,), jnp.int32), flat_e, num_segments=N_EXPERTS)
    starts = jnp.cumsum(counts) - counts
    padded = (counts + MOE_BLOCK - 1) // MOE_BLOCK * MOE_BLOCK
    pad_ends = jnp.cumsum(padded)
    pad_starts = pad_ends - padded
    dest = pad_starts[sorted_e] + jnp.arange(n_assign, dtype=jnp.int32) - starts[sorted_e]
    n_blocks = -(-n_assign // MOE_BLOCK) + N_EXPERTS
    src_tok = jnp.zeros((n_blocks * MOE_BLOCK,), jnp.int32).at[dest].set((order // TOP_K).astype(jnp.int32))
    buf = t[src_tok]
    block_start = jnp.arange(n_blocks, dtype=jnp.int32) * MOE_BLOCK
    block_e = jnp.minimum(jnp.searchsorted(pad_ends, block_start, side='right'), N_EXPERTS - 1).astype(jnp.int32)
    n_used = (pad_ends[-1] // MOE_BLOCK).astype(jnp.int32).reshape(1)
    out = _expert_blocks(block_e, n_used, buf, w_gate, w_up, w_down)
    pos = jnp.zeros((n_assign,), jnp.int32).at[order].set(dest.astype(jnp.int32))
    y_assign = out[pos]
    return jnp.sum(y_assign.reshape(n, TOP_K, d) * gates[..., None].astype(out.dtype), axis=1)


def hier_moe(t, t_rows, router_g_w, router_g_b, router_e_w, router_e_b, w_gate, w_up, w_down):
    n = t.shape[0]
    grp_prob = jax.nn.softmax(jnp.dot(t, router_g_w, preferred_element_type=jnp.float32)
                              + router_g_b.astype(jnp.float32), axis=-1)
    p_grp, grp = lax.top_k(grp_prob, 1)
    e_logits = (jnp.dot(t, router_e_w, preferred_element_type=jnp.float32)
                + router_e_b.astype(jnp.float32)).reshape(n, N_EGROUPS, EXPERTS_PER_GROUP)
    sel = e_logits[jnp.arange(n), grp[:, 0]]
    p_top, idx = lax.top_k(jax.nn.softmax(sel, axis=-1), TOP_K)
    gates = p_grp * p_top / jnp.sum(p_top, axis=-1, keepdims=True)
    experts = grp * EXPERTS_PER_GROUP + idx
    return routed_experts(t_rows, experts, gates, w_gate, w_up, w_down)


def _matmul_kernel(a_ref, b_ref, o_ref):
    o_ref[...] = jnp.dot(a_ref[...].astype(jnp.bfloat16), b_ref[...],
                         preferred_element_type=jnp.float32)


def _matmul(a, b, tm, tn):
    m, k = a.shape
    n = b.shape[1]
    return pl.pallas_call(
        _matmul_kernel,
        grid=(m // tm, n // tn),
        in_specs=[pl.BlockSpec((tm, k), lambda i, j: (i, 0)),
                  pl.BlockSpec((k, tn), lambda i, j: (0, j))],
        out_specs=pl.BlockSpec((tm, tn), lambda i, j: (i, j)),
        out_shape=jax.ShapeDtypeStruct((m, n), jnp.float32),
        compiler_params=pltpu.CompilerParams(dimension_semantics=("parallel", "parallel")),
        name="dense",
    )(a, b)


def _dense(a, w, tm=512):
    lead = a.shape[:-1]
    k = a.shape[-1]
    n = w.shape[1]
    a2 = a.reshape(-1, k)
    m = a2.shape[0]
    m_pad = -(-m // 8) * 8
    if m_pad != m:
        a2 = jnp.pad(a2, ((0, m_pad - m), (0, 0)))
    tm = min(tm, m_pad)
    n_pad = -(-n // 128) * 128
    tn = next(c for c in (1024, 768, 640, 512, 384, 256, 128) if n_pad % c == 0)
    wb = jnp.pad(w, ((0, 0), (0, n_pad - n))).astype(jnp.bfloat16)
    out = _matmul(a2, wb, tm, tn)[:m, :n]
    return out.reshape(lead + (n,))


IN_SLABS = (3 * DIFF_W, SSM_W, SSM_XBC, GDN_QKV, GDN_W, SMALL_W)
PROJ_VMEM_BYTES = 44 * 1024 * 1024


def _mod_row_spec(bsz):
    return pl.BlockSpec((1, MOD_CHUNKS, D_MODEL), lambda i, j: (jnp.where(j == 0, bsz, i), 0, 0))


def _in_proj_kernel(x_ref, nw_ref, mod_ref, w_ref, *out_refs):
    x = x_ref[0]
    xn = x * lax.rsqrt(jnp.mean(x * x, axis=-1, keepdims=True) + EPS) * nw_ref[...]
    mod = mod_ref[0]
    xm = (xn * (1.0 + mod[1:2, :]) + mod[0:1, :]).astype(jnp.bfloat16)
    c0 = 0
    for o_ref, width in zip(out_refs, IN_SLABS):
        o_ref[0] = jnp.dot(xm, w_ref[:, c0:c0 + width], preferred_element_type=jnp.float32)
        c0 += width


def _in_proj(x, norm_w, mod, w):
    b, n, d = x.shape
    tok = lambda width: pl.BlockSpec((1, TOK_TILE, width), lambda i, j: (i, j, 0))
    return pl.pallas_call(
        _in_proj_kernel,
        grid=(b, n // TOK_TILE),
        in_specs=[tok(d), pl.BlockSpec((1, d), lambda i, j: (0, 0)), _mod_row_spec(b),
                  pl.BlockSpec(w.shape, lambda i, j: (0, 0))],
        out_specs=[tok(width) for width in IN_SLABS],
        out_shape=[jax.ShapeDtypeStruct((b, n, width), jnp.float32) for width in IN_SLABS],
        compiler_params=pltpu.CompilerParams(dimension_semantics=("parallel", "parallel"),
                                             vmem_limit_bytes=PROJ_VMEM_BYTES),
        name="in_proj",
    )(x, norm_w, mod, w)


def _out_proj_kernel(att_ref, y0_ref, y1_ref, xs_ref, z_ref, o0_ref, o1_ref, gate_ref, x_ref, mod_ref,
                     dvec_ref, snw_ref, gnw_ref, n2w_ref, g_ref, w_ref, xo_ref, f_ref, fb_ref):
    ys = (y0_ref[0, 0] + y1_ref[0, 0] + dvec_ref[...] * xs_ref[0]) * _silu(z_ref[0])
    gw = SSM_W // SSM_GROUPS
    parts = [att_ref[0]]
    for g in range(SSM_GROUPS):
        yg = ys[:, g * gw:(g + 1) * gw]
        parts.append(yg * lax.rsqrt(jnp.mean(yg * yg, axis=-1, keepdims=True) + EPS) * snw_ref[:, g * gw:(g + 1) * gw])
    o = o0_ref[0, 0] + o1_ref[0, 0]
    parts.append(o * lax.rsqrt(_group_sum_sq(o, g_ref) + EPS) * gnw_ref[...] * _silu(gate_ref[0]))
    ml = jnp.concatenate(parts, axis=1).astype(jnp.bfloat16)
    mod = mod_ref[0]
    xn = x_ref[0] + mod[2:3, :] * jnp.dot(ml, w_ref[...], preferred_element_type=jnp.float32)
    xo_ref[0] = xn
    f = xn * lax.rsqrt(jnp.mean(xn * xn, axis=-1, keepdims=True) + EPS) * n2w_ref[...]
    f = f * (1.0 + mod[4:5, :]) + mod[3:4, :]
    f_ref[0] = f
    fb_ref[0] = f.astype(jnp.bfloat16)


def _out_proj(att, y, xs, z, o, gate, x, mod, dvec, snw, gnw, n2w, g64, w):
    b, n, d = x.shape
    tok = lambda width: pl.BlockSpec((1, TOK_TILE, width), lambda i, j: (i, j, 0))
    dirs = lambda width, k: pl.BlockSpec((1, 1, TOK_TILE, width), lambda i, j: (k, i, j, 0))
    vec = lambda width: pl.BlockSpec((1, width), lambda i, j: (0, 0))
    return pl.pallas_call(
        _out_proj_kernel,
        grid=(b, n // TOK_TILE),
        in_specs=[tok(DIFF_W), dirs(SSM_W, 0), dirs(SSM_W, 1), tok(SSM_W), tok(SSM_W), dirs(GDN_W, 0), dirs(GDN_W, 1),
                  tok(GDN_W), tok(d), _mod_row_spec(b), vec(SSM_W), vec(SSM_W), vec(GDN_W), vec(d),
                  pl.BlockSpec(g64.shape, lambda i, j: (0, 0)), pl.BlockSpec(w.shape, lambda i, j: (0, 0))],
        out_specs=[tok(d), tok(d), tok(d)],
        out_shape=[jax.ShapeDtypeStruct((b, n, d), jnp.float32), jax.ShapeDtypeStruct((b, n, d), jnp.float32),
                   jax.ShapeDtypeStruct((b, n, d), jnp.bfloat16)],
        compiler_params=pltpu.CompilerParams(dimension_semantics=("parallel", "parallel"),
                                             vmem_limit_bytes=PROJ_VMEM_BYTES),
        name="out_proj",
    )(att, y, y, xs, z, o, o, gate, x, mod, dvec, snw, gnw, n2w, g64, w)


def kernel(x, c, ctx, c_ctx, w_mod, b_mod, norm1_w, norm2_w, w_in, w_out,
           diff_qn_w, diff_kn_w, diff_lq1, diff_lk1, diff_lq2, diff_lk2, diff_norm_w,
           ssm_conv_w, ssm_conv_b, ssm_dt_bias, ssm_a_log, ssm_d, ssm_norm_w,
           gdn_conv_w, gdn_dt_bias, gdn_a_log, gdn_norm_w,
           router_g_w, router_g_b, router_e_w, router_e_b, exp_w_gate, exp_w_up, exp_w_down):
    assert ctx.shape[1] == CTX_LEN == TOK_TILE and x.shape[1] % TOK_TILE == 0
    bsz = x.shape[0]
    xs_all = jnp.concatenate([ctx, x], axis=1)
    n_tok = xs_all.shape[1]
    is_lat = (jnp.arange(n_tok) >= CTX_LEN)[None, :, None]
    cc = jnp.concatenate([c, c_ctx[None, :]], axis=0)
    bf16 = jnp.bfloat16
    g64 = _group_matrix(GDN_W, GDN_DV, 1.0 / GDN_DV)
    for l in range(DEPTH):
        last = l == DEPTH - 1
        lam_init = 0.8 - 0.6 * math.exp(-0.3 * l)
        mod = (_dense(jax.nn.silu(cc), w_mod[l]) + b_mod[l]).reshape(bsz + 1, MOD_CHUNKS, D_MODEL)
        qkv, z, xbc, gqkv, gate, small = _in_proj(xs_all, norm1_w[l][None, :], mod, _permuted_w_in(w_in[l]).astype(bf16))
        att = diff_attention_pallas(qkv, lam_init, diff_qn_w[l], diff_kn_w[l],
                                    diff_lq1[l], diff_lk1[l], diff_lq2[l], diff_lk2[l], diff_norm_w[l])
        xs, y, o = mixers_pallas(xbc, gqkv, small, ssm_conv_w[l], ssm_conv_b[l],
                                 ssm_dt_bias[l], ssm_a_log[l], gdn_conv_w[l], gdn_dt_bias[l], gdn_a_log[l])
        xs_all, f, fb = _out_proj(att, y, xs, z, o, gate, xs_all, mod, jnp.repeat(ssm_d[l], SSM_P)[None, :],
                                  ssm_norm_w[l][None, :], jnp.tile(gdn_norm_w[l], GDN_HEADS)[None, :],
                                  norm2_w[l][None, :], g64, w_out[l].astype(bf16))
        moe_w = (router_g_w[l], router_g_b[l], router_e_w[l], router_e_b[l],
                 exp_w_gate[l].astype(bf16), exp_w_up[l].astype(bf16), exp_w_down[l].astype(bf16))
        if last:
            xl = xs_all[:, CTX_LEN:]
            yl = hier_moe(f[:, CTX_LEN:].reshape(-1, D_MODEL), fb[:, CTX_LEN:].reshape(-1, D_MODEL), *moe_w)
            return xl + mod[:bsz, None, 5, :] * yl.reshape(xl.shape)
        ym = hier_moe(f.reshape(-1, D_MODEL), fb.reshape(-1, D_MODEL), *moe_w)
        mod5 = jnp.where(is_lat, mod[:bsz, None, 5, :], mod[bsz, 5, :])
        xs_all = xs_all + mod5 * ym.reshape(xs_all.shape)
```

```python
import functools
import math
import jax
import jax.numpy as jnp
from jax import lax
import numpy as np
from jax.experimental import pallas as pl
from jax.experimental.pallas import tpu as pltpu

D_MODEL = 1024
DEPTH = 2
CTX_LEN = 256
GRID_W = 64
EPS = 1e-6
MOD_CHUNKS = 6

DIFF_HEADS = 4
DIFF_QK = 32
DIFF_V = 2 * DIFF_QK
DIFF_W = DIFF_HEADS * DIFF_V
ROPE_BASE = 10000.0

SSM_HEADS = 8
SSM_P = 64
SSM_GROUPS = 2
SSM_N = 64
SSM_W = SSM_HEADS * SSM_P
SSM_XBC = SSM_W + 2 * SSM_GROUPS * SSM_N
CONV_K = 5

GDN_HEADS = 4
GDN_DK = 64
GDN_DV = 64
GDN_QKV = GDN_HEADS * (2 * GDN_DK + GDN_DV)
GDN_W = GDN_HEADS * GDN_DV

D_MIX = DIFF_W + SSM_W + GDN_W
SPLIT_SIZES = (3 * DIFF_W, SSM_W, SSM_XBC, SSM_HEADS, GDN_QKV, GDN_W, 2 * GDN_HEADS, 2 * GDN_HEADS)

N_EGROUPS = 4
EXPERTS_PER_GROUP = 8
N_EXPERTS = N_EGROUPS * EXPERTS_PER_GROUP
TOP_K = 2
D_EXPERT = 512
MOE_BLOCK = 256

TOK_TILE = 256
LOG2E = 1.4426950408889634
EXP_ROWS = 64
ATTN_VMEM_BYTES = 52 * 1024 * 1024
MOE_VMEM_BYTES = 40 * 1024 * 1024
HALO = 8
SMALL_W = 128
L_DT = 0
L_SCUM = 16
L_BETA = 32
L_GCUM = 40


def rmsnorm(x, w):
    x32 = x.astype(jnp.float32)
    y = x32 * lax.rsqrt(jnp.mean(x32 * x32, axis=-1, keepdims=True) + EPS)
    return y.astype(x.dtype) * w


def modulate(x, shift, scale):
    return x * (1.0 + scale) + shift


def _bf16_terms(x, n):
    out = []
    for _ in range(n):
        t = x.astype(jnp.bfloat16)
        out.append(t)
        x = x - t.astype(jnp.float32)
    return out


def _group_sum_sq(x, g_ref):
    hi, lo = _bf16_terms(x * x, 2)
    g = g_ref[...]
    return (jnp.dot(hi, g, preferred_element_type=jnp.float32)
            + jnp.dot(lo, g, preferred_element_type=jnp.float32))


def _group_matrix(width, group, value):
    idx = np.arange(width) // group
    return jnp.asarray((idx[:, None] == idx[None, :]).astype(np.float32) * value, jnp.bfloat16)


def _attn_prep_kernel(qkv_ref, cos_ref, sin_ref, wq_ref, wk_ref, g_ref, qt_out, k_out, vt_out):
    x = qkv_ref[0]
    cos = cos_ref[...]
    sin = sin_ref[...]
    lane = lax.broadcasted_iota(jnp.int32, cos.shape, 1)
    lo_half = (lane % 16) < 8

    def norm_rope(t, w):
        y = t * lax.rsqrt(_group_sum_sq(t, g_ref) + EPS) * w
        rot = jnp.where(lo_half, pltpu.roll(y, DIFF_W - 8, 1), pltpu.roll(y, 8, 1))
        return y * cos + rot * sin

    q = norm_rope(x[:, 0:DIFF_W], wq_ref[...]) * (DIFF_QK ** -0.5 * LOG2E)
    qt_out[0] = q.T
    k_out[0] = norm_rope(x[:, DIFF_W:2 * DIFF_W], wk_ref[...]).astype(jnp.bfloat16)
    vt_out[0] = x[:, 2 * DIFF_W:3 * DIFF_W].T.astype(jnp.bfloat16)


def _attn_prep(qkv, cos_t, sin_t, wq, wk, gmat):
    b, n, _ = qkv.shape
    nt = n // TOK_TILE
    tok = pl.BlockSpec((1, TOK_TILE, DIFF_W), lambda i, j: (i, j, 0))
    tr = pl.BlockSpec((1, DIFF_W, TOK_TILE), lambda i, j: (i, 0, j))
    return pl.pallas_call(
        _attn_prep_kernel,
        grid=(b, nt),
        in_specs=[pl.BlockSpec((1, TOK_TILE, 3 * DIFF_W), lambda i, j: (i, j, 0)),
                  pl.BlockSpec((TOK_TILE, DIFF_W), lambda i, j: (j, 0)),
                  pl.BlockSpec((TOK_TILE, DIFF_W), lambda i, j: (j, 0)),
                  pl.BlockSpec((1, DIFF_W), lambda i, j: (0, 0)),
                  pl.BlockSpec((1, DIFF_W), lambda i, j: (0, 0)),
                  pl.BlockSpec((DIFF_W, DIFF_W), lambda i, j: (0, 0))],
        out_specs=[tr, tok, tr],
        out_shape=[jax.ShapeDtypeStruct((b, DIFF_W, n), jnp.float32),
                   jax.ShapeDtypeStruct((b, n, DIFF_W), jnp.bfloat16),
                   jax.ShapeDtypeStruct((b, DIFF_W, n), jnp.bfloat16)],
        compiler_params=pltpu.CompilerParams(dimension_semantics=("parallel", "parallel")),
        name="attn_prep",
    )(qkv, cos_t, sin_t, wq, wk, gmat)


def _attn_kernel(lam_ref, qt_ref, k_ref, vt_ref, wo_ref, g_ref, o_ref, s_ref, p_ref, *, n_keys, out_scale):
    tile = pl.program_id(1)
    lam = lam_ref[0]
    sub = lax.broadcasted_iota(jnp.int32, (DIFF_W, TOK_TILE), 0)

    def attend(nk):
        n_chunks = nk // TOK_TILE
        qt = qt_ref[0]

        def masked_qt(head):
            return jnp.concatenate(
                [jnp.where((sub >= (2 * head + mp) * DIFF_QK) & (sub < (2 * head + mp + 1) * DIFF_QK), qt, 0.0)
                 for mp in range(2)], axis=1).astype(jnp.bfloat16)

        def score_chunk(qt2, buf, c, mx8):
            s = jnp.dot(k_ref[0, c * TOK_TILE:(c + 1) * TOK_TILE, :], qt2, preferred_element_type=jnp.float32)
            s_ref[buf, c * TOK_TILE:(c + 1) * TOK_TILE, :] = s
            cm = jnp.max(s.reshape(TOK_TILE // 8, 8, 2 * TOK_TILE), axis=0)
            return cm if mx8 is None else jnp.maximum(mx8, cm)

        mx8 = None
        qt2 = masked_qt(0)
        for c in range(n_chunks):
            mx8 = score_chunk(qt2, 0, c, mx8)
        acc = None
        for head in range(DIFF_HEADS):
            buf = head % 2
            mx = jnp.max(mx8, axis=0, keepdims=True)
            nxt = head + 1 < DIFF_HEADS
            if nxt:
                qt2 = masked_qt(head + 1)
            mx8 = None
            ls8 = None
            for c in range(n_chunks):
                if nxt:
                    mx8 = score_chunk(qt2, 1 - buf, c, mx8)
                for r in range(c * TOK_TILE // EXP_ROWS, (c + 1) * TOK_TILE // EXP_ROWS):
                    e = jnp.exp2(s_ref[buf, r * EXP_ROWS:(r + 1) * EXP_ROWS, :] - mx)
                    p_ref[r * EXP_ROWS:(r + 1) * EXP_ROWS, :] = e.astype(jnp.bfloat16)
                    es = jnp.sum(e.reshape(EXP_ROWS // 8, 8, 2 * TOK_TILE), axis=0)
                    ls8 = es if ls8 is None else ls8 + es
            ls = jnp.sum(ls8, axis=0, keepdims=True)
            ot = jnp.dot(vt_ref[0, :, :nk], p_ref[:nk, :], preferred_element_type=jnp.float32) / ls
            in_head = (sub >= head * DIFF_V) & (sub < (head + 1) * DIFF_V)
            part = jnp.where(in_head, ot[:, :TOK_TILE] - lam * ot[:, TOK_TILE:], 0.0)
            acc = part if acc is None else acc + part
        o = acc.T
        y = o * lax.rsqrt(_group_sum_sq(o, g_ref) + EPS)
        o_ref[0] = y * wo_ref[...] * out_scale

    @pl.when(tile == 0)
    def _():
        attend(TOK_TILE)

    @pl.when(tile > 0)
    def _():
        attend(n_keys)


def _attention(lam, qt, k, vt, wo, gmat, out_scale):
    b, n, _ = k.shape
    nt = n // TOK_TILE
    kern = functools.partial(_attn_kernel, n_keys=n, out_scale=out_scale)
    return pl.pallas_call(
        kern,
        grid=(b, nt),
        in_specs=[pl.BlockSpec(memory_space=pltpu.SMEM),
                  pl.BlockSpec((1, DIFF_W, TOK_TILE), lambda i, j: (i, 0, j)),
                  pl.BlockSpec((1, n, DIFF_W), lambda i, j: (i, 0, 0)),
                  pl.BlockSpec((1, DIFF_W, n), lambda i, j: (i, 0, 0)),
                  pl.BlockSpec((1, DIFF_W), lambda i, j: (0, 0)),
                  pl.BlockSpec((DIFF_W, DIFF_W), lambda i, j: (0, 0))],
        out_specs=pl.BlockSpec((1, TOK_TILE, DIFF_W), lambda i, j: (i, j, 0)),
        out_shape=jax.ShapeDtypeStruct((b, n, DIFF_W), jnp.float32),
        scratch_shapes=[pltpu.VMEM((2, n, 2 * TOK_TILE), jnp.float32),
                        pltpu.VMEM((n, 2 * TOK_TILE), jnp.bfloat16)],
        compiler_params=pltpu.CompilerParams(dimension_semantics=("parallel", "arbitrary"),
                                             vmem_limit_bytes=ATTN_VMEM_BYTES),
        name="diff_attn",
    )(lam, qt, k, vt, wo, gmat)


def _rope_tables(n_latent):
    rows = n_latent // GRID_W
    row = np.repeat(np.arange(rows, dtype=np.float32), GRID_W)
    col = np.tile(np.arange(GRID_W, dtype=np.float32), rows)
    half = DIFF_QK // 2
    inv = (ROPE_BASE ** (-np.arange(0, half, 2, dtype=np.float32) / half)).astype(np.float32)
    ang = np.concatenate([row[:, None] * inv, row[:, None] * inv, col[:, None] * inv, col[:, None] * inv], axis=-1)
    cos = np.concatenate([np.ones((TOK_TILE, DIFF_QK), np.float32), np.cos(ang)], axis=0)
    sin = np.concatenate([np.zeros((TOK_TILE, DIFF_QK), np.float32), np.sin(ang)], axis=0)
    sign = np.where((np.arange(DIFF_QK) % 16) < 8, -1.0, 1.0).astype(np.float32)
    reps = DIFF_W // DIFF_QK
    return jnp.asarray(np.tile(cos, (1, reps))), jnp.asarray(np.tile(sin * sign, (1, reps)))


def diff_attention_pallas(qkv, lam_init, qn_w, kn_w, lq1, lk1, lq2, lk2, out_w):
    n = qkv.shape[1]
    cos_t, sin_t = _rope_tables(n - TOK_TILE)
    g32 = _group_matrix(DIFF_W, DIFF_QK, 1.0 / DIFF_QK)
    g64 = _group_matrix(DIFF_W, DIFF_V, 1.0 / DIFF_V)
    reps = DIFF_W // DIFF_QK
    qt, k, vt = _attn_prep(qkv, cos_t, sin_t, jnp.tile(qn_w, reps)[None, :], jnp.tile(kn_w, reps)[None, :], g32)
    lam = (jnp.exp(jnp.sum(lq1 * lk1)) - jnp.exp(jnp.sum(lq2 * lk2)) + lam_init).reshape(1).astype(jnp.float32)
    return _attention(lam, qt, k, vt, jnp.tile(out_w, DIFF_HEADS)[None, :], g64, 1.0 - lam_init)


def _dwconv5(cur_ref, prev_ref, next_ref, w_ref, ext_ref, has_prev, has_next):
    t = TOK_TILE
    ext_ref[0:HALO, :] = jnp.where(has_prev, prev_ref[0], 0.0)
    ext_ref[HALO:HALO + t, :] = cur_ref[0]
    ext_ref[HALO + t:2 * HALO + t, :] = jnp.where(has_next, next_ref[0], 0.0)
    acc = None
    for j in range(CONV_K):
        term = ext_ref[pl.ds(HALO - CONV_K // 2 + j, t), :] * w_ref[j:j + 1, :]
        acc = term if acc is None else acc + term
    return acc


def _silu(x):
    return x * jax.nn.sigmoid(x)


def _mixer_prep_kernel(xbc_ref, xbc_p, xbc_n, gq_ref, gq_p, gq_n, sm_ref, sw_ref, sb_ref, gw_ref,
                       bias_ref, scale_ref, ones_ref,
                       xs_out, cm_out, bt_out, q_out, k_out, kt_out, v_out, cols_out, rows_out, ext_ref):
    tile = pl.program_id(1)
    nt = pl.num_programs(1)
    has_prev = tile >= 2
    has_next = (tile >= 1) & (tile < nt - 1)

    u = _silu(_dwconv5(xbc_ref, xbc_p, xbc_n, sw_ref, ext_ref, has_prev, has_next) + sb_ref[...])
    xs_out[0] = u[:, :SSM_W]
    bt_out[0] = u[:, SSM_W:SSM_W + SSM_GROUPS * SSM_N].T.astype(jnp.bfloat16)
    cm_out[0] = u[:, SSM_W + SSM_GROUPS * SSM_N:]

    g = _silu(_dwconv5(gq_ref, gq_p, gq_n, gw_ref, ext_ref, has_prev, has_next))

    def l2n(t):
        return t * lax.rsqrt(_group_sum_sq(t, ones_ref) + EPS)

    q_out[0] = l2n(g[:, :GDN_W]) * (GDN_DK ** -0.5)
    k = l2n(g[:, GDN_W:2 * GDN_W])
    k_out[0] = k
    kt_out[0] = k.T.astype(jnp.bfloat16)
    v_out[0] = g[:, 2 * GDN_W:]

    sm = sm_ref[0] + bias_ref[...]
    sp = jnp.maximum(sm, 0.0) + jnp.log1p(jnp.exp(-jnp.abs(sm)))
    sg = jax.nn.sigmoid(sm)
    vals = sp * scale_ref[...]
    ri = lax.broadcasted_iota(jnp.int32, (TOK_TILE, TOK_TILE), 0)
    ci = lax.broadcasted_iota(jnp.int32, (TOK_TILE, TOK_TILE), 1)
    tri_pre = jnp.where(ri >= ci, 1.0, 0.0).astype(jnp.bfloat16)
    tri_suf = jnp.where(ri <= ci, 1.0, 0.0).astype(jnp.bfloat16)
    pre = None
    suf = None
    for term in _bf16_terms(vals, 3):
        a = jnp.dot(tri_pre, term, preferred_element_type=jnp.float32)
        b = jnp.dot(tri_suf, term, preferred_element_type=jnp.float32)
        pre = a if pre is None else pre + a
        suf = b if suf is None else suf + b
    lane = lax.broadcasted_iota(jnp.int32, (TOK_TILE, SMALL_W), 1)
    bwd_lane = ((lane >= L_SCUM + SSM_HEADS) & (lane < L_BETA)) | (lane >= L_GCUM + GDN_HEADS)
    cum = jnp.where(bwd_lane, suf, pre)
    cols = jnp.where(lane < L_SCUM, sp, jnp.where((lane >= L_BETA) & (lane < L_GCUM), sg, cum))
    cols_out[0] = cols
    rows_out[0] = cols.T


def _mixer_prep(xbc, gqkv, small, ssm_w, ssm_b, gdn_w, bias_vec, scale_vec):
    b, n, _ = xbc.shape
    nt = n // TOK_TILE
    hb = TOK_TILE // HALO
    last = n // HALO - 1
    ones_blk = _group_matrix(GDN_W, GDN_DK, 1.0)

    def cur(w):
        return pl.BlockSpec((1, TOK_TILE, w), lambda i, j: (i, j, 0))

    def prev(w):
        return pl.BlockSpec((1, HALO, w), lambda i, j: (i, jnp.maximum(j * hb - 1, 0), 0))

    def nxt(w):
        return pl.BlockSpec((1, HALO, w), lambda i, j: (i, jnp.minimum((j + 1) * hb, last), 0))

    def const(shape):
        return pl.BlockSpec(shape, lambda i, j: (0,) * len(shape))

    f32 = jnp.float32
    return pl.pallas_call(
        _mixer_prep_kernel,
        grid=(b, nt),
        in_specs=[cur(SSM_XBC), prev(SSM_XBC), nxt(SSM_XBC), cur(GDN_QKV), prev(GDN_QKV), nxt(GDN_QKV), cur(SMALL_W),
                  const((8, SSM_XBC)), const((1, SSM_XBC)), const((8, GDN_QKV)),
                  const((1, SMALL_W)), const((1, SMALL_W)), const((GDN_W, GDN_W))],
        out_specs=[cur(SSM_W), cur(SSM_GROUPS * SSM_N),
                   pl.BlockSpec((1, SSM_GROUPS * SSM_N, TOK_TILE), lambda i, j: (i, 0, j)),
                   cur(GDN_W), cur(GDN_W),
                   pl.BlockSpec((1, GDN_W, TOK_TILE), lambda i, j: (i, 0, j)),
                   cur(GDN_W), cur(SMALL_W),
                   pl.BlockSpec((1, SMALL_W, TOK_TILE), lambda i, j: (i, 0, j))],
        out_shape=[jax.ShapeDtypeStruct((b, n, SSM_W), f32),
                   jax.ShapeDtypeStruct((b, n, SSM_GROUPS * SSM_N), f32),
                   jax.ShapeDtypeStruct((b, SSM_GROUPS * SSM_N, n), jnp.bfloat16),
                   jax.ShapeDtypeStruct((b, n, GDN_W), f32),
                   jax.ShapeDtypeStruct((b, n, GDN_W), f32),
                   jax.ShapeDtypeStruct((b, GDN_W, n), jnp.bfloat16),
                   jax.ShapeDtypeStruct((b, n, GDN_W), f32),
                   jax.ShapeDtypeStruct((b, n, SMALL_W), f32),
                   jax.ShapeDtypeStruct((b, SMALL_W, n), f32)],
        scratch_shapes=[pltpu.VMEM((TOK_TILE + 2 * HALO, SSM_XBC), f32)],
        compiler_params=pltpu.CompilerParams(dimension_semantics=("parallel", "parallel")),
        name="mixer_prep",
    )(xbc, xbc, xbc, gqkv, gqkv, gqkv, small, ssm_w, ssm_b, gdn_w, bias_vec, scale_vec, ones_blk)


def _scan_tile(d, s, nt):
    return jnp.where(d == 0, s, jnp.where(s == 0, 0, nt - s))


def _pick(is_f, arr, base, stride, i, axis):
    a, b = base + i, base + stride + i
    if axis == 1:
        return jnp.where(is_f, arr[:, a:a + 1], arr[:, b:b + 1])
    return jnp.where(is_f, arr[a:a + 1, :], arr[b:b + 1, :])


def _order_masks(is_f):
    ri = lax.broadcasted_iota(jnp.int32, (TOK_TILE, TOK_TILE), 0)
    ci = lax.broadcasted_iota(jnp.int32, (TOK_TILE, TOK_TILE), 1)
    diff = (ri - ci) * jnp.where(is_f, 1, -1)
    return diff >= 0, diff > 0


def _ssd_kernel(xs_ref, cm_ref, bt_ref, cols_ref, rows_ref, y_ref, st_ref):
    is_f = pl.program_id(0) == 0

    @pl.when(pl.program_id(2) == 0)
    def _():
        st_ref[...] = jnp.zeros_like(st_ref)

    bf16 = jnp.bfloat16
    f32 = jnp.float32
    t = TOK_TILE
    cols = cols_ref[0]
    rows = rows_ref[0]
    incl, _ = _order_masks(is_f)
    lane128 = lax.broadcasted_iota(jnp.int32, (t, 128), 1)
    cm = cm_ref[0]
    bt = bt_ref[0]
    hpg = SSM_HEADS // SSM_GROUPS
    for g in range(SSM_GROUPS):
        cg = jnp.where((lane128 >= g * SSM_N) & (lane128 < (g + 1) * SSM_N), cm, 0.0).astype(bf16)
        cb = jnp.dot(cg, bt, preferred_element_type=f32)
        st_g = st_ref[:, g * hpg * SSM_P:(g + 1) * hpg * SSM_P]
        yoff = jnp.dot(cg, st_g.astype(bf16), preferred_element_type=f32)
        btg = bt[g * SSM_N:(g + 1) * SSM_N, :].astype(f32)
        for pair in range(hpg // 2):
            xp = xs_ref[0, :, (g * hpg + 2 * pair) * SSM_P:(g * hpg + 2 * pair + 2) * SSM_P]
            acc_y = None
            acc_s = None
            e_col = None
            dec = None
            for half in range(2):
                hd = g * hpg + 2 * pair + half
                cum_col = _pick(is_f, cols, L_SCUM, SSM_HEADS, hd, 1)
                cum_row = _pick(is_f, rows, L_SCUM, SSM_HEADS, hd, 0)
                dt_col = _pick(is_f, cols, L_DT, SSM_HEADS, hd, 1)
                cum_tot = jnp.where(is_f, cum_row[:, t - 1:t], cum_row[:, 0:1])
                lm = jnp.exp(jnp.where(incl, cum_col - cum_row, -jnp.inf))
                m = (cb * lm).astype(bf16)
                in_half = (lane128 >= half * SSM_P) & (lane128 < (half + 1) * SSM_P)
                xm = jnp.where(in_half, xp * dt_col, 0.0).astype(bf16)
                ty = jnp.dot(m, xm, preferred_element_type=f32)
                ts = jnp.dot((btg * jnp.exp(cum_tot - cum_row)).astype(bf16), xm, preferred_element_type=f32)
                acc_y = ty if acc_y is None else acc_y + ty
                acc_s = ts if acc_s is None else acc_s + ts
                ec = jnp.exp(cum_col)
                dc = jnp.exp(cum_tot)
                e_col = ec if e_col is None else jnp.where(in_half, ec, e_col)
                dec = dc if dec is None else jnp.where(in_half[0:1, :], dc, dec)
            lo = pair * 2 * SSM_P
            c0 = g * hpg * SSM_P + lo
            y_ref[0, 0, :, c0:c0 + 2 * SSM_P] = acc_y + yoff[:, lo:lo + 2 * SSM_P] * e_col
            r0 = g * SSM_N
            st_ref[r0:r0 + SSM_N, c0:c0 + 2 * SSM_P] = st_ref[r0:r0 + SSM_N, c0:c0 + 2 * SSM_P] * dec + acc_s


def _ssd_scan(xs, cm, bt, cols, rows):
    b, n, _ = xs.shape
    nt = n // TOK_TILE

    def tok(w):
        return pl.BlockSpec((1, TOK_TILE, w), lambda d, i, s: (i, _scan_tile(d, s, nt), 0))

    def tr(w):
        return pl.BlockSpec((1, w, TOK_TILE), lambda d, i, s: (i, 0, _scan_tile(d, s, nt)))

    return pl.pallas_call(
        _ssd_kernel,
        grid=(2, b, nt),
        in_specs=[tok(SSM_W), tok(SSM_GROUPS * SSM_N), tr(SSM_GROUPS * SSM_N), tok(SMALL_W), tr(SMALL_W)],
        out_specs=pl.BlockSpec((1, 1, TOK_TILE, SSM_W), lambda d, i, s: (d, i, _scan_tile(d, s, nt), 0)),
        out_shape=jax.ShapeDtypeStruct((2, b, n, SSM_W), jnp.float32),
        scratch_shapes=[pltpu.VMEM((SSM_GROUPS * SSM_N, SSM_W), jnp.float32)],
        compiler_params=pltpu.CompilerParams(dimension_semantics=("parallel", "parallel", "arbitrary")),
        name="ssd_scan",
    )(xs, cm, bt, cols, rows)


def _gdn_kernel(q_ref, k_ref, kt_ref, v_ref, cols_ref, rows_ref, o_ref, s_ref):
    is_f = pl.program_id(0) == 0

    @pl.when(pl.program_id(2) == 0)
    def _():
        s_ref[...] = jnp.zeros_like(s_ref)

    bf16 = jnp.bfloat16
    f32 = jnp.float32
    t = TOK_TILE
    cols = cols_ref[0]
    rows = rows_ref[0]
    incl, strict = _order_masks(is_f)
    lane = lax.broadcasted_iota(jnp.int32, (t, GDN_W), 1)
    sub = lax.broadcasted_iota(jnp.int32, (GDN_W, t), 0)
    head_of_lane = [(lane >= h * GDN_DK) & (lane < (h + 1) * GDN_DK) for h in range(GDN_HEADS)]
    head_of_sub = [(sub >= h * GDN_DK) & (sub < (h + 1) * GDN_DK) for h in range(GDN_HEADS)]

    def by_lane(pieces):
        out = pieces[0]
        for h in range(1, GDN_HEADS):
            out = jnp.where(head_of_lane[h] if pieces[h].shape[0] != 1 else head_of_lane[h][0:1, :], pieces[h], out)
        return out

    q = q_ref[0]
    k = k_ref[0]
    v = v_ref[0]
    kt = kt_ref[0]
    gc_col = [_pick(is_f, cols, L_GCUM, GDN_HEADS, h, 1) for h in range(GDN_HEADS)]
    gc_row = [_pick(is_f, rows, L_GCUM, GDN_HEADS, h, 0) for h in range(GDN_HEADS)]
    gc_end = [jnp.where(is_f, r[:, t - 1:t], r[:, 0:1]) for r in gc_row]
    beta_x = by_lane([_pick(is_f, cols, L_BETA, GDN_HEADS, h, 1) + jnp.zeros((t, GDN_W), f32) for h in range(GDN_HEADS)])
    egc = jnp.exp(by_lane([c + jnp.zeros((t, GDN_W), f32) for c in gc_col]))
    kb = k * beta_x
    vb = v * beta_x
    kbg = kb * egc
    qd = q * egc
    ri = lax.broadcasted_iota(jnp.int32, (t, t), 0)
    ci = lax.broadcasted_iota(jnp.int32, (t, t), 1)
    eye = jnp.where(ri == ci, 1.0, 0.0)
    off_levels = []
    size = 1
    while size < t:
        off_levels.append(((ri // (2 * size)) == (ci // (2 * size))) & ((ri // size) != (ci // size)))
        size *= 2

    a_mats = []
    qkd = []
    for h in range(GDN_HEADS):
        kbm = jnp.where(head_of_lane[h], kb, 0.0).astype(bf16)
        qm = jnp.where(head_of_lane[h], q, 0.0).astype(bf16)
        kk = jnp.dot(kbm, kt, preferred_element_type=f32)
        qk = jnp.dot(qm, kt, preferred_element_type=f32)
        dec = jnp.exp(jnp.where(incl, gc_col[h] - gc_row[h], -jnp.inf))
        a_mats.append(jnp.where(strict, kk * dec, 0.0))
        qkd.append((qk * dec).astype(bf16))
    t_inv = [eye - jnp.where(off_levels[0], a, 0.0) for a in a_mats]
    for lvl_mask in off_levels[1:]:
        tbs = [x.astype(bf16) for x in t_inv]
        es = [jnp.where(lvl_mask, a, 0.0).astype(bf16) for a in a_mats]
        tes = [jnp.dot(tb, e, preferred_element_type=f32).astype(bf16) for tb, e in zip(tbs, es)]
        t_inv = [x - jnp.dot(te, tb, preferred_element_type=f32) for x, te, tb in zip(t_inv, tes, tbs)]
    uw = None
    for h in range(GDN_HEADS):
        rhs = jnp.concatenate([jnp.where(head_of_lane[h], vb, 0.0), jnp.where(head_of_lane[h], kbg, 0.0)], axis=1)
        term = jnp.dot(t_inv[h].astype(bf16), rhs.astype(bf16), preferred_element_type=f32)
        uw = term if uw is None else uw + term

    s_old = s_ref[...]
    sb = s_old.astype(bf16)
    u = uw[:, :GDN_W]
    w = uw[:, GDN_W:]
    v_new = u - jnp.dot(w.astype(bf16), sb, preferred_element_type=f32)
    o = jnp.dot(qd.astype(bf16), sb, preferred_element_type=f32)
    for h in range(GDN_HEADS):
        o = o + jnp.dot(qkd[h], jnp.where(head_of_lane[h], v_new, 0.0).astype(bf16), preferred_element_type=f32)
    o_ref[0, 0] = o

    mult = jnp.exp(gc_end[0] - gc_row[0]) + jnp.zeros((GDN_W, t), f32)
    for h in range(1, GDN_HEADS):
        mult = jnp.where(head_of_sub[h], jnp.exp(gc_end[h] - gc_row[h]), mult)
    kend_t = (kt.astype(f32) * mult).astype(bf16)
    upd = jnp.dot(kend_t, v_new.astype(bf16), preferred_element_type=f32)
    g_end = jnp.exp(by_lane(gc_end))
    same_head = head_of_sub[0] & head_of_lane[0]
    for h in range(1, GDN_HEADS):
        same_head = same_head | (head_of_sub[h] & head_of_lane[h])
    s_ref[...] = jnp.where(same_head, s_old * g_end + upd, 0.0)


def _gdn_scan(q, k, kt, v, cols, rows):
    b, n, _ = q.shape
    nt = n // TOK_TILE

    def tok(w):
        return pl.BlockSpec((1, TOK_TILE, w), lambda d, i, s: (i, _scan_tile(d, s, nt), 0))

    def tr(w):
        return pl.BlockSpec((1, w, TOK_TILE), lambda d, i, s: (i, 0, _scan_tile(d, s, nt)))

    return pl.pallas_call(
        _gdn_kernel,
        grid=(2, b, nt),
        in_specs=[tok(GDN_W), tok(GDN_W), tr(GDN_W), tok(GDN_W), tok(SMALL_W), tr(SMALL_W)],
        out_specs=pl.BlockSpec((1, 1, TOK_TILE, GDN_W), lambda d, i, s: (d, i, _scan_tile(d, s, nt), 0)),
        out_shape=jax.ShapeDtypeStruct((2, b, n, GDN_W), jnp.float32),
        scratch_shapes=[pltpu.VMEM((GDN_W, GDN_W), jnp.float32)],
        compiler_params=pltpu.CompilerParams(dimension_semantics=("parallel", "parallel", "arbitrary")),
        name="gdn_scan",
    )(q, k, kt, v, cols, rows)


def _small_lane_params(ssm_dt_bias, ssm_a_log, gdn_dt_bias, gdn_a_log):
    f32 = jnp.float32
    z = lambda k: jnp.zeros((k,), f32)
    sb = ssm_dt_bias.astype(f32).reshape(-1)
    bias = jnp.concatenate([sb, sb, z(L_GCUM - L_BETA), gdn_dt_bias.astype(f32).reshape(-1),
                            z(SMALL_W - L_GCUM - 2 * GDN_HEADS)])
    scale = jnp.concatenate([z(L_SCUM), -jnp.exp(ssm_a_log.astype(f32)).reshape(-1), z(L_GCUM - L_BETA),
                             -jnp.exp(gdn_a_log.astype(f32)).reshape(-1), z(SMALL_W - L_GCUM - 2 * GDN_HEADS)])
    return bias[None, :], scale[None, :]


def _permuted_w_in(w_in):
    offs = np.cumsum((0,) + SPLIT_SIZES)
    dt0, beta0, a0 = int(offs[3]), int(offs[6]), int(offs[7])
    dt = list(range(dt0, dt0 + SSM_HEADS))
    small = dt + dt + dt + dt + list(range(beta0, beta0 + 2 * GDN_HEADS)) + list(range(a0, a0 + 2 * GDN_HEADS))
    cols = list(range(0, dt0)) + list(range(int(offs[4]), beta0)) + small
    w = w_in[:, np.asarray(cols)]
    return jnp.pad(w, ((0, 0), (0, SMALL_W - len(small))))


def mixers_pallas(xbc, gqkv, small, ssm_conv_w, ssm_conv_b, ssm_dt_bias, ssm_a_log, gdn_conv_w, gdn_dt_bias, gdn_a_log):
    bias_vec, scale_vec = _small_lane_params(ssm_dt_bias, ssm_a_log, gdn_dt_bias, gdn_a_log)
    pad = ((0, 8 - CONV_K), (0, 0))
    xs, cm, bt, q, k, kt, v, cols, rows = _mixer_prep(
        xbc, gqkv, small, jnp.pad(ssm_conv_w, pad), ssm_conv_b[None, :], jnp.pad(gdn_conv_w, pad), bias_vec, scale_vec)
    y = _ssd_scan(xs, cm, bt, cols, rows)
    o = _gdn_scan(q, k, kt, v, cols, rows)
    return xs, y, o


def _expert_kernel(be_ref, nu_ref, x_ref, wg_ref, wu_ref, wd_ref, o_ref, wgb_ref, wub_ref, wdb_ref):
    i = pl.program_id(0)

    @pl.when((i == 0) | (be_ref[i] != be_ref[jnp.maximum(i - 1, 0)]))
    def _():
        wgb_ref[...] = wg_ref[0].astype(jnp.bfloat16)
        wub_ref[...] = wu_ref[0].astype(jnp.bfloat16)
        wdb_ref[...] = wd_ref[0].astype(jnp.bfloat16)

    @pl.when(i < nu_ref[0])
    def _():
        x = x_ref[...]
        g = jnp.dot(x, wgb_ref[...], preferred_element_type=jnp.float32)
        u = jnp.dot(x, wub_ref[...], preferred_element_type=jnp.float32)
        h = (_silu(g) * u).astype(jnp.bfloat16)
        o_ref[...] = jnp.dot(h, wdb_ref[...], preferred_element_type=jnp.float32)

    @pl.when(i >= nu_ref[0])
    def _():
        o_ref[...] = jnp.zeros_like(o_ref)


def _expert_blocks(block_e, n_used, buf, w_gate, w_up, w_down):
    n_rows, d = buf.shape
    n_blocks = n_rows // MOE_BLOCK
    grid_spec = pltpu.PrefetchScalarGridSpec(
        num_scalar_prefetch=2,
        grid=(n_blocks,),
        in_specs=[pl.BlockSpec((MOE_BLOCK, d), lambda i, be, nu: (i, 0)),
                  pl.BlockSpec((1, d, D_EXPERT), lambda i, be, nu: (be[i], 0, 0)),
                  pl.BlockSpec((1, d, D_EXPERT), lambda i, be, nu: (be[i], 0, 0)),
                  pl.BlockSpec((1, D_EXPERT, d), lambda i, be, nu: (be[i], 0, 0))],
        out_specs=pl.BlockSpec((MOE_BLOCK, d), lambda i, be, nu: (i, 0)),
        scratch_shapes=[pltpu.VMEM((d, D_EXPERT), jnp.bfloat16), pltpu.VMEM((d, D_EXPERT), jnp.bfloat16),
                        pltpu.VMEM((D_EXPERT, d), jnp.bfloat16)],
    )
    return pl.pallas_call(
        _expert_kernel,
        grid_spec=grid_spec,
        out_shape=jax.ShapeDtypeStruct((n_rows, d), jnp.float32),
        compiler_params=pltpu.CompilerParams(dimension_semantics=("arbitrary",), vmem_limit_bytes=MOE_VMEM_BYTES),
        name="moe_experts",
    )(block_e, n_used, buf, w_gate, w_up, w_down)


def routed_experts(t, experts, gates, w_gate, w_up, w_down):
    n, d = t.shape
    n_assign = n * TOP_K
    flat_e = experts.reshape(n_assign)
    order = jnp.argsort(flat_e).astype(jnp.int32)
    slot = jnp.argsort(order).astype(jnp.int32)
    counts = jax.ops.segment_sum(jnp.ones((n_assign,), jnp.int32), flat_e, num_segments=N_EXPERTS)
    starts = jnp.cumsum(counts) - counts
    padded = (counts + MOE_BLOCK - 1) // MOE_BLOCK * MOE_BLOCK
    pad_ends = jnp.cumsum(padded)
    pad_starts = pad_ends - padded
    n_blocks = -(-n_assign // MOE_BLOCK) + N_EXPERTS
    block_start = jnp.arange(n_blocks, dtype=jnp.int32) * MOE_BLOCK
    block_e = jnp.minimum(jnp.searchsorted(pad_ends, block_start, side='right'), N_EXPERTS - 1).astype(jnp.int32)
    n_used = (pad_ends[-1] // MOE_BLOCK).astype(jnp.int32).reshape(1)
    row = jnp.arange(n_blocks * MOE_BLOCK, dtype=jnp.int32)
    row_e = jnp.repeat(block_e, MOE_BLOCK)
    within = row - pad_starts[row_e]
    src_slot = jnp.minimum(starts[row_e] + within, n_assign - 1)
    src_tok = jnp.where(within < counts[row_e], order[src_slot] // TOP_K, 0)
    buf = t[src_tok]
    out = _expert_blocks(block_e, n_used, buf, w_gate, w_up, w_down)
    pos = pad_starts[flat_e] + slot - starts[flat_e]
    y_assign = out[pos]
    return jnp.sum(y_assign.reshape(n, TOP_K, d) * gates[..., None].astype(out.dtype), axis=1)


def hier_moe(t, t_rows, router_g_w, router_g_b, router_e_w, router_e_b, w_gate, w_up, w_down):
    n = t.shape[0]
    grp_prob = jax.nn.softmax(jnp.dot(t, router_g_w, preferred_element_type=jnp.float32)
                              + router_g_b.astype(jnp.float32), axis=-1)
    p_grp, grp = lax.top_k(grp_prob, 1)
    e_logits = (jnp.dot(t, router_e_w, preferred_element_type=jnp.float32)
                + router_e_b.astype(jnp.float32)).reshape(n, N_EGROUPS, EXPERTS_PER_GROUP)
    sel = e_logits[jnp.arange(n), grp[:, 0]]
    p_top, idx = lax.top_k(jax.nn.softmax(sel, axis=-1), TOP_K)
    gates = p_grp * p_top / jnp.sum(p_top, axis=-1, keepdims=True)
    experts = grp * EXPERTS_PER_GROUP + idx
    return routed_experts(t_rows, experts, gates, w_gate, w_up, w_down)


def _matmul_kernel(a_ref, b_ref, o_ref):
    o_ref[...] = jnp.dot(a_ref[...].astype(jnp.bfloat16), b_ref[...],
                         preferred_element_type=jnp.float32)


def _matmul(a, b, tm, tn):
    m, k = a.shape
    n = b.shape[1]
    return pl.pallas_call(
        _matmul_kernel,
        grid=(m // tm, n // tn),
        in_specs=[pl.BlockSpec((tm, k), lambda i, j: (i, 0)),
                  pl.BlockSpec((k, tn), lambda i, j: (0, j))],
        out_specs=pl.BlockSpec((tm, tn), lambda i, j: (i, j)),
        out_shape=jax.ShapeDtypeStruct((m, n), jnp.float32),
        compiler_params=pltpu.CompilerParams(dimension_semantics=("parallel", "parallel")),
        name="dense",
    )(a, b)


def _dense(a, w, tm=512):
    lead = a.shape[:-1]
    k = a.shape[-1]
    n = w.shape[1]
    a2 = a.reshape(-1, k)
    m = a2.shape[0]
    m_pad = -(-m // 8) * 8
    if m_pad != m:
        a2 = jnp.pad(a2, ((0, m_pad - m), (0, 0)))
    tm = min(tm, m_pad)
    n_pad = -(-n // 128) * 128
    tn = next(c for c in (1024, 768, 640, 512, 384, 256, 128) if n_pad % c == 0)
    wb = jnp.pad(w, ((0, 0), (0, n_pad - n))).astype(jnp.bfloat16)
    out = _matmul(a2, wb, tm, tn)[:m, :n]
    return out.reshape(lead + (n,))


IN_SLABS = (3 * DIFF_W, SSM_W, SSM_XBC, GDN_QKV, GDN_W, SMALL_W)
PROJ_VMEM_BYTES = 44 * 1024 * 1024


def _mod_row_spec(bsz):
    return pl.BlockSpec((1, MOD_CHUNKS, D_MODEL), lambda i, j: (jnp.where(j == 0, bsz, i), 0, 0))


def _in_proj_kernel(x_ref, nw_ref, mod_ref, w_ref, *out_refs):
    x = x_ref[0]
    xn = x * lax.rsqrt(jnp.mean(x * x, axis=-1, keepdims=True) + EPS) * nw_ref[...]
    mod = mod_ref[0]
    xm = (xn * (1.0 + mod[1:2, :]) + mod[0:1, :]).astype(jnp.bfloat16)
    c0 = 0
    for o_ref, width in zip(out_refs, IN_SLABS):
        o_ref[0] = jnp.dot(xm, w_ref[:, c0:c0 + width], preferred_element_type=jnp.float32)
        c0 += width


def _in_proj(x, norm_w, mod, w):
    b, n, d = x.shape
    tok = lambda width: pl.BlockSpec((1, TOK_TILE, width), lambda i, j: (i, j, 0))
    return pl.pallas_call(
        _in_proj_kernel,
        grid=(b, n // TOK_TILE),
        in_specs=[tok(d), pl.BlockSpec((1, d), lambda i, j: (0, 0)), _mod_row_spec(b),
                  pl.BlockSpec(w.shape, lambda i, j: (0, 0))],
        out_specs=[tok(width) for width in IN_SLABS],
        out_shape=[jax.ShapeDtypeStruct((b, n, width), jnp.float32) for width in IN_SLABS],
        compiler_params=pltpu.CompilerParams(dimension_semantics=("parallel", "parallel"),
                                             vmem_limit_bytes=PROJ_VMEM_BYTES),
        name="in_proj",
    )(x, norm_w, mod, w)


def _out_proj_kernel(att_ref, y0_ref, y1_ref, xs_ref, z_ref, o0_ref, o1_ref, gate_ref, x_ref, mod_ref,
                     dvec_ref, snw_ref, gnw_ref, n2w_ref, g_ref, w_ref, xo_ref, f_ref, fb_ref):
    ys = (y0_ref[0, 0] + y1_ref[0, 0] + dvec_ref[...] * xs_ref[0]) * _silu(z_ref[0])
    gw = SSM_W // SSM_GROUPS
    parts = [att_ref[0]]
    for g in range(SSM_GROUPS):
        yg = ys[:, g * gw:(g + 1) * gw]
        parts.append(yg * lax.rsqrt(jnp.mean(yg * yg, axis=-1, keepdims=True) + EPS) * snw_ref[:, g * gw:(g + 1) * gw])
    o = o0_ref[0, 0] + o1_ref[0, 0]
    parts.append(o * lax.rsqrt(_group_sum_sq(o, g_ref) + EPS) * gnw_ref[...] * _silu(gate_ref[0]))
    ml = jnp.concatenate(parts, axis=1).astype(jnp.bfloat16)
    mod = mod_ref[0]
    xn = x_ref[0] + mod[2:3, :] * jnp.dot(ml, w_ref[...], preferred_element_type=jnp.float32)
    xo_ref[0] = xn
    f = xn * lax.rsqrt(jnp.mean(xn * xn, axis=-1, keepdims=True) + EPS) * n2w_ref[...]
    f = f * (1.0 + mod[4:5, :]) + mod[3:4, :]
    f_ref[0] = f
    fb_ref[0] = f.astype(jnp.bfloat16)


def _out_proj(att, y, xs, z, o, gate, x, mod, dvec, snw, gnw, n2w, g64, w):
    b, n, d = x.shape
    tok = lambda width: pl.BlockSpec((1, TOK_TILE, width), lambda i, j: (i, j, 0))
    dirs = lambda width, k: pl.BlockSpec((1, 1, TOK_TILE, width), lambda i, j: (k, i, j, 0))
    vec = lambda width: pl.BlockSpec((1, width), lambda i, j: (0, 0))
    return pl.pallas_call(
        _out_proj_kernel,
        grid=(b, n // TOK_TILE),
        in_specs=[tok(DIFF_W), dirs(SSM_W, 0), dirs(SSM_W, 1), tok(SSM_W), tok(SSM_W), dirs(GDN_W, 0), dirs(GDN_W, 1),
                  tok(GDN_W), tok(d), _mod_row_spec(b), vec(SSM_W), vec(SSM_W), vec(GDN_W), vec(d),
                  pl.BlockSpec(g64.shape, lambda i, j: (0, 0)), pl.BlockSpec(w.shape, lambda i, j: (0, 0))],
        out_specs=[tok(d), tok(d), tok(d)],
        out_shape=[jax.ShapeDtypeStruct((b, n, d), jnp.float32), jax.ShapeDtypeStruct((b, n, d), jnp.float32),
                   jax.ShapeDtypeStruct((b, n, d), jnp.bfloat16)],
        compiler_params=pltpu.CompilerParams(dimension_semantics=("parallel", "parallel"),
                                             vmem_limit_bytes=PROJ_VMEM_BYTES),
        name="out_proj",
    )(att, y, y, xs, z, o, o, gate, x, mod, dvec, snw, gnw, n2w, g64, w)


def kernel(x, c, ctx, c_ctx, w_mod, b_mod, norm1_w, norm2_w, w_in, w_out,
           diff_qn_w, diff_kn_w, diff_lq1, diff_lk1, diff_lq2, diff_lk2, diff_norm_w,
           ssm_conv_w, ssm_conv_b, ssm_dt_bias, ssm_a_log, ssm_d, ssm_norm_w,
           gdn_conv_w, gdn_dt_bias, gdn_a_log, gdn_norm_w,
           router_g_w, router_g_b, router_e_w, router_e_b, exp_w_gate, exp_w_up, exp_w_down):
    assert ctx.shape[1] == CTX_LEN == TOK_TILE and x.shape[1] % TOK_TILE == 0
    bsz = x.shape[0]
    xs_all = jnp.concatenate([ctx, x], axis=1)
    n_tok = xs_all.shape[1]
    is_lat = (jnp.arange(n_tok) >= CTX_LEN)[None, :, None]
    cc = jnp.concatenate([c, c_ctx[None, :]], axis=0)
    bf16 = jnp.bfloat16
    g64 = _group_matrix(GDN_W, GDN_DV, 1.0 / GDN_DV)
    for l in range(DEPTH):
        last = l == DEPTH - 1
        lam_init = 0.8 - 0.6 * math.exp(-0.3 * l)
        mod = (_dense(jax.nn.silu(cc), w_mod[l]) + b_mod[l]).reshape(bsz + 1, MOD_CHUNKS, D_MODEL)
        qkv, z, xbc, gqkv, gate, small = _in_proj(xs_all, norm1_w[l][None, :], mod, _permuted_w_in(w_in[l]).astype(bf16))
        att = diff_attention_pallas(qkv, lam_init, diff_qn_w[l], diff_kn_w[l],
                                    diff_lq1[l], diff_lk1[l], diff_lq2[l], diff_lk2[l], diff_norm_w[l])
        xs, y, o = mixers_pallas(xbc, gqkv, small, ssm_conv_w[l], ssm_conv_b[l],
                                 ssm_dt_bias[l], ssm_a_log[l], gdn_conv_w[l], gdn_dt_bias[l], gdn_a_log[l])
        xs_all, f, fb = _out_proj(att, y, xs, z, o, gate, xs_all, mod, jnp.repeat(ssm_d[l], SSM_P)[None, :],
                                  ssm_norm_w[l][None, :], jnp.tile(gdn_norm_w[l], GDN_HEADS)[None, :],
                                  norm2_w[l][None, :], g64, w_out[l].astype(bf16))
        moe_w = (router_g_w[l], router_g_b[l], router_e_w[l], router_e_b[l],
                 exp_w_gate[l], exp_w_up[l], exp_w_down[l])
        if last:
            xl = xs_all[:, CTX_LEN:]
            yl = hier_moe(f[:, CTX_LEN:].reshape(-1, D_MODEL), fb[:, CTX_LEN:].reshape(-1, D_MODEL), *moe_w)
            return xl + mod[:bsz, None, 5, :] * yl.reshape(xl.shape)
        ym = hier_moe(f.reshape(-1, D_MODEL), fb.reshape(-1, D_MODEL), *moe_w)
        mod5 = jnp.where(is_lat, mod[:bsz, None, 5, :], mod[bsz, 5, :])
        xs_all = xs_all + mod5 * ym.reshape(xs_all.shape)
```

```python
import functools
import math
import jax
import jax.numpy as jnp
from jax import lax
import numpy as np
from jax.experimental import pallas as pl
from jax.experimental.pallas import tpu as pltpu

D_MODEL = 1024
DEPTH = 2
CTX_LEN = 256
GRID_W = 64
EPS = 1e-6
MOD_CHUNKS = 6

DIFF_HEADS = 4
DIFF_QK = 32
DIFF_V = 2 * DIFF_QK
DIFF_W = DIFF_HEADS * DIFF_V
ROPE_BASE = 10000.0

SSM_HEADS = 8
SSM_P = 64
SSM_GROUPS = 2
SSM_N = 64
SSM_W = SSM_HEADS * SSM_P
SSM_XBC = SSM_W + 2 * SSM_GROUPS * SSM_N
CONV_K = 5

GDN_HEADS = 4
GDN_DK = 64
GDN_DV = 64
GDN_QKV = GDN_HEADS * (2 * GDN_DK + GDN_DV)
GDN_W = GDN_HEADS * GDN_DV

D_MIX = DIFF_W + SSM_W + GDN_W
SPLIT_SIZES = (3 * DIFF_W, SSM_W, SSM_XBC, SSM_HEADS, GDN_QKV, GDN_W, 2 * GDN_HEADS, 2 * GDN_HEADS)

N_EGROUPS = 4
EXPERTS_PER_GROUP = 8
N_EXPERTS = N_EGROUPS * EXPERTS_PER_GROUP
TOP_K = 2
D_EXPERT = 512
MOE_BLOCK = 256

TOK_TILE = 256
LOG2E = 1.4426950408889634
EXP_ROWS = 64
ATTN_VMEM_BYTES = 52 * 1024 * 1024
MOE_VMEM_BYTES = 40 * 1024 * 1024
HALO = 8
SMALL_W = 128
L_DT = 0
L_SCUM = 16
L_BETA = 32
L_GCUM = 40


def rmsnorm(x, w):
    x32 = x.astype(jnp.float32)
    y = x32 * lax.rsqrt(jnp.mean(x32 * x32, axis=-1, keepdims=True) + EPS)
    return y.astype(x.dtype) * w


def modulate(x, shift, scale):
    return x * (1.0 + scale) + shift


def _bf16_terms(x, n):
    out = []
    for _ in range(n):
        t = x.astype(jnp.bfloat16)
        out.append(t)
        x = x - t.astype(jnp.float32)
    return out


def _group_sum_sq(x, g_ref):
    hi, lo = _bf16_terms(x * x, 2)
    g = g_ref[...]
    return (jnp.dot(hi, g, preferred_element_type=jnp.float32)
            + jnp.dot(lo, g, preferred_element_type=jnp.float32))


def _group_matrix(width, group, value):
    idx = np.arange(width) // group
    return jnp.asarray((idx[:, None] == idx[None, :]).astype(np.float32) * value, jnp.bfloat16)


def _attn_prep_kernel(qkv_ref, cos_ref, sin_ref, wq_ref, wk_ref, g_ref, qt_out, k_out, vt_out):
    x = qkv_ref[0]
    cos = cos_ref[...]
    sin = sin_ref[...]
    lane = lax.broadcasted_iota(jnp.int32, cos.shape, 1)
    lo_half = (lane % 16) < 8

    def norm_rope(t, w):
        y = t * lax.rsqrt(_group_sum_sq(t, g_ref) + EPS) * w
        rot = jnp.where(lo_half, pltpu.roll(y, DIFF_W - 8, 1), pltpu.roll(y, 8, 1))
        return y * cos + rot * sin

    q = norm_rope(x[:, 0:DIFF_W], wq_ref[...]) * (DIFF_QK ** -0.5 * LOG2E)
    qt_out[0] = q.T
    k_out[0] = norm_rope(x[:, DIFF_W:2 * DIFF_W], wk_ref[...]).astype(jnp.bfloat16)
    vt_out[0] = x[:, 2 * DIFF_W:3 * DIFF_W].T.astype(jnp.bfloat16)


def _attn_prep(qkv, cos_t, sin_t, wq, wk, gmat):
    b, n, _ = qkv.shape
    nt = n // TOK_TILE
    tok = pl.BlockSpec((1, TOK_TILE, DIFF_W), lambda i, j: (i, j, 0))
    tr = pl.BlockSpec((1, DIFF_W, TOK_TILE), lambda i, j: (i, 0, j))
    return pl.pallas_call(
        _attn_prep_kernel,
        grid=(b, nt),
        in_specs=[pl.BlockSpec((1, TOK_TILE, 3 * DIFF_W), lambda i, j: (i, j, 0)),
                  pl.BlockSpec((TOK_TILE, DIFF_W), lambda i, j: (j, 0)),
                  pl.BlockSpec((TOK_TILE, DIFF_W), lambda i, j: (j, 0)),
                  pl.BlockSpec((1, DIFF_W), lambda i, j: (0, 0)),
                  pl.BlockSpec((1, DIFF_W), lambda i, j: (0, 0)),
                  pl.BlockSpec((DIFF_W, DIFF_W), lambda i, j: (0, 0))],
        out_specs=[tr, tok, tr],
        out_shape=[jax.ShapeDtypeStruct((b, DIFF_W, n), jnp.float32),
                   jax.ShapeDtypeStruct((b, n, DIFF_W), jnp.bfloat16),
                   jax.ShapeDtypeStruct((b, DIFF_W, n), jnp.bfloat16)],
        compiler_params=pltpu.CompilerParams(dimension_semantics=("parallel", "parallel")),
        name="attn_prep",
    )(qkv, cos_t, sin_t, wq, wk, gmat)


def _attn_kernel(lam_ref, qt_ref, k_ref, vt_ref, wo_ref, g_ref, o_ref, s_ref, p_ref, *, n_keys, out_scale):
    tile = pl.program_id(1)
    lam = lam_ref[0]
    sub = lax.broadcasted_iota(jnp.int32, (DIFF_W, TOK_TILE), 0)

    def attend(nk):
        n_chunks = nk // TOK_TILE
        qt = qt_ref[0]

        def masked_qt(head):
            return jnp.concatenate(
                [jnp.where((sub >= (2 * head + mp) * DIFF_QK) & (sub < (2 * head + mp + 1) * DIFF_QK), qt, 0.0)
                 for mp in range(2)], axis=1).astype(jnp.bfloat16)

        def score_chunk(qt2, buf, c, mx8):
            s = jnp.dot(k_ref[0, c * TOK_TILE:(c + 1) * TOK_TILE, :], qt2, preferred_element_type=jnp.float32)
            s_ref[buf, c * TOK_TILE:(c + 1) * TOK_TILE, :] = s
            cm = jnp.max(s.reshape(TOK_TILE // 8, 8, 2 * TOK_TILE), axis=0)
            return cm if mx8 is None else jnp.maximum(mx8, cm)

        mx8 = None
        qt2 = masked_qt(0)
        for c in range(n_chunks):
            mx8 = score_chunk(qt2, 0, c, mx8)
        acc = None
        for head in range(DIFF_HEADS):
            buf = head % 2
            mx = jnp.max(mx8, axis=0, keepdims=True)
            nxt = head + 1 < DIFF_HEADS
            if nxt:
                qt2 = masked_qt(head + 1)
            mx8 = None
            ls8 = None
            for c in range(n_chunks):
                if nxt:
                    mx8 = score_chunk(qt2, 1 - buf, c, mx8)
                for r in range(c * TOK_TILE // EXP_ROWS, (c + 1) * TOK_TILE // EXP_ROWS):
                    e = jnp.exp2(s_ref[buf, r * EXP_ROWS:(r + 1) * EXP_ROWS, :] - mx)
                    p_ref[r * EXP_ROWS:(r + 1) * EXP_ROWS, :] = e.astype(jnp.bfloat16)
                    es = jnp.sum(e.reshape(EXP_ROWS // 8, 8, 2 * TOK_TILE), axis=0)
                    ls8 = es if ls8 is None else ls8 + es
            ls = jnp.sum(ls8, axis=0, keepdims=True)
            ot = jnp.dot(vt_ref[0, :, :nk], p_ref[:nk, :], preferred_element_type=jnp.float32) / ls
            in_head = (sub >= head * DIFF_V) & (sub < (head + 1) * DIFF_V)
            part = jnp.where(in_head, ot[:, :TOK_TILE] - lam * ot[:, TOK_TILE:], 0.0)
            acc = part if acc is None else acc + part
        o = acc.T
        y = o * lax.rsqrt(_group_sum_sq(o, g_ref) + EPS)
        o_ref[0] = y * wo_ref[...] * out_scale

    @pl.when(tile == 0)
    def _():
        attend(TOK_TILE)

    @pl.when(tile > 0)
    def _():
        attend(n_keys)


def _attention(lam, qt, k, vt, wo, gmat, out_scale):
    b, n, _ = k.shape
    nt = n // TOK_TILE
    kern = functools.partial(_attn_kernel, n_keys=n, out_scale=out_scale)
    return pl.pallas_call(
        kern,
        grid=(b, nt),
        in_specs=[pl.BlockSpec(memory_space=pltpu.SMEM),
                  pl.BlockSpec((1, DIFF_W, TOK_TILE), lambda i, j: (i, 0, j)),
                  pl.BlockSpec((1, n, DIFF_W), lambda i, j: (i, 0, 0)),
                  pl.BlockSpec((1, DIFF_W, n), lambda i, j: (i, 0, 0)),
                  pl.BlockSpec((1, DIFF_W), lambda i, j: (0, 0)),
                  pl.BlockSpec((DIFF_W, DIFF_W), lambda i, j: (0, 0))],
        out_specs=pl.BlockSpec((1, TOK_TILE, DIFF_W), lambda i, j: (i, j, 0)),
        out_shape=jax.ShapeDtypeStruct((b, n, DIFF_W), jnp.float32),
        scratch_shapes=[pltpu.VMEM((2, n, 2 * TOK_TILE), jnp.float32),
                        pltpu.VMEM((n, 2 * TOK_TILE), jnp.bfloat16)],
        compiler_params=pltpu.CompilerParams(dimension_semantics=("parallel", "arbitrary"),
                                             vmem_limit_bytes=ATTN_VMEM_BYTES),
        name="diff_attn",
    )(lam, qt, k, vt, wo, gmat)


def _rope_tables(n_latent):
    rows = n_latent // GRID_W
    row = np.repeat(np.arange(rows, dtype=np.float32), GRID_W)
    col = np.tile(np.arange(GRID_W, dtype=np.float32), rows)
    half = DIFF_QK // 2
    inv = (ROPE_BASE ** (-np.arange(0, half, 2, dtype=np.float32) / half)).astype(np.float32)
    ang = np.concatenate([row[:, None] * inv, row[:, None] * inv, col[:, None] * inv, col[:, None] * inv], axis=-1)
    cos = np.concatenate([np.ones((TOK_TILE, DIFF_QK), np.float32), np.cos(ang)], axis=0)
    sin = np.concatenate([np.zeros((TOK_TILE, DIFF_QK), np.float32), np.sin(ang)], axis=0)
    sign = np.where((np.arange(DIFF_QK) % 16) < 8, -1.0, 1.0).astype(np.float32)
    reps = DIFF_W // DIFF_QK
    return jnp.asarray(np.tile(cos, (1, reps))), jnp.asarray(np.tile(sin * sign, (1, reps)))


def diff_attention_pallas(qkv, lam_init, qn_w, kn_w, lq1, lk1, lq2, lk2, out_w):
    n = qkv.shape[1]
    cos_t, sin_t = _rope_tables(n - TOK_TILE)
    g32 = _group_matrix(DIFF_W, DIFF_QK, 1.0 / DIFF_QK)
    g64 = _group_matrix(DIFF_W, DIFF_V, 1.0 / DIFF_V)
    reps = DIFF_W // DIFF_QK
    qt, k, vt = _attn_prep(qkv, cos_t, sin_t, jnp.tile(qn_w, reps)[None, :], jnp.tile(kn_w, reps)[None, :], g32)
    lam = (jnp.exp(jnp.sum(lq1 * lk1)) - jnp.exp(jnp.sum(lq2 * lk2)) + lam_init).reshape(1).astype(jnp.float32)
    return _attention(lam, qt, k, vt, jnp.tile(out_w, DIFF_HEADS)[None, :], g64, 1.0 - lam_init)


def _dwconv5(cur_ref, prev_ref, next_ref, w_ref, ext_ref, has_prev, has_next):
    t = TOK_TILE
    ext_ref[0:HALO, :] = jnp.where(has_prev, prev_ref[0], 0.0)
    ext_ref[HALO:HALO + t, :] = cur_ref[0]
    ext_ref[HALO + t:2 * HALO + t, :] = jnp.where(has_next, next_ref[0], 0.0)
    acc = None
    for j in range(CONV_K):
        term = ext_ref[pl.ds(HALO - CONV_K // 2 + j, t), :] * w_ref[j:j + 1, :]
        acc = term if acc is None else acc + term
    return acc


def _silu(x):
    return x * jax.nn.sigmoid(x)


def _mixer_prep_kernel(xbc_ref, xbc_p, xbc_n, gq_ref, gq_p, gq_n, sm_ref, sw_ref, sb_ref, gw_ref,
                       bias_ref, scale_ref, ones_ref,
                       xs_out, cm_out, bt_out, q_out, k_out, kt_out, v_out, cols_out, rows_out, ext_ref):
    tile = pl.program_id(1)
    nt = pl.num_programs(1)
    has_prev = tile >= 2
    has_next = (tile >= 1) & (tile < nt - 1)

    u = _silu(_dwconv5(xbc_ref, xbc_p, xbc_n, sw_ref, ext_ref, has_prev, has_next) + sb_ref[...])
    xs_out[0] = u[:, :SSM_W]
    bt_out[0] = u[:, SSM_W:SSM_W + SSM_GROUPS * SSM_N].T.astype(jnp.bfloat16)
    cm_out[0] = u[:, SSM_W + SSM_GROUPS * SSM_N:]

    g = _silu(_dwconv5(gq_ref, gq_p, gq_n, gw_ref, ext_ref, has_prev, has_next))

    def l2n(t):
        return t * lax.rsqrt(_group_sum_sq(t, ones_ref) + EPS)

    q_out[0] = l2n(g[:, :GDN_W]) * (GDN_DK ** -0.5)
    k = l2n(g[:, GDN_W:2 * GDN_W])
    k_out[0] = k
    kt_out[0] = k.T.astype(jnp.bfloat16)
    v_out[0] = g[:, 2 * GDN_W:]

    sm = sm_ref[0] + bias_ref[...]
    sp = jnp.maximum(sm, 0.0) + jnp.log1p(jnp.exp(-jnp.abs(sm)))
    sg = jax.nn.sigmoid(sm)
    vals = sp * scale_ref[...]
    ri = lax.broadcasted_iota(jnp.int32, (TOK_TILE, TOK_TILE), 0)
    ci = lax.broadcasted_iota(jnp.int32, (TOK_TILE, TOK_TILE), 1)
    tri_pre = jnp.where(ri >= ci, 1.0, 0.0).astype(jnp.bfloat16)
    tri_suf = jnp.where(ri <= ci, 1.0, 0.0).astype(jnp.bfloat16)
    pre = None
    suf = None
    for term in _bf16_terms(vals, 3):
        a = jnp.dot(tri_pre, term, preferred_element_type=jnp.float32)
        b = jnp.dot(tri_suf, term, preferred_element_type=jnp.float32)
        pre = a if pre is None else pre + a
        suf = b if suf is None else suf + b
    lane = lax.broadcasted_iota(jnp.int32, (TOK_TILE, SMALL_W), 1)
    bwd_lane = ((lane >= L_SCUM + SSM_HEADS) & (lane < L_BETA)) | (lane >= L_GCUM + GDN_HEADS)
    cum = jnp.where(bwd_lane, suf, pre)
    cols = jnp.where(lane < L_SCUM, sp, jnp.where((lane >= L_BETA) & (lane < L_GCUM), sg, cum))
    cols_out[0] = cols
    rows_out[0] = cols.T


def _mixer_prep(xbc, gqkv, small, ssm_w, ssm_b, gdn_w, bias_vec, scale_vec):
    b, n, _ = xbc.shape
    nt = n // TOK_TILE
    hb = TOK_TILE // HALO
    last = n // HALO - 1
    ones_blk = _group_matrix(GDN_W, GDN_DK, 1.0)

    def cur(w):
        return pl.BlockSpec((1, TOK_TILE, w), lambda i, j: (i, j, 0))

    def prev(w):
        return pl.BlockSpec((1, HALO, w), lambda i, j: (i, jnp.maximum(j * hb - 1, 0), 0))

    def nxt(w):
        return pl.BlockSpec((1, HALO, w), lambda i, j: (i, jnp.minimum((j + 1) * hb, last), 0))

    def const(shape):
        return pl.BlockSpec(shape, lambda i, j: (0,) * len(shape))

    f32 = jnp.float32
    return pl.pallas_call(
        _mixer_prep_kernel,
        grid=(b, nt),
        in_specs=[cur(SSM_XBC), prev(SSM_XBC), nxt(SSM_XBC), cur(GDN_QKV), prev(GDN_QKV), nxt(GDN_QKV), cur(SMALL_W),
                  const((8, SSM_XBC)), const((1, SSM_XBC)), const((8, GDN_QKV)),
                  const((1, SMALL_W)), const((1, SMALL_W)), const((GDN_W, GDN_W))],
        out_specs=[cur(SSM_W), cur(SSM_GROUPS * SSM_N),
                   pl.BlockSpec((1, SSM_GROUPS * SSM_N, TOK_TILE), lambda i, j: (i, 0, j)),
                   cur(GDN_W), cur(GDN_W),
                   pl.BlockSpec((1, GDN_W, TOK_TILE), lambda i, j: (i, 0, j)),
                   cur(GDN_W), cur(SMALL_W),
                   pl.BlockSpec((1, SMALL_W, TOK_TILE), lambda i, j: (i, 0, j))],
        out_shape=[jax.ShapeDtypeStruct((b, n, SSM_W), f32),
                   jax.ShapeDtypeStruct((b, n, SSM_GROUPS * SSM_N), f32),
                   jax.ShapeDtypeStruct((b, SSM_GROUPS * SSM_N, n), jnp.bfloat16),
                   jax.ShapeDtypeStruct((b, n, GDN_W), f32),
                   jax.ShapeDtypeStruct((b, n, GDN_W), f32),
                   jax.ShapeDtypeStruct((b, GDN_W, n), jnp.bfloat16),
                   jax.ShapeDtypeStruct((b, n, GDN_W), f32),
                   jax.ShapeDtypeStruct((b, n, SMALL_W), f32),
                   jax.ShapeDtypeStruct((b, SMALL_W, n), f32)],
        scratch_shapes=[pltpu.VMEM((TOK_TILE + 2 * HALO, SSM_XBC), f32)],
        compiler_params=pltpu.CompilerParams(dimension_semantics=("parallel", "parallel")),
        name="mixer_prep",
    )(xbc, xbc, xbc, gqkv, gqkv, gqkv, small, ssm_w, ssm_b, gdn_w, bias_vec, scale_vec, ones_blk)


def _scan_tile(d, s, nt):
    return jnp.where(d == 0, s, jnp.where(s == 0, 0, nt - s))


def _pick(is_f, arr, base, stride, i, axis):
    a, b = base + i, base + stride + i
    if axis == 1:
        return jnp.where(is_f, arr[:, a:a + 1], arr[:, b:b + 1])
    return jnp.where(is_f, arr[a:a + 1, :], arr[b:b + 1, :])


def _order_masks(is_f):
    ri = lax.broadcasted_iota(jnp.int32, (TOK_TILE, TOK_TILE), 0)
    ci = lax.broadcasted_iota(jnp.int32, (TOK_TILE, TOK_TILE), 1)
    diff = (ri - ci) * jnp.where(is_f, 1, -1)
    return diff >= 0, diff > 0


def _ssd_kernel(xs_ref, cm_ref, bt_ref, cols_ref, rows_ref, y_ref, st_ref):
    is_f = pl.program_id(0) == 0

    @pl.when(pl.program_id(2) == 0)
    def _():
        st_ref[...] = jnp.zeros_like(st_ref)

    bf16 = jnp.bfloat16
    f32 = jnp.float32
    t = TOK_TILE
    cols = cols_ref[0]
    rows = rows_ref[0]
    incl, _ = _order_masks(is_f)
    lane128 = lax.broadcasted_iota(jnp.int32, (t, 128), 1)
    cm = cm_ref[0]
    bt = bt_ref[0]
    hpg = SSM_HEADS // SSM_GROUPS
    for g in range(SSM_GROUPS):
        cg = jnp.where((lane128 >= g * SSM_N) & (lane128 < (g + 1) * SSM_N), cm, 0.0).astype(bf16)
        cb = jnp.dot(cg, bt, preferred_element_type=f32)
        st_g = st_ref[:, g * hpg * SSM_P:(g + 1) * hpg * SSM_P]
        yoff = jnp.dot(cg, st_g.astype(bf16), preferred_element_type=f32)
        btg = bt[g * SSM_N:(g + 1) * SSM_N, :].astype(f32)
        for pair in range(hpg // 2):
            xp = xs_ref[0, :, (g * hpg + 2 * pair) * SSM_P:(g * hpg + 2 * pair + 2) * SSM_P]
            acc_y = None
            acc_s = None
            e_col = None
            dec = None
            for half in range(2):
                hd = g * hpg + 2 * pair + half
                cum_col = _pick(is_f, cols, L_SCUM, SSM_HEADS, hd, 1)
                cum_row = _pick(is_f, rows, L_SCUM, SSM_HEADS, hd, 0)
                dt_col = _pick(is_f, cols, L_DT, SSM_HEADS, hd, 1)
                cum_tot = jnp.where(is_f, cum_row[:, t - 1:t], cum_row[:, 0:1])
                lm = jnp.exp(jnp.where(incl, cum_col - cum_row, -jnp.inf))
                m = (cb * lm).astype(bf16)
                in_half = (lane128 >= half * SSM_P) & (lane128 < (half + 1) * SSM_P)
                xm = jnp.where(in_half, xp * dt_col, 0.0).astype(bf16)
                ty = jnp.dot(m, xm, preferred_element_type=f32)
                ts = jnp.dot((btg * jnp.exp(cum_tot - cum_row)).astype(bf16), xm, preferred_element_type=f32)
                acc_y = ty if acc_y is None else acc_y + ty
                acc_s = ts if acc_s is None else acc_s + ts
                ec = jnp.exp(cum_col)
                dc = jnp.exp(cum_tot)
                e_col = ec if e_col is None else jnp.where(in_half, ec, e_col)
                dec = dc if dec is None else jnp.where(in_half[0:1, :], dc, dec)
            lo = pair * 2 * SSM_P
            c0 = g * hpg * SSM_P + lo
            y_ref[0, 0, :, c0:c0 + 2 * SSM_P] = acc_y + yoff[:, lo:lo + 2 * SSM_P] * e_col
            r0 = g * SSM_N
            st_ref[r0:r0 + SSM_N, c0:c0 + 2 * SSM_P] = st_ref[r0:r0 + SSM_N, c0:c0 + 2 * SSM_P] * dec + acc_s


def _ssd_scan(xs, cm, bt, cols, rows):
    b, n, _ = xs.shape
    nt = n // TOK_TILE

    def tok(w):
        return pl.BlockSpec((1, TOK_TILE, w), lambda d, i, s: (i, _scan_tile(d, s, nt), 0))

    def tr(w):
        return pl.BlockSpec((1, w, TOK_TILE), lambda d, i, s: (i, 0, _scan_tile(d, s, nt)))

    return pl.pallas_call(
        _ssd_kernel,
        grid=(2, b, nt),
        in_specs=[tok(SSM_W), tok(SSM_GROUPS * SSM_N), tr(SSM_GROUPS * SSM_N), tok(SMALL_W), tr(SMALL_W)],
        out_specs=pl.BlockSpec((1, 1, TOK_TILE, SSM_W), lambda d, i, s: (d, i, _scan_tile(d, s, nt), 0)),
        out_shape=jax.ShapeDtypeStruct((2, b, n, SSM_W), jnp.float32),
        scratch_shapes=[pltpu.VMEM((SSM_GROUPS * SSM_N, SSM_W), jnp.float32)],
        compiler_params=pltpu.CompilerParams(dimension_semantics=("parallel", "parallel", "arbitrary")),
        name="ssd_scan",
    )(xs, cm, bt, cols, rows)


def _gdn_kernel(q_ref, k_ref, kt_ref, v_ref, cols_ref, rows_ref, o_ref, s_ref):
    is_f = pl.program_id(0) == 0

    @pl.when(pl.program_id(2) == 0)
    def _():
        s_ref[...] = jnp.zeros_like(s_ref)

    bf16 = jnp.bfloat16
    f32 = jnp.float32
    t = TOK_TILE
    cols = cols_ref[0]
    rows = rows_ref[0]
    incl, strict = _order_masks(is_f)
    lane = lax.broadcasted_iota(jnp.int32, (t, GDN_W), 1)
    sub = lax.broadcasted_iota(jnp.int32, (GDN_W, t), 0)
    head_of_lane = [(lane >= h * GDN_DK) & (lane < (h + 1) * GDN_DK) for h in range(GDN_HEADS)]
    head_of_sub = [(sub >= h * GDN_DK) & (sub < (h + 1) * GDN_DK) for h in range(GDN_HEADS)]

    def by_lane(pieces):
        out = pieces[0]
        for h in range(1, GDN_HEADS):
            out = jnp.where(head_of_lane[h] if pieces[h].shape[0] != 1 else head_of_lane[h][0:1, :], pieces[h], out)
        return out

    q = q_ref[0]
    k = k_ref[0]
    v = v_ref[0]
    kt = kt_ref[0]
    gc_col = [_pick(is_f, cols, L_GCUM, GDN_HEADS, h, 1) for h in range(GDN_HEADS)]
    gc_row = [_pick(is_f, rows, L_GCUM, GDN_HEADS, h, 0) for h in range(GDN_HEADS)]
    gc_end = [jnp.where(is_f, r[:, t - 1:t], r[:, 0:1]) for r in gc_row]
    beta_x = by_lane([_pick(is_f, cols, L_BETA, GDN_HEADS, h, 1) + jnp.zeros((t, GDN_W), f32) for h in range(GDN_HEADS)])
    egc = jnp.exp(by_lane([c + jnp.zeros((t, GDN_W), f32) for c in gc_col]))
    kb = k * beta_x
    vb = v * beta_x
    kbg = kb * egc
    qd = q * egc
    ri = lax.broadcasted_iota(jnp.int32, (t, t), 0)
    ci = lax.broadcasted_iota(jnp.int32, (t, t), 1)
    eye = jnp.where(ri == ci, 1.0, 0.0)
    off_levels = []
    size = 1
    while size < t:
        off_levels.append(((ri // (2 * size)) == (ci // (2 * size))) & ((ri // size) != (ci // size)))
        size *= 2

    a_mats = []
    qkd = []
    for h in range(GDN_HEADS):
        kbm = jnp.where(head_of_lane[h], kb, 0.0).astype(bf16)
        qm = jnp.where(head_of_lane[h], q, 0.0).astype(bf16)
        kk = jnp.dot(kbm, kt, preferred_element_type=f32)
        qk = jnp.dot(qm, kt, preferred_element_type=f32)
        dec = jnp.exp(jnp.where(incl, gc_col[h] - gc_row[h], -jnp.inf))
        a_mats.append(jnp.where(strict, kk * dec, 0.0))
        qkd.append((qk * dec).astype(bf16))
    t_inv = [eye - jnp.where(off_levels[0], a, 0.0) for a in a_mats]
    for lvl_mask in off_levels[1:]:
        tbs = [x.astype(bf16) for x in t_inv]
        es = [jnp.where(lvl_mask, a, 0.0).astype(bf16) for a in a_mats]
        tes = [jnp.dot(tb, e, preferred_element_type=f32).astype(bf16) for tb, e in zip(tbs, es)]
        t_inv = [x - jnp.dot(te, tb, preferred_element_type=f32) for x, te, tb in zip(t_inv, tes, tbs)]
    uw = None
    for h in range(GDN_HEADS):
        rhs = jnp.concatenate([jnp.where(head_of_lane[h], vb, 0.0), jnp.where(head_of_lane[h], kbg, 0.0)], axis=1)
        term = jnp.dot(t_inv[h].astype(bf16), rhs.astype(bf16), preferred_element_type=f32)
        uw = term if uw is None else uw + term

    s_old = s_ref[...]
    sb = s_old.astype(bf16)
    u = uw[:, :GDN_W]
    w = uw[:, GDN_W:]
    v_new = u - jnp.dot(w.astype(bf16), sb, preferred_element_type=f32)
    o = jnp.dot(qd.astype(bf16), sb, preferred_element_type=f32)
    for h in range(GDN_HEADS):
        o = o + jnp.dot(qkd[h], jnp.where(head_of_lane[h], v_new, 0.0).astype(bf16), preferred_element_type=f32)
    o_ref[0, 0] = o

    mult = jnp.exp(gc_end[0] - gc_row[0]) + jnp.zeros((GDN_W, t), f32)
    for h in range(1, GDN_HEADS):
        mult = jnp.where(head_of_sub[h], jnp.exp(gc_end[h] - gc_row[h]), mult)
    kend_t = (kt.astype(f32) * mult).astype(bf16)
    upd = jnp.dot(kend_t, v_new.astype(bf16), preferred_element_type=f32)
    g_end = jnp.exp(by_lane(gc_end))
    same_head = head_of_sub[0] & head_of_lane[0]
    for h in range(1, GDN_HEADS):
        same_head = same_head | (head_of_sub[h] & head_of_lane[h])
    s_ref[...] = jnp.where(same_head, s_old * g_end + upd, 0.0)


def _gdn_scan(q, k, kt, v, cols, rows):
    b, n, _ = q.shape
    nt = n // TOK_TILE

    def tok(w):
        return pl.BlockSpec((1, TOK_TILE, w), lambda d, i, s: (i, _scan_tile(d, s, nt), 0))

    def tr(w):
        return pl.BlockSpec((1, w, TOK_TILE), lambda d, i, s: (i, 0, _scan_tile(d, s, nt)))

    return pl.pallas_call(
        _gdn_kernel,
        grid=(2, b, nt),
        in_specs=[tok(GDN_W), tok(GDN_W), tr(GDN_W), tok(GDN_W), tok(SMALL_W), tr(SMALL_W)],
        out_specs=pl.BlockSpec((1, 1, TOK_TILE, GDN_W), lambda d, i, s: (d, i, _scan_tile(d, s, nt), 0)),
        out_shape=jax.ShapeDtypeStruct((2, b, n, GDN_W), jnp.float32),
        scratch_shapes=[pltpu.VMEM((GDN_W, GDN_W), jnp.float32)],
        compiler_params=pltpu.CompilerParams(dimension_semantics=("parallel", "parallel", "arbitrary")),
        name="gdn_scan",
    )(q, k, kt, v, cols, rows)


def _small_lane_params(ssm_dt_bias, ssm_a_log, gdn_dt_bias, gdn_a_log):
    f32 = jnp.float32
    z = lambda k: jnp.zeros((k,), f32)
    sb = ssm_dt_bias.astype(f32).reshape(-1)
    bias = jnp.concatenate([sb, sb, z(L_GCUM - L_BETA), gdn_dt_bias.astype(f32).reshape(-1),
                            z(SMALL_W - L_GCUM - 2 * GDN_HEADS)])
    scale = jnp.concatenate([z(L_SCUM), -jnp.exp(ssm_a_log.astype(f32)).reshape(-1), z(L_GCUM - L_BETA),
                             -jnp.exp(gdn_a_log.astype(f32)).reshape(-1), z(SMALL_W - L_GCUM - 2 * GDN_HEADS)])
    return bias[None, :], scale[None, :]


def _permuted_w_in(w_in):
    offs = np.cumsum((0,) + SPLIT_SIZES)
    dt0, beta0, a0 = int(offs[3]), int(offs[6]), int(offs[7])
    dt = list(range(dt0, dt0 + SSM_HEADS))
    small = dt + dt + dt + dt + list(range(beta0, beta0 + 2 * GDN_HEADS)) + list(range(a0, a0 + 2 * GDN_HEADS))
    cols = list(range(0, dt0)) + list(range(int(offs[4]), beta0)) + small
    w = w_in[:, np.asarray(cols)]
    return jnp.pad(w, ((0, 0), (0, SMALL_W - len(small))))


def mixers_pallas(xbc, gqkv, small, ssm_conv_w, ssm_conv_b, ssm_dt_bias, ssm_a_log, gdn_conv_w, gdn_dt_bias, gdn_a_log):
    bias_vec, scale_vec = _small_lane_params(ssm_dt_bias, ssm_a_log, gdn_dt_bias, gdn_a_log)
    pad = ((0, 8 - CONV_K), (0, 0))
    xs, cm, bt, q, k, kt, v, cols, rows = _mixer_prep(
        xbc, gqkv, small, jnp.pad(ssm_conv_w, pad), ssm_conv_b[None, :], jnp.pad(gdn_conv_w, pad), bias_vec, scale_vec)
    y = _ssd_scan(xs, cm, bt, cols, rows)
    o = _gdn_scan(q, k, kt, v, cols, rows)
    return xs, y, o


def _expert_kernel(be_ref, nu_ref, x_ref, rg_ref, wg_ref, wu_ref, wd_ref, o_ref, wgb_ref, wub_ref, wdb_ref):
    i = pl.program_id(0)

    @pl.when((i == 0) | (be_ref[i] != be_ref[jnp.maximum(i - 1, 0)]))
    def _():
        wgb_ref[...] = wg_ref[0].astype(jnp.bfloat16)
        wub_ref[...] = wu_ref[0].astype(jnp.bfloat16)
        wdb_ref[...] = wd_ref[0].astype(jnp.bfloat16)

    @pl.when(i < nu_ref[0])
    def _():
        x = x_ref[...]
        g = jnp.dot(x, wgb_ref[...], preferred_element_type=jnp.float32)
        u = jnp.dot(x, wub_ref[...], preferred_element_type=jnp.float32)
        h = (_silu(g) * u).astype(jnp.bfloat16)
        o_ref[...] = (jnp.dot(h, wdb_ref[...], preferred_element_type=jnp.float32) * rg_ref[...]).astype(o_ref.dtype)

    @pl.when(i >= nu_ref[0])
    def _():
        o_ref[...] = jnp.zeros_like(o_ref)


def _expert_blocks(block_e, n_used, buf, row_gate, w_gate, w_up, w_down):
    n_rows, d = buf.shape
    n_blocks = n_rows // MOE_BLOCK
    grid_spec = pltpu.PrefetchScalarGridSpec(
        num_scalar_prefetch=2,
        grid=(n_blocks,),
        in_specs=[pl.BlockSpec((MOE_BLOCK, d), lambda i, be, nu: (i, 0)),
                  pl.BlockSpec((MOE_BLOCK, 1), lambda i, be, nu: (i, 0)),
                  pl.BlockSpec((1, d, D_EXPERT), lambda i, be, nu: (be[i], 0, 0)),
                  pl.BlockSpec((1, d, D_EXPERT), lambda i, be, nu: (be[i], 0, 0)),
                  pl.BlockSpec((1, D_EXPERT, d), lambda i, be, nu: (be[i], 0, 0))],
        out_specs=pl.BlockSpec((MOE_BLOCK, d), lambda i, be, nu: (i, 0)),
        scratch_shapes=[pltpu.VMEM((d, D_EXPERT), jnp.bfloat16), pltpu.VMEM((d, D_EXPERT), jnp.bfloat16),
                        pltpu.VMEM((D_EXPERT, d), jnp.bfloat16)],
    )
    return pl.pallas_call(
        _expert_kernel,
        grid_spec=grid_spec,
        out_shape=jax.ShapeDtypeStruct((n_rows, d), jnp.bfloat16),
        compiler_params=pltpu.CompilerParams(dimension_semantics=("arbitrary",), vmem_limit_bytes=MOE_VMEM_BYTES),
        name="moe_experts",
    )(block_e, n_used, buf, row_gate, w_gate, w_up, w_down)


def routed_experts(t, experts, gates, w_gate, w_up, w_down):
    n, d = t.shape
    n_assign = n * TOP_K
    flat_e = experts.reshape(n_assign)
    order = jnp.argsort(flat_e).astype(jnp.int32)
    slot = jnp.argsort(order).astype(jnp.int32)
    counts = jax.ops.segment_sum(jnp.ones((n_assign,), jnp.int32), flat_e, num_segments=N_EXPERTS)
    starts = jnp.cumsum(counts) - counts
    padded = (counts + MOE_BLOCK - 1) // MOE_BLOCK * MOE_BLOCK
    pad_ends = jnp.cumsum(padded)
    pad_starts = pad_ends - padded
    n_blocks = -(-n_assign // MOE_BLOCK) + N_EXPERTS
    block_start = jnp.arange(n_blocks, dtype=jnp.int32) * MOE_BLOCK
    block_e = jnp.minimum(jnp.searchsorted(pad_ends, block_start, side='right'), N_EXPERTS - 1).astype(jnp.int32)
    n_used = (pad_ends[-1] // MOE_BLOCK).astype(jnp.int32).reshape(1)
    row = jnp.arange(n_blocks * MOE_BLOCK, dtype=jnp.int32)
    row_e = jnp.repeat(block_e, MOE_BLOCK)
    within = row - pad_starts[row_e]
    src_slot = jnp.minimum(starts[row_e] + within, n_assign - 1)
    src_assign = order[src_slot]
    valid = within < counts[row_e]
    src_tok = jnp.where(valid, src_assign // TOP_K, 0)
    row_gate = jnp.where(valid, gates.reshape(n_assign).astype(jnp.float32)[src_assign], 0.0)[:, None]
    out = _expert_blocks(block_e, n_used, t[src_tok], row_gate, w_gate, w_up, w_down)
    pos = (pad_starts[flat_e] + slot - starts[flat_e]).reshape(n, TOP_K)
    y = out[pos[:, 0]].astype(jnp.float32)
    for k in range(1, TOP_K):
        y = y + out[pos[:, k]].astype(jnp.float32)
    return y


def hier_moe(t, t_rows, router_g_w, router_g_b, router_e_w, router_e_b, w_gate, w_up, w_down):
    n = t.shape[0]
    grp_prob = jax.nn.softmax(jnp.dot(t, router_g_w, preferred_element_type=jnp.float32)
                              + router_g_b.astype(jnp.float32), axis=-1)
    p_grp, grp = lax.top_k(grp_prob, 1)
    e_logits = (jnp.dot(t, router_e_w, preferred_element_type=jnp.float32)
                + router_e_b.astype(jnp.float32)).reshape(n, N_EGROUPS, EXPERTS_PER_GROUP)
    sel = e_logits[jnp.arange(n), grp[:, 0]]
    p_top, idx = lax.top_k(jax.nn.softmax(sel, axis=-1), TOP_K)
    gates = p_grp * p_top / jnp.sum(p_top, axis=-1, keepdims=True)
    experts = grp * EXPERTS_PER_GROUP + idx
    return routed_experts(t_rows, experts, gates, w_gate, w_up, w_down)


def _matmul_kernel(a_ref, b_ref, o_ref):
    o_ref[...] = jnp.dot(a_ref[...].astype(jnp.bfloat16), b_ref[...],
                         preferred_element_type=jnp.float32)


def _matmul(a, b, tm, tn):
    m, k = a.shape
    n = b.shape[1]
    return pl.pallas_call(
        _matmul_kernel,
        grid=(m // tm, n // tn),
        in_specs=[pl.BlockSpec((tm, k), lambda i, j: (i, 0)),
                  pl.BlockSpec((k, tn), lambda i, j: (0, j))],
        out_specs=pl.BlockSpec((tm, tn), lambda i, j: (i, j)),
        out_shape=jax.ShapeDtypeStruct((m, n), jnp.float32),
        compiler_params=pltpu.CompilerParams(dimension_semantics=("parallel", "parallel")),
        name="dense",
    )(a, b)


def _dense(a, w, tm=512):
    lead = a.shape[:-1]
    k = a.shape[-1]
    n = w.shape[1]
    a2 = a.reshape(-1, k)
    m = a2.shape[0]
    m_pad = -(-m // 8) * 8
    if m_pad != m:
        a2 = jnp.pad(a2, ((0, m_pad - m), (0, 0)))
    tm = min(tm, m_pad)
    n_pad = -(-n // 128) * 128
    tn = next(c for c in (1024, 768, 640, 512, 384, 256, 128) if n_pad % c == 0)
    wb = jnp.pad(w, ((0, 0), (0, n_pad - n))).astype(jnp.bfloat16)
    out = _matmul(a2, wb, tm, tn)[:m, :n]
    return out.reshape(lead + (n,))


IN_SLABS = (3 * DIFF_W, SSM_W, SSM_XBC, GDN_QKV, GDN_W, SMALL_W)
PROJ_VMEM_BYTES = 44 * 1024 * 1024


def _mod_row_spec(bsz):
    return pl.BlockSpec((1, MOD_CHUNKS, D_MODEL), lambda i, j: (jnp.where(j == 0, bsz, i), 0, 0))


def _in_proj_kernel(x_ref, nw_ref, mod_ref, w_ref, *out_refs):
    x = x_ref[0]
    xn = x * lax.rsqrt(jnp.mean(x * x, axis=-1, keepdims=True) + EPS) * nw_ref[...]
    mod = mod_ref[0]
    xm = (xn * (1.0 + mod[1:2, :]) + mod[0:1, :]).astype(jnp.bfloat16)
    c0 = 0
    for o_ref, width in zip(out_refs, IN_SLABS):
        o_ref[0] = jnp.dot(xm, w_ref[:, c0:c0 + width], preferred_element_type=jnp.float32)
        c0 += width


def _in_proj(x, norm_w, mod, w):
    b, n, d = x.shape
    tok = lambda width: pl.BlockSpec((1, TOK_TILE, width), lambda i, j: (i, j, 0))
    return pl.pallas_call(
        _in_proj_kernel,
        grid=(b, n // TOK_TILE),
        in_specs=[tok(d), pl.BlockSpec((1, d), lambda i, j: (0, 0)), _mod_row_spec(b),
                  pl.BlockSpec(w.shape, lambda i, j: (0, 0))],
        out_specs=[tok(width) for width in IN_SLABS],
        out_shape=[jax.ShapeDtypeStruct((b, n, width), jnp.float32) for width in IN_SLABS],
        compiler_params=pltpu.CompilerParams(dimension_semantics=("parallel", "parallel"),
                                             vmem_limit_bytes=PROJ_VMEM_BYTES),
        name="in_proj",
    )(x, norm_w, mod, w)


def _out_proj_kernel(att_ref, y0_ref, y1_ref, xs_ref, z_ref, o0_ref, o1_ref, gate_ref, x_ref, mod_ref,
                     dvec_ref, snw_ref, gnw_ref, n2w_ref, g_ref, w_ref, xo_ref, f_ref, fb_ref):
    ys = (y0_ref[0, 0] + y1_ref[0, 0] + dvec_ref[...] * xs_ref[0]) * _silu(z_ref[0])
    gw = SSM_W // SSM_GROUPS
    parts = [att_ref[0]]
    for g in range(SSM_GROUPS):
        yg = ys[:, g * gw:(g + 1) * gw]
        parts.append(yg * lax.rsqrt(jnp.mean(yg * yg, axis=-1, keepdims=True) + EPS) * snw_ref[:, g * gw:(g + 1) * gw])
    o = o0_ref[0, 0] + o1_ref[0, 0]
    parts.append(o * lax.rsqrt(_group_sum_sq(o, g_ref) + EPS) * gnw_ref[...] * _silu(gate_ref[0]))
    ml = jnp.concatenate(parts, axis=1).astype(jnp.bfloat16)
    mod = mod_ref[0]
    xn = x_ref[0] + mod[2:3, :] * jnp.dot(ml, w_ref[...], preferred_element_type=jnp.float32)
    xo_ref[0] = xn
    f = xn * lax.rsqrt(jnp.mean(xn * xn, axis=-1, keepdims=True) + EPS) * n2w_ref[...]
    f = f * (1.0 + mod[4:5, :]) + mod[3:4, :]
    f_ref[0] = f
    fb_ref[0] = f.astype(jnp.bfloat16)


def _out_proj(att, y, xs, z, o, gate, x, mod, dvec, snw, gnw, n2w, g64, w, latent_only):
    b, n, d = x.shape
    t0 = 1 if latent_only else 0
    n_out = n - t0 * TOK_TILE
    tok = lambda width: pl.BlockSpec((1, TOK_TILE, width), lambda i, j: (i, j + t0, 0))
    dirs = lambda width, k: pl.BlockSpec((1, 1, TOK_TILE, width), lambda i, j: (k, i, j + t0, 0))
    vec = lambda width: pl.BlockSpec((1, width), lambda i, j: (0, 0))
    out_tok = pl.BlockSpec((1, TOK_TILE, d), lambda i, j: (i, j, 0))
    mod_spec = pl.BlockSpec((1, MOD_CHUNKS, D_MODEL), lambda i, j: (i, 0, 0)) if latent_only else _mod_row_spec(b)
    return pl.pallas_call(
        _out_proj_kernel,
        grid=(b, n_out // TOK_TILE),
        in_specs=[tok(DIFF_W), dirs(SSM_W, 0), dirs(SSM_W, 1), tok(SSM_W), tok(SSM_W), dirs(GDN_W, 0), dirs(GDN_W, 1),
                  tok(GDN_W), tok(d), mod_spec, vec(SSM_W), vec(SSM_W), vec(GDN_W), vec(d),
                  pl.BlockSpec(g64.shape, lambda i, j: (0, 0)), pl.BlockSpec(w.shape, lambda i, j: (0, 0))],
        out_specs=[out_tok, out_tok, out_tok],
        out_shape=[jax.ShapeDtypeStruct((b, n_out, d), jnp.float32), jax.ShapeDtypeStruct((b, n_out, d), jnp.float32),
                   jax.ShapeDtypeStruct((b, n_out, d), jnp.bfloat16)],
        compiler_params=pltpu.CompilerParams(dimension_semantics=("parallel", "parallel"),
                                             vmem_limit_bytes=PROJ_VMEM_BYTES),
        name="out_proj",
    )(att, y, y, xs, z, o, o, gate, x, mod, dvec, snw, gnw, n2w, g64, w)


def kernel(x, c, ctx, c_ctx, w_mod, b_mod, norm1_w, norm2_w, w_in, w_out,
           diff_qn_w, diff_kn_w, diff_lq1, diff_lk1, diff_lq2, diff_lk2, diff_norm_w,
           ssm_conv_w, ssm_conv_b, ssm_dt_bias, ssm_a_log, ssm_d, ssm_norm_w,
           gdn_conv_w, gdn_dt_bias, gdn_a_log, gdn_norm_w,
           router_g_w, router_g_b, router_e_w, router_e_b, exp_w_gate, exp_w_up, exp_w_down):
    assert ctx.shape[1] == CTX_LEN == TOK_TILE and x.shape[1] % TOK_TILE == 0
    bsz = x.shape[0]
    xs_all = jnp.concatenate([ctx, x], axis=1)
    n_tok = xs_all.shape[1]
    is_lat = (jnp.arange(n_tok) >= CTX_LEN)[None, :, None]
    cc = jnp.concatenate([c, c_ctx[None, :]], axis=0)
    bf16 = jnp.bfloat16
    g64 = _group_matrix(GDN_W, GDN_DV, 1.0 / GDN_DV)
    for l in range(DEPTH):
        last = l == DEPTH - 1
        lam_init = 0.8 - 0.6 * math.exp(-0.3 * l)
        mod = (_dense(jax.nn.silu(cc), w_mod[l]) + b_mod[l]).reshape(bsz + 1, MOD_CHUNKS, D_MODEL)
        qkv, z, xbc, gqkv, gate, small = _in_proj(xs_all, norm1_w[l][None, :], mod, _permuted_w_in(w_in[l]).astype(bf16))
        att = diff_attention_pallas(qkv, lam_init, diff_qn_w[l], diff_kn_w[l],
                                    diff_lq1[l], diff_lk1[l], diff_lq2[l], diff_lk2[l], diff_norm_w[l])
        xs, y, o = mixers_pallas(xbc, gqkv, small, ssm_conv_w[l], ssm_conv_b[l],
                                 ssm_dt_bias[l], ssm_a_log[l], gdn_conv_w[l], gdn_dt_bias[l], gdn_a_log[l])
        xs_all, f, fb = _out_proj(att, y, xs, z, o, gate, xs_all, mod, jnp.repeat(ssm_d[l], SSM_P)[None, :],
                                  ssm_norm_w[l][None, :], jnp.tile(gdn_norm_w[l], GDN_HEADS)[None, :],
                                  norm2_w[l][None, :], g64, w_out[l].astype(bf16), latent_only=last)
        moe_w = (router_g_w[l], router_g_b[l], router_e_w[l], router_e_b[l],
                 exp_w_gate[l], exp_w_up[l], exp_w_down[l])
        if last:
            yl = hier_moe(f.reshape(-1, D_MODEL), fb.reshape(-1, D_MODEL), *moe_w)
            return xs_all + mod[:bsz, None, 5, :] * yl.reshape(xs_all.shape)
        ym = hier_moe(f.reshape(-1, D_MODEL), fb.reshape(-1, D_MODEL), *moe_w)
        mod5 = jnp.where(is_lat, mod[:bsz, None, 5, :], mod[bsz, 5, :])
        xs_all = xs_all + mod5 * ym.reshape(xs_all.shape)
```

```python
import functools
import math
import jax
import jax.numpy as jnp
from jax import lax
import numpy as np
from jax.experimental import pallas as pl
from jax.experimental.pallas import tpu as pltpu

D_MODEL = 1024
DEPTH = 2
CTX_LEN = 256
GRID_W = 64
EPS = 1e-6
MOD_CHUNKS = 6

DIFF_HEADS = 4
DIFF_QK = 32
DIFF_V = 2 * DIFF_QK
DIFF_W = DIFF_HEADS * DIFF_V
ROPE_BASE = 10000.0

SSM_HEADS = 8
SSM_P = 64
SSM_GROUPS = 2
SSM_N = 64
SSM_W = SSM_HEADS * SSM_P
SSM_XBC = SSM_W + 2 * SSM_GROUPS * SSM_N
CONV_K = 5

GDN_HEADS = 4
GDN_DK = 64
GDN_DV = 64
GDN_QKV = GDN_HEADS * (2 * GDN_DK + GDN_DV)
GDN_W = GDN_HEADS * GDN_DV

D_MIX = DIFF_W + SSM_W + GDN_W
SPLIT_SIZES = (3 * DIFF_W, SSM_W, SSM_XBC, SSM_HEADS, GDN_QKV, GDN_W, 2 * GDN_HEADS, 2 * GDN_HEADS)

N_EGROUPS = 4
EXPERTS_PER_GROUP = 8
N_EXPERTS = N_EGROUPS * EXPERTS_PER_GROUP
TOP_K = 2
D_EXPERT = 512
MOE_BLOCK = 256

TOK_TILE = 256
LOG2E = 1.4426950408889634
EXP_ROWS = 64
ATTN_VMEM_BYTES = 52 * 1024 * 1024
MOE_VMEM_BYTES = 40 * 1024 * 1024
HALO = 8
SMALL_W = 128
L_DT = 0
L_SCUM = 16
L_BETA = 32
L_GCUM = 40


def rmsnorm(x, w):
    x32 = x.astype(jnp.float32)
    y = x32 * lax.rsqrt(jnp.mean(x32 * x32, axis=-1, keepdims=True) + EPS)
    return y.astype(x.dtype) * w


def modulate(x, shift, scale):
    return x * (1.0 + scale) + shift


def _bf16_terms(x, n):
    out = []
    for _ in range(n):
        t = x.astype(jnp.bfloat16)
        out.append(t)
        x = x - t.astype(jnp.float32)
    return out


def _group_sum_sq(x, g_ref):
    hi, lo = _bf16_terms(x * x, 2)
    g = g_ref[...]
    return (jnp.dot(hi, g, preferred_element_type=jnp.float32)
            + jnp.dot(lo, g, preferred_element_type=jnp.float32))


def _group_matrix(width, group, value):
    idx = np.arange(width) // group
    return jnp.asarray((idx[:, None] == idx[None, :]).astype(np.float32) * value, jnp.bfloat16)


def _attn_prep_kernel(qkv_ref, cos_ref, sin_ref, wq_ref, wk_ref, g_ref, qt_out, k_out, vt_out):
    x = qkv_ref[0]
    cos = cos_ref[...]
    sin = sin_ref[...]
    lane = lax.broadcasted_iota(jnp.int32, cos.shape, 1)
    lo_half = (lane % 16) < 8

    def norm_rope(t, w):
        y = t * lax.rsqrt(_group_sum_sq(t, g_ref) + EPS) * w
        rot = jnp.where(lo_half, pltpu.roll(y, DIFF_W - 8, 1), pltpu.roll(y, 8, 1))
        return y * cos + rot * sin

    q = norm_rope(x[:, 0:DIFF_W], wq_ref[...]) * (DIFF_QK ** -0.5 * LOG2E)
    qt_out[0] = q.T
    k_out[0] = norm_rope(x[:, DIFF_W:2 * DIFF_W], wk_ref[...]).astype(jnp.bfloat16)
    vt_out[0] = x[:, 2 * DIFF_W:3 * DIFF_W].T.astype(jnp.bfloat16)


def _attn_prep(qkv, cos_t, sin_t, wq, wk, gmat):
    b, n, _ = qkv.shape
    nt = n // TOK_TILE
    tok = pl.BlockSpec((1, TOK_TILE, DIFF_W), lambda i, j: (i, j, 0))
    tr = pl.BlockSpec((1, DIFF_W, TOK_TILE), lambda i, j: (i, 0, j))
    return pl.pallas_call(
        _attn_prep_kernel,
        grid=(b, nt),
        in_specs=[pl.BlockSpec((1, TOK_TILE, 3 * DIFF_W), lambda i, j: (i, j, 0)),
                  pl.BlockSpec((TOK_TILE, DIFF_W), lambda i, j: (j, 0)),
                  pl.BlockSpec((TOK_TILE, DIFF_W), lambda i, j: (j, 0)),
                  pl.BlockSpec((1, DIFF_W), lambda i, j: (0, 0)),
                  pl.BlockSpec((1, DIFF_W), lambda i, j: (0, 0)),
                  pl.BlockSpec((DIFF_W, DIFF_W), lambda i, j: (0, 0))],
        out_specs=[tr, tok, tr],
        out_shape=[jax.ShapeDtypeStruct((b, DIFF_W, n), jnp.float32),
                   jax.ShapeDtypeStruct((b, n, DIFF_W), jnp.bfloat16),
                   jax.ShapeDtypeStruct((b, DIFF_W, n), jnp.bfloat16)],
        compiler_params=pltpu.CompilerParams(dimension_semantics=("parallel", "parallel")),
        name="attn_prep",
    )(qkv, cos_t, sin_t, wq, wk, gmat)


def _attn_kernel(lam_ref, qt_ref, k_ref, vt_ref, wo_ref, g_ref, o_ref, s_ref, p_ref, *, n_keys, out_scale):
    tile = pl.program_id(1)
    lam = lam_ref[0]
    sub = lax.broadcasted_iota(jnp.int32, (DIFF_W, TOK_TILE), 0)

    def attend(nk):
        n_chunks = nk // TOK_TILE
        qt = qt_ref[0]

        def masked_qt(head):
            return jnp.concatenate(
                [jnp.where((sub >= (2 * head + mp) * DIFF_QK) & (sub < (2 * head + mp + 1) * DIFF_QK), qt, 0.0)
                 for mp in range(2)], axis=1).astype(jnp.bfloat16)

        def score_chunk(qt2, buf, c, mx8):
            s = jnp.dot(k_ref[0, c * TOK_TILE:(c + 1) * TOK_TILE, :], qt2, preferred_element_type=jnp.float32)
            s_ref[buf, c * TOK_TILE:(c + 1) * TOK_TILE, :] = s
            cm = jnp.max(s.reshape(TOK_TILE // 8, 8, 2 * TOK_TILE), axis=0)
            return cm if mx8 is None else jnp.maximum(mx8, cm)

        mx8 = None
        qt2 = masked_qt(0)
        for c in range(n_chunks):
            mx8 = score_chunk(qt2, 0, c, mx8)
        acc = None
        for head in range(DIFF_HEADS):
            buf = head % 2
            mx = jnp.max(mx8, axis=0, keepdims=True)
            nxt = head + 1 < DIFF_HEADS
            if nxt:
                qt2 = masked_qt(head + 1)
            mx8 = None
            ls8 = None
            for c in range(n_chunks):
                if nxt:
                    mx8 = score_chunk(qt2, 1 - buf, c, mx8)
                for r in range(c * TOK_TILE // EXP_ROWS, (c + 1) * TOK_TILE // EXP_ROWS):
                    e = jnp.exp2(s_ref[buf, r * EXP_ROWS:(r + 1) * EXP_ROWS, :] - mx)
                    p_ref[r * EXP_ROWS:(r + 1) * EXP_ROWS, :] = e.astype(jnp.bfloat16)
                    es = jnp.sum(e.reshape(EXP_ROWS // 8, 8, 2 * TOK_TILE), axis=0)
                    ls8 = es if ls8 is None else ls8 + es
            ls = jnp.sum(ls8, axis=0, keepdims=True)
            ot = jnp.dot(vt_ref[0, :, :nk], p_ref[:nk, :], preferred_element_type=jnp.float32) / ls
            in_head = (sub >= head * DIFF_V) & (sub < (head + 1) * DIFF_V)
            part = jnp.where(in_head, ot[:, :TOK_TILE] - lam * ot[:, TOK_TILE:], 0.0)
            acc = part if acc is None else acc + part
        o = acc.T
        y = o * lax.rsqrt(_group_sum_sq(o, g_ref) + EPS)
        o_ref[0] = y * wo_ref[...] * out_scale

    @pl.when(tile == 0)
    def _():
        attend(TOK_TILE)

    @pl.when(tile > 0)
    def _():
        attend(n_keys)


def _attention(lam, qt, k, vt, wo, gmat, out_scale):
    b, n, _ = k.shape
    nt = n // TOK_TILE
    kern = functools.partial(_attn_kernel, n_keys=n, out_scale=out_scale)
    return pl.pallas_call(
        kern,
        grid=(b, nt),
        in_specs=[pl.BlockSpec(memory_space=pltpu.SMEM),
                  pl.BlockSpec((1, DIFF_W, TOK_TILE), lambda i, j: (i, 0, j)),
                  pl.BlockSpec((1, n, DIFF_W), lambda i, j: (i, 0, 0)),
                  pl.BlockSpec((1, DIFF_W, n), lambda i, j: (i, 0, 0)),
                  pl.BlockSpec((1, DIFF_W), lambda i, j: (0, 0)),
                  pl.BlockSpec((DIFF_W, DIFF_W), lambda i, j: (0, 0))],
        out_specs=pl.BlockSpec((1, TOK_TILE, DIFF_W), lambda i, j: (i, j, 0)),
        out_shape=jax.ShapeDtypeStruct((b, n, DIFF_W), jnp.float32),
        scratch_shapes=[pltpu.VMEM((2, n, 2 * TOK_TILE), jnp.float32),
                        pltpu.VMEM((n, 2 * TOK_TILE), jnp.bfloat16)],
        compiler_params=pltpu.CompilerParams(dimension_semantics=("parallel", "arbitrary"),
                                             vmem_limit_bytes=ATTN_VMEM_BYTES),
        name="diff_attn",
    )(lam, qt, k, vt, wo, gmat)


def _rope_tables(n_latent):
    rows = n_latent // GRID_W
    row = np.repeat(np.arange(rows, dtype=np.float32), GRID_W)
    col = np.tile(np.arange(GRID_W, dtype=np.float32), rows)
    half = DIFF_QK // 2
    inv = (ROPE_BASE ** (-np.arange(0, half, 2, dtype=np.float32) / half)).astype(np.float32)
    ang = np.concatenate([row[:, None] * inv, row[:, None] * inv, col[:, None] * inv, col[:, None] * inv], axis=-1)
    cos = np.concatenate([np.ones((TOK_TILE, DIFF_QK), np.float32), np.cos(ang)], axis=0)
    sin = np.concatenate([np.zeros((TOK_TILE, DIFF_QK), np.float32), np.sin(ang)], axis=0)
    sign = np.where((np.arange(DIFF_QK) % 16) < 8, -1.0, 1.0).astype(np.float32)
    reps = DIFF_W // DIFF_QK
    return jnp.asarray(np.tile(cos, (1, reps))), jnp.asarray(np.tile(sin * sign, (1, reps)))


def diff_attention_pallas(qkv, lam_init, qn_w, kn_w, lq1, lk1, lq2, lk2, out_w):
    n = qkv.shape[1]
    cos_t, sin_t = _rope_tables(n - TOK_TILE)
    g32 = _group_matrix(DIFF_W, DIFF_QK, 1.0 / DIFF_QK)
    g64 = _group_matrix(DIFF_W, DIFF_V, 1.0 / DIFF_V)
    reps = DIFF_W // DIFF_QK
    qt, k, vt = _attn_prep(qkv, cos_t, sin_t, jnp.tile(qn_w, reps)[None, :], jnp.tile(kn_w, reps)[None, :], g32)
    lam = (jnp.exp(jnp.sum(lq1 * lk1)) - jnp.exp(jnp.sum(lq2 * lk2)) + lam_init).reshape(1).astype(jnp.float32)
    return _attention(lam, qt, k, vt, jnp.tile(out_w, DIFF_HEADS)[None, :], g64, 1.0 - lam_init)


def _dwconv5(cur_ref, prev_ref, next_ref, w_ref, ext_ref, has_prev, has_next):
    t = TOK_TILE
    ext_ref[0:HALO, :] = jnp.where(has_prev, prev_ref[0], 0.0)
    ext_ref[HALO:HALO + t, :] = cur_ref[0]
    ext_ref[HALO + t:2 * HALO + t, :] = jnp.where(has_next, next_ref[0], 0.0)
    acc = None
    for j in range(CONV_K):
        term = ext_ref[pl.ds(HALO - CONV_K // 2 + j, t), :] * w_ref[j:j + 1, :]
        acc = term if acc is None else acc + term
    return acc


def _silu(x):
    return x * jax.nn.sigmoid(x)


def _mixer_prep_kernel(xbc_ref, xbc_p, xbc_n, gq_ref, gq_p, gq_n, sm_ref, sw_ref, sb_ref, gw_ref,
                       bias_ref, scale_ref, ones_ref,
                       xs_out, cm_out, bt_out, q_out, k_out, kt_out, v_out, cols_out, rows_out, ext_ref):
    tile = pl.program_id(1)
    nt = pl.num_programs(1)
    has_prev = tile >= 2
    has_next = (tile >= 1) & (tile < nt - 1)

    u = _silu(_dwconv5(xbc_ref, xbc_p, xbc_n, sw_ref, ext_ref, has_prev, has_next) + sb_ref[...])
    xs_out[0] = u[:, :SSM_W]
    bt_out[0] = u[:, SSM_W:SSM_W + SSM_GROUPS * SSM_N].T.astype(jnp.bfloat16)
    cm_out[0] = u[:, SSM_W + SSM_GROUPS * SSM_N:]

    g = _silu(_dwconv5(gq_ref, gq_p, gq_n, gw_ref, ext_ref, has_prev, has_next))

    def l2n(t):
        return t * lax.rsqrt(_group_sum_sq(t, ones_ref) + EPS)

    q_out[0] = l2n(g[:, :GDN_W]) * (GDN_DK ** -0.5)
    k = l2n(g[:, GDN_W:2 * GDN_W])
    k_out[0] = k
    kt_out[0] = k.T.astype(jnp.bfloat16)
    v_out[0] = g[:, 2 * GDN_W:]

    sm = sm_ref[0] + bias_ref[...]
    sp = jnp.maximum(sm, 0.0) + jnp.log1p(jnp.exp(-jnp.abs(sm)))
    sg = jax.nn.sigmoid(sm)
    vals = sp * scale_ref[...]
    ri = lax.broadcasted_iota(jnp.int32, (TOK_TILE, TOK_TILE), 0)
    ci = lax.broadcasted_iota(jnp.int32, (TOK_TILE, TOK_TILE), 1)
    tri_pre = jnp.where(ri >= ci, 1.0, 0.0).astype(jnp.bfloat16)
    tri_suf = jnp.where(ri <= ci, 1.0, 0.0).astype(jnp.bfloat16)
    pre = None
    suf = None
    for term in _bf16_terms(vals, 3):
        a = jnp.dot(tri_pre, term, preferred_element_type=jnp.float32)
        b = jnp.dot(tri_suf, term, preferred_element_type=jnp.float32)
        pre = a if pre is None else pre + a
        suf = b if suf is None else suf + b
    lane = lax.broadcasted_iota(jnp.int32, (TOK_TILE, SMALL_W), 1)
    bwd_lane = ((lane >= L_SCUM + SSM_HEADS) & (lane < L_BETA)) | (lane >= L_GCUM + GDN_HEADS)
    cum = jnp.where(bwd_lane, suf, pre)
    cols = jnp.where(lane < L_SCUM, sp, jnp.where((lane >= L_BETA) & (lane < L_GCUM), sg, cum))
    cols_out[0] = cols
    rows_out[0] = cols.T


def _mixer_prep(xbc, gqkv, small, ssm_w, ssm_b, gdn_w, bias_vec, scale_vec):
    b, n, _ = xbc.shape
    nt = n // TOK_TILE
    hb = TOK_TILE // HALO
    last = n // HALO - 1
    ones_blk = _group_matrix(GDN_W, GDN_DK, 1.0)

    def cur(w):
        return pl.BlockSpec((1, TOK_TILE, w), lambda i, j: (i, j, 0))

    def prev(w):
        return pl.BlockSpec((1, HALO, w), lambda i, j: (i, jnp.maximum(j * hb - 1, 0), 0))

    def nxt(w):
        return pl.BlockSpec((1, HALO, w), lambda i, j: (i, jnp.minimum((j + 1) * hb, last), 0))

    def const(shape):
        return pl.BlockSpec(shape, lambda i, j: (0,) * len(shape))

    f32 = jnp.float32
    return pl.pallas_call(
        _mixer_prep_kernel,
        grid=(b, nt),
        in_specs=[cur(SSM_XBC), prev(SSM_XBC), nxt(SSM_XBC), cur(GDN_QKV), prev(GDN_QKV), nxt(GDN_QKV), cur(SMALL_W),
                  const((8, SSM_XBC)), const((1, SSM_XBC)), const((8, GDN_QKV)),
                  const((1, SMALL_W)), const((1, SMALL_W)), const((GDN_W, GDN_W))],
        out_specs=[cur(SSM_W), cur(SSM_GROUPS * SSM_N),
                   pl.BlockSpec((1, SSM_GROUPS * SSM_N, TOK_TILE), lambda i, j: (i, 0, j)),
                   cur(GDN_W), cur(GDN_W),
                   pl.BlockSpec((1, GDN_W, TOK_TILE), lambda i, j: (i, 0, j)),
                   cur(GDN_W), cur(SMALL_W),
                   pl.BlockSpec((1, SMALL_W, TOK_TILE), lambda i, j: (i, 0, j))],
        out_shape=[jax.ShapeDtypeStruct((b, n, SSM_W), f32),
                   jax.ShapeDtypeStruct((b, n, SSM_GROUPS * SSM_N), f32),
                   jax.ShapeDtypeStruct((b, SSM_GROUPS * SSM_N, n), jnp.bfloat16),
                   jax.ShapeDtypeStruct((b, n, GDN_W), f32),
                   jax.ShapeDtypeStruct((b, n, GDN_W), f32),
                   jax.ShapeDtypeStruct((b, GDN_W, n), jnp.bfloat16),
                   jax.ShapeDtypeStruct((b, n, GDN_W), f32),
                   jax.ShapeDtypeStruct((b, n, SMALL_W), f32),
                   jax.ShapeDtypeStruct((b, SMALL_W, n), f32)],
        scratch_shapes=[pltpu.VMEM((TOK_TILE + 2 * HALO, SSM_XBC), f32)],
        compiler_params=pltpu.CompilerParams(dimension_semantics=("parallel", "parallel")),
        name="mixer_prep",
    )(xbc, xbc, xbc, gqkv, gqkv, gqkv, small, ssm_w, ssm_b, gdn_w, bias_vec, scale_vec, ones_blk)


def _scan_tile(d, s, nt):
    return jnp.where(d == 0, s, jnp.where(s == 0, 0, nt - s))


def _pick(is_f, arr, base, stride, i, axis):
    a, b = base + i, base + stride + i
    if axis == 1:
        return jnp.where(is_f, arr[:, a:a + 1], arr[:, b:b + 1])
    return jnp.where(is_f, arr[a:a + 1, :], arr[b:b + 1, :])


def _order_masks(is_f):
    ri = lax.broadcasted_iota(jnp.int32, (TOK_TILE, TOK_TILE), 0)
    ci = lax.broadcasted_iota(jnp.int32, (TOK_TILE, TOK_TILE), 1)
    diff = (ri - ci) * jnp.where(is_f, 1, -1)
    return diff >= 0, diff > 0


def _ssd_kernel(xs_ref, cm_ref, bt_ref, cols_ref, rows_ref, y_ref, st_ref):
    is_f = pl.program_id(0) == 0

    @pl.when(pl.program_id(2) == 0)
    def _():
        st_ref[...] = jnp.zeros_like(st_ref)

    bf16 = jnp.bfloat16
    f32 = jnp.float32
    t = TOK_TILE
    cols = cols_ref[0]
    rows = rows_ref[0]
    incl, _ = _order_masks(is_f)
    lane128 = lax.broadcasted_iota(jnp.int32, (t, 128), 1)
    cm = cm_ref[0]
    bt = bt_ref[0]
    hpg = SSM_HEADS // SSM_GROUPS
    for g in range(SSM_GROUPS):
        cg = jnp.where((lane128 >= g * SSM_N) & (lane128 < (g + 1) * SSM_N), cm, 0.0).astype(bf16)
        cb = jnp.dot(cg, bt, preferred_element_type=f32)
        st_g = st_ref[:, g * hpg * SSM_P:(g + 1) * hpg * SSM_P]
        yoff = jnp.dot(cg, st_g.astype(bf16), preferred_element_type=f32)
        btg = bt[g * SSM_N:(g + 1) * SSM_N, :].astype(f32)
        for pair in range(hpg // 2):
            xp = xs_ref[0, :, (g * hpg + 2 * pair) * SSM_P:(g * hpg + 2 * pair + 2) * SSM_P]
            acc_y = None
            acc_s = None
            e_col = None
            dec = None
            for half in range(2):
                hd = g * hpg + 2 * pair + half
                cum_col = _pick(is_f, cols, L_SCUM, SSM_HEADS, hd, 1)
                cum_row = _pick(is_f, rows, L_SCUM, SSM_HEADS, hd, 0)
                dt_col = _pick(is_f, cols, L_DT, SSM_HEADS, hd, 1)
                cum_tot = jnp.where(is_f, cum_row[:, t - 1:t], cum_row[:, 0:1])
                lm = jnp.exp(jnp.where(incl, cum_col - cum_row, -jnp.inf))
                m = (cb * lm).astype(bf16)
                in_half = (lane128 >= half * SSM_P) & (lane128 < (half + 1) * SSM_P)
                xm = jnp.where(in_half, xp * dt_col, 0.0).astype(bf16)
                ty = jnp.dot(m, xm, preferred_element_type=f32)
                ts = jnp.dot((btg * jnp.exp(cum_tot - cum_row)).astype(bf16), xm, preferred_element_type=f32)
                acc_y = ty if acc_y is None else acc_y + ty
                acc_s = ts if acc_s is None else acc_s + ts
                ec = jnp.exp(cum_col)
                dc = jnp.exp(cum_tot)
                e_col = ec if e_col is None else jnp.where(in_half, ec, e_col)
                dec = dc if dec is None else jnp.where(in_half[0:1, :], dc, dec)
            lo = pair * 2 * SSM_P
            c0 = g * hpg * SSM_P + lo
            y_ref[0, 0, :, c0:c0 + 2 * SSM_P] = acc_y + yoff[:, lo:lo + 2 * SSM_P] * e_col
            r0 = g * SSM_N
            st_ref[r0:r0 + SSM_N, c0:c0 + 2 * SSM_P] = st_ref[r0:r0 + SSM_N, c0:c0 + 2 * SSM_P] * dec + acc_s


def _ssd_scan(xs, cm, bt, cols, rows):
    b, n, _ = xs.shape
    nt = n // TOK_TILE

    def tok(w):
        return pl.BlockSpec((1, TOK_TILE, w), lambda d, i, s: (i, _scan_tile(d, s, nt), 0))

    def tr(w):
        return pl.BlockSpec((1, w, TOK_TILE), lambda d, i, s: (i, 0, _scan_tile(d, s, nt)))

    return pl.pallas_call(
        _ssd_kernel,
        grid=(2, b, nt),
        in_specs=[tok(SSM_W), tok(SSM_GROUPS * SSM_N), tr(SSM_GROUPS * SSM_N), tok(SMALL_W), tr(SMALL_W)],
        out_specs=pl.BlockSpec((1, 1, TOK_TILE, SSM_W), lambda d, i, s: (d, i, _scan_tile(d, s, nt), 0)),
        out_shape=jax.ShapeDtypeStruct((2, b, n, SSM_W), jnp.float32),
        scratch_shapes=[pltpu.VMEM((SSM_GROUPS * SSM_N, SSM_W), jnp.float32)],
        compiler_params=pltpu.CompilerParams(dimension_semantics=("parallel", "parallel", "arbitrary")),
        name="ssd_scan",
    )(xs, cm, bt, cols, rows)


def _gdn_kernel(q_ref, k_ref, kt_ref, v_ref, cols_ref, rows_ref, o_ref, s_ref):
    is_f = pl.program_id(0) == 0

    @pl.when(pl.program_id(2) == 0)
    def _():
        s_ref[...] = jnp.zeros_like(s_ref)

    bf16 = jnp.bfloat16
    f32 = jnp.float32
    t = TOK_TILE
    cols = cols_ref[0]
    rows = rows_ref[0]
    incl, strict = _order_masks(is_f)
    lane = lax.broadcasted_iota(jnp.int32, (t, GDN_W), 1)
    sub = lax.broadcasted_iota(jnp.int32, (GDN_W, t), 0)
    head_of_lane = [(lane >= h * GDN_DK) & (lane < (h + 1) * GDN_DK) for h in range(GDN_HEADS)]
    head_of_sub = [(sub >= h * GDN_DK) & (sub < (h + 1) * GDN_DK) for h in range(GDN_HEADS)]

    def by_lane(pieces):
        out = pieces[0]
        for h in range(1, GDN_HEADS):
            out = jnp.where(head_of_lane[h] if pieces[h].shape[0] != 1 else head_of_lane[h][0:1, :], pieces[h], out)
        return out

    q = q_ref[0]
    k = k_ref[0]
    v = v_ref[0]
    kt = kt_ref[0]
    gc_col = [_pick(is_f, cols, L_GCUM, GDN_HEADS, h, 1) for h in range(GDN_HEADS)]
    gc_row = [_pick(is_f, rows, L_GCUM, GDN_HEADS, h, 0) for h in range(GDN_HEADS)]
    gc_end = [jnp.where(is_f, r[:, t - 1:t], r[:, 0:1]) for r in gc_row]
    beta_x = by_lane([_pick(is_f, cols, L_BETA, GDN_HEADS, h, 1) + jnp.zeros((t, GDN_W), f32) for h in range(GDN_HEADS)])
    egc = jnp.exp(by_lane([c + jnp.zeros((t, GDN_W), f32) for c in gc_col]))
    kb = k * beta_x
    vb = v * beta_x
    kbg = kb * egc
    qd = q * egc
    ri = lax.broadcasted_iota(jnp.int32, (t, t), 0)
    ci = lax.broadcasted_iota(jnp.int32, (t, t), 1)
    eye = jnp.where(ri == ci, 1.0, 0.0)
    off_levels = []
    size = 1
    while size < t:
        off_levels.append(((ri // (2 * size)) == (ci // (2 * size))) & ((ri // size) != (ci // size)))
        size *= 2

    a_mats = []
    qkd = []
    for h in range(GDN_HEADS):
        kbm = jnp.where(head_of_lane[h], kb, 0.0).astype(bf16)
        qm = jnp.where(head_of_lane[h], q, 0.0).astype(bf16)
        kk = jnp.dot(kbm, kt, preferred_element_type=f32)
        qk = jnp.dot(qm, kt, preferred_element_type=f32)
        dec = jnp.exp(jnp.where(incl, gc_col[h] - gc_row[h], -jnp.inf))
        a_mats.append(jnp.where(strict, kk * dec, 0.0))
        qkd.append((qk * dec).astype(bf16))
    t_inv = [eye - jnp.where(off_levels[0], a, 0.0) for a in a_mats]
    for lvl_mask in off_levels[1:]:
        tbs = [x.astype(bf16) for x in t_inv]
        es = [jnp.where(lvl_mask, a, 0.0).astype(bf16) for a in a_mats]
        tes = [jnp.dot(tb, e, preferred_element_type=f32).astype(bf16) for tb, e in zip(tbs, es)]
        t_inv = [x - jnp.dot(te, tb, preferred_element_type=f32) for x, te, tb in zip(t_inv, tes, tbs)]
    uw = None
    for h in range(GDN_HEADS):
        rhs = jnp.concatenate([jnp.where(head_of_lane[h], vb, 0.0), jnp.where(head_of_lane[h], kbg, 0.0)], axis=1)
        term = jnp.dot(t_inv[h].astype(bf16), rhs.astype(bf16), preferred_element_type=f32)
        uw = term if uw is None else uw + term

    s_old = s_ref[...]
    sb = s_old.astype(bf16)
    u = uw[:, :GDN_W]
    w = uw[:, GDN_W:]
    v_new = u - jnp.dot(w.astype(bf16), sb, preferred_element_type=f32)
    o = jnp.dot(qd.astype(bf16), sb, preferred_element_type=f32)
    for h in range(GDN_HEADS):
        o = o + jnp.dot(qkd[h], jnp.where(head_of_lane[h], v_new, 0.0).astype(bf16), preferred_element_type=f32)
    o_ref[0, 0] = o

    mult = jnp.exp(gc_end[0] - gc_row[0]) + jnp.zeros((GDN_W, t), f32)
    for h in range(1, GDN_HEADS):
        mult = jnp.where(head_of_sub[h], jnp.exp(gc_end[h] - gc_row[h]), mult)
    kend_t = (kt.astype(f32) * mult).astype(bf16)
    upd = jnp.dot(kend_t, v_new.astype(bf16), preferred_element_type=f32)
    g_end = jnp.exp(by_lane(gc_end))
    same_head = head_of_sub[0] & head_of_lane[0]
    for h in range(1, GDN_HEADS):
        same_head = same_head | (head_of_sub[h] & head_of_lane[h])
    s_ref[...] = jnp.where(same_head, s_old * g_end + upd, 0.0)


def _gdn_scan(q, k, kt, v, cols, rows):
    b, n, _ = q.shape
    nt = n // TOK_TILE

    def tok(w):
        return pl.BlockSpec((1, TOK_TILE, w), lambda d, i, s: (i, _scan_tile(d, s, nt), 0))

    def tr(w):
        return pl.BlockSpec((1, w, TOK_TILE), lambda d, i, s: (i, 0, _scan_tile(d, s, nt)))

    return pl.pallas_call(
        _gdn_kernel,
        grid=(2, b, nt),
        in_specs=[tok(GDN_W), tok(GDN_W), tr(GDN_W), tok(GDN_W), tok(SMALL_W), tr(SMALL_W)],
        out_specs=pl.BlockSpec((1, 1, TOK_TILE, GDN_W), lambda d, i, s: (d, i, _scan_tile(d, s, nt), 0)),
        out_shape=jax.ShapeDtypeStruct((2, b, n, GDN_W), jnp.float32),
        scratch_shapes=[pltpu.VMEM((GDN_W, GDN_W), jnp.float32)],
        compiler_params=pltpu.CompilerParams(dimension_semantics=("parallel", "parallel", "arbitrary")),
        name="gdn_scan",
    )(q, k, kt, v, cols, rows)


def _small_lane_params(ssm_dt_bias, ssm_a_log, gdn_dt_bias, gdn_a_log):
    f32 = jnp.float32
    z = lambda k: jnp.zeros((k,), f32)
    sb = ssm_dt_bias.astype(f32).reshape(-1)
    bias = jnp.concatenate([sb, sb, z(L_GCUM - L_BETA), gdn_dt_bias.astype(f32).reshape(-1),
                            z(SMALL_W - L_GCUM - 2 * GDN_HEADS)])
    scale = jnp.concatenate([z(L_SCUM), -jnp.exp(ssm_a_log.astype(f32)).reshape(-1), z(L_GCUM - L_BETA),
                             -jnp.exp(gdn_a_log.astype(f32)).reshape(-1), z(SMALL_W - L_GCUM - 2 * GDN_HEADS)])
    return bias[None, :], scale[None, :]


def _permuted_w_in(w_in):
    offs = np.cumsum((0,) + SPLIT_SIZES)
    dt0, beta0, a0 = int(offs[3]), int(offs[6]), int(offs[7])
    dt = list(range(dt0, dt0 + SSM_HEADS))
    small = dt + dt + dt + dt + list(range(beta0, beta0 + 2 * GDN_HEADS)) + list(range(a0, a0 + 2 * GDN_HEADS))
    cols = list(range(0, dt0)) + list(range(int(offs[4]), beta0)) + small
    w = w_in[:, np.asarray(cols)]
    return jnp.pad(w, ((0, 0), (0, SMALL_W - len(small))))


def mixers_pallas(xbc, gqkv, small, ssm_conv_w, ssm_conv_b, ssm_dt_bias, ssm_a_log, gdn_conv_w, gdn_dt_bias, gdn_a_log):
    bias_vec, scale_vec = _small_lane_params(ssm_dt_bias, ssm_a_log, gdn_dt_bias, gdn_a_log)
    pad = ((0, 8 - CONV_K), (0, 0))
    xs, cm, bt, q, k, kt, v, cols, rows = _mixer_prep(
        xbc, gqkv, small, jnp.pad(ssm_conv_w, pad), ssm_conv_b[None, :], jnp.pad(gdn_conv_w, pad), bias_vec, scale_vec)
    y = _ssd_scan(xs, cm, bt, cols, rows)
    o = _gdn_scan(q, k, kt, v, cols, rows)
    return xs, y, o


def _expert_kernel(be_ref, nu_ref, x_ref, rg_ref, wg_ref, wu_ref, wd_ref, o_ref, wgb_ref, wub_ref, wdb_ref):
    i = pl.program_id(0)

    @pl.when((i == 0) | (be_ref[i] != be_ref[jnp.maximum(i - 1, 0)]))
    def _():
        wgb_ref[...] = wg_ref[0, 0].astype(jnp.bfloat16)
        wub_ref[...] = wu_ref[0, 0].astype(jnp.bfloat16)
        wdb_ref[...] = wd_ref[0, 0].astype(jnp.bfloat16)

    @pl.when(i < nu_ref[0])
    def _():
        x = x_ref[...]
        g = jnp.dot(x, wgb_ref[...], preferred_element_type=jnp.float32)
        u = jnp.dot(x, wub_ref[...], preferred_element_type=jnp.float32)
        h = (_silu(g) * u).astype(jnp.bfloat16)
        o_ref[...] = (jnp.dot(h, wdb_ref[...], preferred_element_type=jnp.float32) * rg_ref[...]).astype(o_ref.dtype)

    @pl.when(i >= nu_ref[0])
    def _():
        o_ref[...] = jnp.zeros_like(o_ref)


def _expert_blocks(layer, block_e, n_used, buf, row_gate, w_gate, w_up, w_down):
    n_rows, d = buf.shape
    n_blocks = n_rows // MOE_BLOCK
    grid_spec = pltpu.PrefetchScalarGridSpec(
        num_scalar_prefetch=2,
        grid=(n_blocks,),
        in_specs=[pl.BlockSpec((MOE_BLOCK, d), lambda i, be, nu: (i, 0)),
                  pl.BlockSpec((MOE_BLOCK, 1), lambda i, be, nu: (i, 0)),
                  pl.BlockSpec((1, 1, d, D_EXPERT), lambda i, be, nu: (layer, be[i], 0, 0)),
                  pl.BlockSpec((1, 1, d, D_EXPERT), lambda i, be, nu: (layer, be[i], 0, 0)),
                  pl.BlockSpec((1, 1, D_EXPERT, d), lambda i, be, nu: (layer, be[i], 0, 0))],
        out_specs=pl.BlockSpec((MOE_BLOCK, d), lambda i, be, nu: (i, 0)),
        scratch_shapes=[pltpu.VMEM((d, D_EXPERT), jnp.bfloat16), pltpu.VMEM((d, D_EXPERT), jnp.bfloat16),
                        pltpu.VMEM((D_EXPERT, d), jnp.bfloat16)],
    )
    return pl.pallas_call(
        _expert_kernel,
        grid_spec=grid_spec,
        out_shape=jax.ShapeDtypeStruct((n_rows, d), jnp.bfloat16),
        compiler_params=pltpu.CompilerParams(dimension_semantics=("arbitrary",), vmem_limit_bytes=MOE_VMEM_BYTES),
        name="moe_experts",
    )(block_e, n_used, buf, row_gate, w_gate, w_up, w_down)


def routed_experts(layer, t, experts, gates, w_gate, w_up, w_down):
    n, d = t.shape
    n_assign = n * TOP_K
    flat_e = experts.reshape(n_assign)
    order = jnp.argsort(flat_e).astype(jnp.int32)
    slot = jnp.argsort(order).astype(jnp.int32)
    counts = jax.ops.segment_sum(jnp.ones((n_assign,), jnp.int32), flat_e, num_segments=N_EXPERTS)
    starts = jnp.cumsum(counts) - counts
    padded = (counts + MOE_BLOCK - 1) // MOE_BLOCK * MOE_BLOCK
    pad_ends = jnp.cumsum(padded)
    pad_starts = pad_ends - padded
    n_blocks = -(-n_assign // MOE_BLOCK) + N_EXPERTS
    block_start = jnp.arange(n_blocks, dtype=jnp.int32) * MOE_BLOCK
    block_e = jnp.minimum(jnp.sum(block_start[:, None] >= pad_ends[None, :], axis=1), N_EXPERTS - 1).astype(jnp.int32)
    n_used = (pad_ends[-1] // MOE_BLOCK).astype(jnp.int32).reshape(1)
    within = (block_start - pad_starts[block_e])[:, None] + jnp.arange(MOE_BLOCK, dtype=jnp.int32)[None, :]
    valid = (within < counts[block_e][:, None]).reshape(-1)
    src_slot = jnp.minimum(starts[block_e][:, None] + within, n_assign - 1).reshape(-1)
    src_assign = order[src_slot]
    src_tok = jnp.where(valid, src_assign // TOP_K, 0)
    row_gate = jnp.where(valid, gates.reshape(n_assign).astype(jnp.float32)[src_assign], 0.0)[:, None]
    out = _expert_blocks(layer, block_e, n_used, t[src_tok], row_gate, w_gate, w_up, w_down)
    pos = (pad_starts[flat_e] + slot - starts[flat_e]).reshape(n, TOP_K)
    y = out[pos[:, 0]].astype(jnp.float32)
    for k in range(1, TOP_K):
        y = y + out[pos[:, k]].astype(jnp.float32)
    return y


def hier_moe(layer, t, t_rows, router_g_w, router_g_b, router_e_w, router_e_b, w_gate, w_up, w_down):
    n = t.shape[0]
    grp_prob = jax.nn.softmax(jnp.dot(t, router_g_w, preferred_element_type=jnp.float32)
                              + router_g_b.astype(jnp.float32), axis=-1)
    grp = jnp.argmax(grp_prob, axis=-1).astype(jnp.int32)
    p_grp = jnp.max(grp_prob, axis=-1)
    e_logits = (jnp.dot(t, router_e_w, preferred_element_type=jnp.float32)
                + router_e_b.astype(jnp.float32)).reshape(n, N_EGROUPS, EXPERTS_PER_GROUP)
    in_grp = grp[:, None, None] == jnp.arange(N_EGROUPS, dtype=jnp.int32)[None, :, None]
    probs = jax.nn.softmax(jnp.sum(jnp.where(in_grp, e_logits, 0.0), axis=1), axis=-1)
    lanes = jnp.arange(EXPERTS_PER_GROUP, dtype=jnp.int32)[None, :]
    idx, p_top = [], []
    for _ in range(TOP_K):
        i = jnp.argmax(probs, axis=-1).astype(jnp.int32)
        idx.append(i)
        p_top.append(jnp.max(probs, axis=-1))
        probs = jnp.where(lanes == i[:, None], -jnp.inf, probs)
    idx = jnp.stack(idx, axis=-1)
    p_top = jnp.stack(p_top, axis=-1)
    gates = p_grp[:, None] * p_top / jnp.sum(p_top, axis=-1, keepdims=True)
    experts = grp[:, None] * EXPERTS_PER_GROUP + idx
    return routed_experts(layer, t_rows, experts, gates, w_gate, w_up, w_down)


def _matmul_kernel(a_ref, b_ref, o_ref):
    o_ref[...] = jnp.dot(a_ref[...].astype(jnp.bfloat16), b_ref[...],
                         preferred_element_type=jnp.float32)


def _matmul(a, b, tm, tn):
    m, k = a.shape
    n = b.shape[1]
    return pl.pallas_call(
        _matmul_kernel,
        grid=(m // tm, n // tn),
        in_specs=[pl.BlockSpec((tm, k), lambda i, j: (i, 0)),
                  pl.BlockSpec((k, tn), lambda i, j: (0, j))],
        out_specs=pl.BlockSpec((tm, tn), lambda i, j: (i, j)),
        out_shape=jax.ShapeDtypeStruct((m, n), jnp.float32),
        compiler_params=pltpu.CompilerParams(dimension_semantics=("parallel", "parallel")),
        name="dense",
    )(a, b)


def _dense(a, w, tm=512):
    lead = a.shape[:-1]
    k = a.shape[-1]
    n = w.shape[1]
    a2 = a.reshape(-1, k)
    m = a2.shape[0]
    m_pad = -(-m // 8) * 8
    if m_pad != m:
        a2 = jnp.pad(a2, ((0, m_pad - m), (0, 0)))
    tm = min(tm, m_pad)
    n_pad = -(-n // 128) * 128
    tn = next(c for c in (1024, 768, 640, 512, 384, 256, 128) if n_pad % c == 0)
    wb = jnp.pad(w, ((0, 0), (0, n_pad - n))).astype(jnp.bfloat16)
    out = _matmul(a2, wb, tm, tn)[:m, :n]
    return out.reshape(lead + (n,))


IN_SLABS = (3 * DIFF_W, SSM_W, SSM_XBC, GDN_QKV, GDN_W, SMALL_W)
PROJ_VMEM_BYTES = 44 * 1024 * 1024


def _mod_row_spec(bsz):
    return pl.BlockSpec((1, MOD_CHUNKS, D_MODEL), lambda i, j: (jnp.where(j == 0, bsz, i), 0, 0))


def _in_proj_kernel(x_ref, nw_ref, mod_ref, w_ref, *out_refs):
    x = x_ref[0]
    xn = x * lax.rsqrt(jnp.mean(x * x, axis=-1, keepdims=True) + EPS) * nw_ref[...]
    mod = mod_ref[0]
    xm = (xn * (1.0 + mod[1:2, :]) + mod[0:1, :]).astype(jnp.bfloat16)
    c0 = 0
    for o_ref, width in zip(out_refs, IN_SLABS):
        o_ref[0] = jnp.dot(xm, w_ref[:, c0:c0 + width], preferred_element_type=jnp.float32)
        c0 += width


def _in_proj(x, norm_w, mod, w):
    b, n, d = x.shape
    tok = lambda width: pl.BlockSpec((1, TOK_TILE, width), lambda i, j: (i, j, 0))
    return pl.pallas_call(
        _in_proj_kernel,
        grid=(b, n // TOK_TILE),
        in_specs=[tok(d), pl.BlockSpec((1, d), lambda i, j: (0, 0)), _mod_row_spec(b),
                  pl.BlockSpec(w.shape, lambda i, j: (0, 0))],
        out_specs=[tok(width) for width in IN_SLABS],
        out_shape=[jax.ShapeDtypeStruct((b, n, width), jnp.float32) for width in IN_SLABS],
        compiler_params=pltpu.CompilerParams(dimension_semantics=("parallel", "parallel"),
                                             vmem_limit_bytes=PROJ_VMEM_BYTES),
        name="in_proj",
    )(x, norm_w, mod, w)


def _out_proj_kernel(att_ref, y0_ref, y1_ref, xs_ref, z_ref, o0_ref, o1_ref, gate_ref, x_ref, mod_ref,
                     dvec_ref, snw_ref, gnw_ref, n2w_ref, g_ref, w_ref, xo_ref, f_ref, fb_ref):
    ys = (y0_ref[0, 0] + y1_ref[0, 0] + dvec_ref[...] * xs_ref[0]) * _silu(z_ref[0])
    gw = SSM_W // SSM_GROUPS
    parts = [att_ref[0]]
    for g in range(SSM_GROUPS):
        yg = ys[:, g * gw:(g + 1) * gw]
        parts.append(yg * lax.rsqrt(jnp.mean(yg * yg, axis=-1, keepdims=True) + EPS) * snw_ref[:, g * gw:(g + 1) * gw])
    o = o0_ref[0, 0] + o1_ref[0, 0]
    parts.append(o * lax.rsqrt(_group_sum_sq(o, g_ref) + EPS) * gnw_ref[...] * _silu(gate_ref[0]))
    ml = jnp.concatenate(parts, axis=1).astype(jnp.bfloat16)
    mod = mod_ref[0]
    xn = x_ref[0] + mod[2:3, :] * jnp.dot(ml, w_ref[...], preferred_element_type=jnp.float32)
    xo_ref[0] = xn
    f = xn * lax.rsqrt(jnp.mean(xn * xn, axis=-1, keepdims=True) + EPS) * n2w_ref[...]
    f = f * (1.0 + mod[4:5, :]) + mod[3:4, :]
    f_ref[0] = f
    fb_ref[0] = f.astype(jnp.bfloat16)


def _out_proj(att, y, xs, z, o, gate, x, mod, dvec, snw, gnw, n2w, g64, w, latent_only):
    b, n, d = x.shape
    t0 = 1 if latent_only else 0
    n_out = n - t0 * TOK_TILE
    tok = lambda width: pl.BlockSpec((1, TOK_TILE, width), lambda i, j: (i, j + t0, 0))
    dirs = lambda width, k: pl.BlockSpec((1, 1, TOK_TILE, width), lambda i, j: (k, i, j + t0, 0))
    vec = lambda width: pl.BlockSpec((1, width), lambda i, j: (0, 0))
    out_tok = pl.BlockSpec((1, TOK_TILE, d), lambda i, j: (i, j, 0))
    mod_spec = pl.BlockSpec((1, MOD_CHUNKS, D_MODEL), lambda i, j: (i, 0, 0)) if latent_only else _mod_row_spec(b)
    return pl.pallas_call(
        _out_proj_kernel,
        grid=(b, n_out // TOK_TILE),
        in_specs=[tok(DIFF_W), dirs(SSM_W, 0), dirs(SSM_W, 1), tok(SSM_W), tok(SSM_W), dirs(GDN_W, 0), dirs(GDN_W, 1),
                  tok(GDN_W), tok(d), mod_spec, vec(SSM_W), vec(SSM_W), vec(GDN_W), vec(d),
                  pl.BlockSpec(g64.shape, lambda i, j: (0, 0)), pl.BlockSpec(w.shape, lambda i, j: (0, 0))],
        out_specs=[out_tok, out_tok, out_tok],
        out_shape=[jax.ShapeDtypeStruct((b, n_out, d), jnp.float32), jax.ShapeDtypeStruct((b, n_out, d), jnp.float32),
                   jax.ShapeDtypeStruct((b, n_out, d), jnp.bfloat16)],
        compiler_params=pltpu.CompilerParams(dimension_semantics=("parallel", "parallel"),
                                             vmem_limit_bytes=PROJ_VMEM_BYTES),
        name="out_proj",
    )(att, y, y, xs, z, o, o, gate, x, mod, dvec, snw, gnw, n2w, g64, w)


def kernel(x, c, ctx, c_ctx, w_mod, b_mod, norm1_w, norm2_w, w_in, w_out,
           diff_qn_w, diff_kn_w, diff_lq1, diff_lk1, diff_lq2, diff_lk2, diff_norm_w,
           ssm_conv_w, ssm_conv_b, ssm_dt_bias, ssm_a_log, ssm_d, ssm_norm_w,
           gdn_conv_w, gdn_dt_bias, gdn_a_log, gdn_norm_w,
           router_g_w, router_g_b, router_e_w, router_e_b, exp_w_gate, exp_w_up, exp_w_down):
    assert ctx.shape[1] == CTX_LEN == TOK_TILE and x.shape[1] % TOK_TILE == 0
    bsz = x.shape[0]
    xs_all = jnp.concatenate([ctx, x], axis=1)
    n_tok = xs_all.shape[1]
    is_lat = (jnp.arange(n_tok) >= CTX_LEN)[None, :, None]
    cc = jnp.concatenate([c, c_ctx[None, :]], axis=0)
    bf16 = jnp.bfloat16
    g64 = _group_matrix(GDN_W, GDN_DV, 1.0 / GDN_DV)
    for l in range(DEPTH):
        last = l == DEPTH - 1
        lam_init = 0.8 - 0.6 * math.exp(-0.3 * l)
        mod = (_dense(jax.nn.silu(cc), w_mod[l]) + b_mod[l]).reshape(bsz + 1, MOD_CHUNKS, D_MODEL)
        qkv, z, xbc, gqkv, gate, small = _in_proj(xs_all, norm1_w[l][None, :], mod, _permuted_w_in(w_in[l]).astype(bf16))
        att = diff_attention_pallas(qkv, lam_init, diff_qn_w[l], diff_kn_w[l],
                                    diff_lq1[l], diff_lk1[l], diff_lq2[l], diff_lk2[l], diff_norm_w[l])
        xs, y, o = mixers_pallas(xbc, gqkv, small, ssm_conv_w[l], ssm_conv_b[l],
                                 ssm_dt_bias[l], ssm_a_log[l], gdn_conv_w[l], gdn_dt_bias[l], gdn_a_log[l])
        xs_all, f, fb = _out_proj(att, y, xs, z, o, gate, xs_all, mod, jnp.repeat(ssm_d[l], SSM_P)[None, :],
                                  ssm_norm_w[l][None, :], jnp.tile(gdn_norm_w[l], GDN_HEADS)[None, :],
                                  norm2_w[l][None, :], g64, w_out[l].astype(bf16), latent_only=last)
        moe_w = (router_g_w[l], router_g_b[l], router_e_w[l], router_e_b[l], exp_w_gate, exp_w_up, exp_w_down)
        if last:
            yl = hier_moe(l, f.reshape(-1, D_MODEL), fb.reshape(-1, D_MODEL), *moe_w)
            return xs_all + mod[:bsz, None, 5, :] * yl.reshape(xs_all.shape)
        ym = hier_moe(l, f.reshape(-1, D_MODEL), fb.reshape(-1, D_MODEL), *moe_w)
        mod5 = jnp.where(is_lat, mod[:bsz, None, 5, :], mod[bsz, 5, :])
        xs_all = xs_all + mod5 * ym.reshape(xs_all.shape)
```

```python
import functools
import math
import jax
import jax.numpy as jnp
from jax import lax
import numpy as np
from jax.experimental import pallas as pl
from jax.experimental.pallas import tpu as pltpu

D_MODEL = 1024
DEPTH = 2
CTX_LEN = 256
GRID_W = 64
EPS = 1e-6
MOD_CHUNKS = 6

DIFF_HEADS = 4
DIFF_QK = 32
DIFF_V = 2 * DIFF_QK
DIFF_W = DIFF_HEADS * DIFF_V
ROPE_BASE = 10000.0

SSM_HEADS = 8
SSM_P = 64
SSM_GROUPS = 2
SSM_N = 64
SSM_W = SSM_HEADS * SSM_P
SSM_XBC = SSM_W + 2 * SSM_GROUPS * SSM_N
CONV_K = 5

GDN_HEADS = 4
GDN_DK = 64
GDN_DV = 64
GDN_QKV = GDN_HEADS * (2 * GDN_DK + GDN_DV)
GDN_W = GDN_HEADS * GDN_DV

D_MIX = DIFF_W + SSM_W + GDN_W
SPLIT_SIZES = (3 * DIFF_W, SSM_W, SSM_XBC, SSM_HEADS, GDN_QKV, GDN_W, 2 * GDN_HEADS, 2 * GDN_HEADS)

N_EGROUPS = 4
EXPERTS_PER_GROUP = 8
N_EXPERTS = N_EGROUPS * EXPERTS_PER_GROUP
TOP_K = 2
D_EXPERT = 512
MOE_BLOCK = 512

TOK_TILE = 256
LOG2E = 1.4426950408889634
EXP_ROWS = 64
VT_ROWS = DIFF_W + 16
ATTN_VMEM_BYTES = 52 * 1024 * 1024
MOE_VMEM_BYTES = 40 * 1024 * 1024
HALO = 8
SMALL_W = 128
L_DT = 0
L_SCUM = 16
L_BETA = 32
L_GCUM = 40


def rmsnorm(x, w):
    x32 = x.astype(jnp.float32)
    y = x32 * lax.rsqrt(jnp.mean(x32 * x32, axis=-1, keepdims=True) + EPS)
    return y.astype(x.dtype) * w


def modulate(x, shift, scale):
    return x * (1.0 + scale) + shift


def _bf16_terms(x, n):
    out = []
    for _ in range(n):
        t = x.astype(jnp.bfloat16)
        out.append(t)
        x = x - t.astype(jnp.float32)
    return out


def _group_sum_sq(x, g_ref):
    hi, lo = _bf16_terms(x * x, 2)
    g = g_ref[...]
    return (jnp.dot(hi, g, preferred_element_type=jnp.float32)
            + jnp.dot(lo, g, preferred_element_type=jnp.float32))


def _group_matrix(width, group, value):
    idx = np.arange(width) // group
    return jnp.asarray((idx[:, None] == idx[None, :]).astype(np.float32) * value, jnp.bfloat16)


def _attn_prep_kernel(qkv_ref, cos_ref, sin_ref, wq_ref, wk_ref, g_ref, qt_out, k_out, vt_out):
    x = qkv_ref[0]
    cos = cos_ref[...]
    sin = sin_ref[...]
    lane = lax.broadcasted_iota(jnp.int32, cos.shape, 1)
    lo_half = (lane % 16) < 8

    def norm_rope(t, w):
        y = t * lax.rsqrt(_group_sum_sq(t, g_ref) + EPS) * w
        rot = jnp.where(lo_half, pltpu.roll(y, DIFF_W - 8, 1), pltpu.roll(y, 8, 1))
        return y * cos + rot * sin

    q = norm_rope(x[:, 0:DIFF_W], wq_ref[...]) * (DIFF_QK ** -0.5 * LOG2E)
    qt_out[0] = q.T
    k_out[0] = norm_rope(x[:, DIFF_W:2 * DIFF_W], wk_ref[...]).astype(jnp.bfloat16)
    vt_out[0, :DIFF_W, :] = x[:, 2 * DIFF_W:3 * DIFF_W].T.astype(jnp.bfloat16)
    vt_out[0, DIFF_W:, :] = jnp.ones((VT_ROWS - DIFF_W, TOK_TILE), jnp.bfloat16)


def _attn_prep(qkv, cos_t, sin_t, wq, wk, gmat):
    b, n, _ = qkv.shape
    nt = n // TOK_TILE
    tok = pl.BlockSpec((1, TOK_TILE, DIFF_W), lambda i, j: (i, j, 0))
    tr = pl.BlockSpec((1, DIFF_W, TOK_TILE), lambda i, j: (i, 0, j))
    return pl.pallas_call(
        _attn_prep_kernel,
        grid=(b, nt),
        in_specs=[pl.BlockSpec((1, TOK_TILE, 3 * DIFF_W), lambda i, j: (i, j, 0)),
                  pl.BlockSpec((TOK_TILE, DIFF_W), lambda i, j: (j, 0)),
                  pl.BlockSpec((TOK_TILE, DIFF_W), lambda i, j: (j, 0)),
                  pl.BlockSpec((1, DIFF_W), lambda i, j: (0, 0)),
                  pl.BlockSpec((1, DIFF_W), lambda i, j: (0, 0)),
                  pl.BlockSpec((DIFF_W, DIFF_W), lambda i, j: (0, 0))],
        out_specs=[tr, tok, pl.BlockSpec((1, VT_ROWS, TOK_TILE), lambda i, j: (i, 0, j))],
        out_shape=[jax.ShapeDtypeStruct((b, DIFF_W, n), jnp.float32),
                   jax.ShapeDtypeStruct((b, n, DIFF_W), jnp.bfloat16),
                   jax.ShapeDtypeStruct((b, VT_ROWS, n), jnp.bfloat16)],
        compiler_params=pltpu.CompilerParams(dimension_semantics=("parallel", "parallel")),
        name="attn_prep",
    )(qkv, cos_t, sin_t, wq, wk, gmat)


def _attn_kernel(lam_ref, qt_ref, k_ref, vt_ref, wo_ref, g_ref, o_ref, s_ref, p_ref, *, n_keys, out_scale):
    tile = pl.program_id(1)
    lam = lam_ref[0]
    sub = lax.broadcasted_iota(jnp.int32, (DIFF_W, TOK_TILE), 0)

    def attend(nk):
        n_chunks = nk // TOK_TILE
        qt = qt_ref[0]

        def masked_qt(head):
            return jnp.concatenate(
                [jnp.where((sub >= (2 * head + mp) * DIFF_QK) & (sub < (2 * head + mp + 1) * DIFF_QK), qt, 0.0)
                 for mp in range(2)], axis=1).astype(jnp.bfloat16)

        def score_chunk(qt2, buf, c, mx8):
            s = jnp.dot(k_ref[0, c * TOK_TILE:(c + 1) * TOK_TILE, :], qt2, preferred_element_type=jnp.float32)
            s_ref[buf, c * TOK_TILE:(c + 1) * TOK_TILE, :] = s
            cm = jnp.max(s.reshape(TOK_TILE // 8, 8, 2 * TOK_TILE), axis=0)
            return cm if mx8 is None else jnp.maximum(mx8, cm)

        mx8 = None
        qt2 = masked_qt(0)
        for c in range(n_chunks):
            mx8 = score_chunk(qt2, 0, c, mx8)
        acc = None
        for head in range(DIFF_HEADS):
            buf = head % 2
            mx = jnp.max(mx8, axis=0, keepdims=True)
            nxt = head + 1 < DIFF_HEADS
            if nxt:
                qt2 = masked_qt(head + 1)
            mx8 = None
            for c in range(n_chunks):
                if nxt:
                    mx8 = score_chunk(qt2, 1 - buf, c, mx8)
                for r in range(c * TOK_TILE // EXP_ROWS, (c + 1) * TOK_TILE // EXP_ROWS):
                    e = jnp.exp2(s_ref[buf, r * EXP_ROWS:(r + 1) * EXP_ROWS, :] - mx)
                    p_ref[r * EXP_ROWS:(r + 1) * EXP_ROWS, :] = e.astype(jnp.bfloat16)
            ov = jnp.dot(vt_ref[0, :, :nk], p_ref[:nk, :], preferred_element_type=jnp.float32)
            ot = ov[:DIFF_W] / ov[DIFF_W:DIFF_W + 1]
            in_head = (sub >= head * DIFF_V) & (sub < (head + 1) * DIFF_V)
            part = jnp.where(in_head, ot[:, :TOK_TILE] - lam * ot[:, TOK_TILE:], 0.0)
            acc = part if acc is None else acc + part
        o = acc.T
        y = o * lax.rsqrt(_group_sum_sq(o, g_ref) + EPS)
        o_ref[0] = y * wo_ref[...] * out_scale

    @pl.when(tile == 0)
    def _():
        attend(TOK_TILE)

    @pl.when(tile > 0)
    def _():
        attend(n_keys)


def _attention(lam, qt, k, vt, wo, gmat, out_scale):
    b, n, _ = k.shape
    nt = n // TOK_TILE
    kern = functools.partial(_attn_kernel, n_keys=n, out_scale=out_scale)
    return pl.pallas_call(
        kern,
        grid=(b, nt),
        in_specs=[pl.BlockSpec(memory_space=pltpu.SMEM),
                  pl.BlockSpec((1, DIFF_W, TOK_TILE), lambda i, j: (i, 0, j)),
                  pl.BlockSpec((1, n, DIFF_W), lambda i, j: (i, 0, 0)),
                  pl.BlockSpec((1, VT_ROWS, n), lambda i, j: (i, 0, 0)),
                  pl.BlockSpec((1, DIFF_W), lambda i, j: (0, 0)),
                  pl.BlockSpec((DIFF_W, DIFF_W), lambda i, j: (0, 0))],
        out_specs=pl.BlockSpec((1, TOK_TILE, DIFF_W), lambda i, j: (i, j, 0)),
        out_shape=jax.ShapeDtypeStruct((b, n, DIFF_W), jnp.float32),
        scratch_shapes=[pltpu.VMEM((2, n, 2 * TOK_TILE), jnp.float32),
                        pltpu.VMEM((n, 2 * TOK_TILE), jnp.bfloat16)],
        compiler_params=pltpu.CompilerParams(dimension_semantics=("parallel", "arbitrary"),
                                             vmem_limit_bytes=ATTN_VMEM_BYTES),
        name="diff_attn",
    )(lam, qt, k, vt, wo, gmat)


def _rope_tables(n_latent):
    rows = n_latent // GRID_W
    row = np.repeat(np.arange(rows, dtype=np.float32), GRID_W)
    col = np.tile(np.arange(GRID_W, dtype=np.float32), rows)
    half = DIFF_QK // 2
    inv = (ROPE_BASE ** (-np.arange(0, half, 2, dtype=np.float32) / half)).astype(np.float32)
    ang = np.concatenate([row[:, None] * inv, row[:, None] * inv, col[:, None] * inv, col[:, None] * inv], axis=-1)
    cos = np.concatenate([np.ones((TOK_TILE, DIFF_QK), np.float32), np.cos(ang)], axis=0)
    sin = np.concatenate([np.zeros((TOK_TILE, DIFF_QK), np.float32), np.sin(ang)], axis=0)
    sign = np.where((np.arange(DIFF_QK) % 16) < 8, -1.0, 1.0).astype(np.float32)
    reps = DIFF_W // DIFF_QK
    return jnp.asarray(np.tile(cos, (1, reps))), jnp.asarray(np.tile(sin * sign, (1, reps)))


def diff_attention_pallas(qkv, lam_init, qn_w, kn_w, lq1, lk1, lq2, lk2, out_w):
    n = qkv.shape[1]
    cos_t, sin_t = _rope_tables(n - TOK_TILE)
    g32 = _group_matrix(DIFF_W, DIFF_QK, 1.0 / DIFF_QK)
    g64 = _group_matrix(DIFF_W, DIFF_V, 1.0 / DIFF_V)
    reps = DIFF_W // DIFF_QK
    qt, k, vt = _attn_prep(qkv, cos_t, sin_t, jnp.tile(qn_w, reps)[None, :], jnp.tile(kn_w, reps)[None, :], g32)
    lam = (jnp.exp(jnp.sum(lq1 * lk1)) - jnp.exp(jnp.sum(lq2 * lk2)) + lam_init).reshape(1).astype(jnp.float32)
    return _attention(lam, qt, k, vt, jnp.tile(out_w, DIFF_HEADS)[None, :], g64, 1.0 - lam_init)


def _dwconv5(cur_ref, prev_ref, next_ref, w_ref, ext_ref, has_prev, has_next):
    t = TOK_TILE
    ext_ref[0:HALO, :] = jnp.where(has_prev, prev_ref[0], 0.0)
    ext_ref[HALO:HALO + t, :] = cur_ref[0]
    ext_ref[HALO + t:2 * HALO + t, :] = jnp.where(has_next, next_ref[0], 0.0)
    acc = None
    for j in range(CONV_K):
        term = ext_ref[pl.ds(HALO - CONV_K // 2 + j, t), :] * w_ref[j:j + 1, :]
        acc = term if acc is None else acc + term
    return acc


def _silu(x):
    return x * jax.nn.sigmoid(x)


def _mixer_prep_kernel(xbc_ref, xbc_p, xbc_n, gq_ref, gq_p, gq_n, sm_ref, sw_ref, sb_ref, gw_ref,
                       bias_ref, scale_ref, ones_ref,
                       xs_out, cm_out, bt_out, q_out, k_out, kt_out, v_out, cols_out, rows_out, ext_ref):
    tile = pl.program_id(1)
    nt = pl.num_programs(1)
    has_prev = tile >= 2
    has_next = (tile >= 1) & (tile < nt - 1)

    u = _silu(_dwconv5(xbc_ref, xbc_p, xbc_n, sw_ref, ext_ref, has_prev, has_next) + sb_ref[...])
    xs_out[0] = u[:, :SSM_W]
    bt_out[0] = u[:, SSM_W:SSM_W + SSM_GROUPS * SSM_N].T.astype(jnp.bfloat16)
    cm_out[0] = u[:, SSM_W + SSM_GROUPS * SSM_N:]

    g = _silu(_dwconv5(gq_ref, gq_p, gq_n, gw_ref, ext_ref, has_prev, has_next))

    def l2n(t):
        return t * lax.rsqrt(_group_sum_sq(t, ones_ref) + EPS)

    q_out[0] = l2n(g[:, :GDN_W]) * (GDN_DK ** -0.5)
    k = l2n(g[:, GDN_W:2 * GDN_W])
    k_out[0] = k
    kt_out[0] = k.T.astype(jnp.bfloat16)
    v_out[0] = g[:, 2 * GDN_W:]

    sm = sm_ref[0] + bias_ref[...]
    sp = jnp.maximum(sm, 0.0) + jnp.log1p(jnp.exp(-jnp.abs(sm)))
    sg = jax.nn.sigmoid(sm)
    vals = sp * scale_ref[...]
    ri = lax.broadcasted_iota(jnp.int32, (TOK_TILE, TOK_TILE), 0)
    ci = lax.broadcasted_iota(jnp.int32, (TOK_TILE, TOK_TILE), 1)
    tri_pre = jnp.where(ri >= ci, 1.0, 0.0).astype(jnp.bfloat16)
    tri_suf = jnp.where(ri <= ci, 1.0, 0.0).astype(jnp.bfloat16)
    pre = None
    suf = None
    for term in _bf16_terms(vals, 3):
        a = jnp.dot(tri_pre, term, preferred_element_type=jnp.float32)
        b = jnp.dot(tri_suf, term, preferred_element_type=jnp.float32)
        pre = a if pre is None else pre + a
        suf = b if suf is None else suf + b
    lane = lax.broadcasted_iota(jnp.int32, (TOK_TILE, SMALL_W), 1)
    bwd_lane = ((lane >= L_SCUM + SSM_HEADS) & (lane < L_BETA)) | (lane >= L_GCUM + GDN_HEADS)
    cum = jnp.where(bwd_lane, suf, pre)
    cols = jnp.where(lane < L_SCUM, sp, jnp.where((lane >= L_BETA) & (lane < L_GCUM), sg, cum))
    cols_out[0] = cols
    rows_out[0] = cols.T


def _mixer_prep(xbc, gqkv, small, ssm_w, ssm_b, gdn_w, bias_vec, scale_vec):
    b, n, _ = xbc.shape
    nt = n // TOK_TILE
    hb = TOK_TILE // HALO
    last = n // HALO - 1
    ones_blk = _group_matrix(GDN_W, GDN_DK, 1.0)

    def cur(w):
        return pl.BlockSpec((1, TOK_TILE, w), lambda i, j: (i, j, 0))

    def prev(w):
        return pl.BlockSpec((1, HALO, w), lambda i, j: (i, jnp.maximum(j * hb - 1, 0), 0))

    def nxt(w):
        return pl.BlockSpec((1, HALO, w), lambda i, j: (i, jnp.minimum((j + 1) * hb, last), 0))

    def const(shape):
        return pl.BlockSpec(shape, lambda i, j: (0,) * len(shape))

    f32 = jnp.float32
    return pl.pallas_call(
        _mixer_prep_kernel,
        grid=(b, nt),
        in_specs=[cur(SSM_XBC), prev(SSM_XBC), nxt(SSM_XBC), cur(GDN_QKV), prev(GDN_QKV), nxt(GDN_QKV), cur(SMALL_W),
                  const((8, SSM_XBC)), const((1, SSM_XBC)), const((8, GDN_QKV)),
                  const((1, SMALL_W)), const((1, SMALL_W)), const((GDN_W, GDN_W))],
        out_specs=[cur(SSM_W), cur(SSM_GROUPS * SSM_N),
                   pl.BlockSpec((1, SSM_GROUPS * SSM_N, TOK_TILE), lambda i, j: (i, 0, j)),
                   cur(GDN_W), cur(GDN_W),
                   pl.BlockSpec((1, GDN_W, TOK_TILE), lambda i, j: (i, 0, j)),
                   cur(GDN_W), cur(SMALL_W),
                   pl.BlockSpec((1, SMALL_W, TOK_TILE), lambda i, j: (i, 0, j))],
        out_shape=[jax.ShapeDtypeStruct((b, n, SSM_W), f32),
                   jax.ShapeDtypeStruct((b, n, SSM_GROUPS * SSM_N), f32),
                   jax.ShapeDtypeStruct((b, SSM_GROUPS * SSM_N, n), jnp.bfloat16),
                   jax.ShapeDtypeStruct((b, n, GDN_W), f32),
                   jax.ShapeDtypeStruct((b, n, GDN_W), f32),
                   jax.ShapeDtypeStruct((b, GDN_W, n), jnp.bfloat16),
                   jax.ShapeDtypeStruct((b, n, GDN_W), f32),
                   jax.ShapeDtypeStruct((b, n, SMALL_W), f32),
                   jax.ShapeDtypeStruct((b, SMALL_W, n), f32)],
        scratch_shapes=[pltpu.VMEM((TOK_TILE + 2 * HALO, SSM_XBC), f32)],
        compiler_params=pltpu.CompilerParams(dimension_semantics=("parallel", "parallel")),
        name="mixer_prep",
    )(xbc, xbc, xbc, gqkv, gqkv, gqkv, small, ssm_w, ssm_b, gdn_w, bias_vec, scale_vec, ones_blk)


def _scan_tile(d, s, nt):
    return jnp.where(d == 0, s, jnp.where(s == 0, 0, nt - s))


def _pick(is_f, arr, base, stride, i, axis):
    a, b = base + i, base + stride + i
    if axis == 1:
        return jnp.where(is_f, arr[:, a:a + 1], arr[:, b:b + 1])
    return jnp.where(is_f, arr[a:a + 1, :], arr[b:b + 1, :])


def _order_masks(is_f):
    ri = lax.broadcasted_iota(jnp.int32, (TOK_TILE, TOK_TILE), 0)
    ci = lax.broadcasted_iota(jnp.int32, (TOK_TILE, TOK_TILE), 1)
    diff = (ri - ci) * jnp.where(is_f, 1, -1)
    return diff >= 0, diff > 0


def _ssd_kernel(xs_ref, cm_ref, bt_ref, cols_ref, rows_ref, y_ref, st_ref):
    is_f = pl.program_id(0) == 0

    @pl.when(pl.program_id(2) == 0)
    def _():
        st_ref[...] = jnp.zeros_like(st_ref)

    bf16 = jnp.bfloat16
    f32 = jnp.float32
    t = TOK_TILE
    cols = cols_ref[0]
    rows = rows_ref[0]
    incl, _ = _order_masks(is_f)
    lane128 = lax.broadcasted_iota(jnp.int32, (t, 128), 1)
    cm = cm_ref[0]
    bt = bt_ref[0]
    hpg = SSM_HEADS // SSM_GROUPS
    for g in range(SSM_GROUPS):
        cg = jnp.where((lane128 >= g * SSM_N) & (lane128 < (g + 1) * SSM_N), cm, 0.0).astype(bf16)
        cb = jnp.dot(cg, bt, preferred_element_type=f32)
        st_g = st_ref[:, g * hpg * SSM_P:(g + 1) * hpg * SSM_P]
        yoff = jnp.dot(cg, st_g.astype(bf16), preferred_element_type=f32)
        btg = bt[g * SSM_N:(g + 1) * SSM_N, :].astype(f32)
        for pair in range(hpg // 2):
            xp = xs_ref[0, :, (g * hpg + 2 * pair) * SSM_P:(g * hpg + 2 * pair + 2) * SSM_P]
            acc_y = None
            acc_s = None
            e_col = None
            dec = None
            for half in range(2):
                hd = g * hpg + 2 * pair + half
                cum_col = _pick(is_f, cols, L_SCUM, SSM_HEADS, hd, 1)
                cum_row = _pick(is_f, rows, L_SCUM, SSM_HEADS, hd, 0)
                dt_col = _pick(is_f, cols, L_DT, SSM_HEADS, hd, 1)
                cum_tot = jnp.where(is_f, cum_row[:, t - 1:t], cum_row[:, 0:1])
                lm = jnp.exp(jnp.where(incl, cum_col - cum_row, -jnp.inf))
                m = (cb * lm).astype(bf16)
                in_half = (lane128 >= half * SSM_P) & (lane128 < (half + 1) * SSM_P)
                xm = jnp.where(in_half, xp * dt_col, 0.0).astype(bf16)
                ty = jnp.dot(m, xm, preferred_element_type=f32)
                ts = jnp.dot((btg * jnp.exp(cum_tot - cum_row)).astype(bf16), xm, preferred_element_type=f32)
                acc_y = ty if acc_y is None else acc_y + ty
                acc_s = ts if acc_s is None else acc_s + ts
                ec = jnp.exp(cum_col)
                dc = jnp.exp(cum_tot)
                e_col = ec if e_col is None else jnp.where(in_half, ec, e_col)
                dec = dc if dec is None else jnp.where(in_half[0:1, :], dc, dec)
            lo = pair * 2 * SSM_P
            c0 = g * hpg * SSM_P + lo
            y_ref[0, 0, :, c0:c0 + 2 * SSM_P] = acc_y + yoff[:, lo:lo + 2 * SSM_P] * e_col
            r0 = g * SSM_N
            st_ref[r0:r0 + SSM_N, c0:c0 + 2 * SSM_P] = st_ref[r0:r0 + SSM_N, c0:c0 + 2 * SSM_P] * dec + acc_s


def _ssd_scan(xs, cm, bt, cols, rows):
    b, n, _ = xs.shape
    nt = n // TOK_TILE

    def tok(w):
        return pl.BlockSpec((1, TOK_TILE, w), lambda d, i, s: (i, _scan_tile(d, s, nt), 0))

    def tr(w):
        return pl.BlockSpec((1, w, TOK_TILE), lambda d, i, s: (i, 0, _scan_tile(d, s, nt)))

    return pl.pallas_call(
        _ssd_kernel,
        grid=(2, b, nt),
        in_specs=[tok(SSM_W), tok(SSM_GROUPS * SSM_N), tr(SSM_GROUPS * SSM_N), tok(SMALL_W), tr(SMALL_W)],
        out_specs=pl.BlockSpec((1, 1, TOK_TILE, SSM_W), lambda d, i, s: (d, i, _scan_tile(d, s, nt), 0)),
        out_shape=jax.ShapeDtypeStruct((2, b, n, SSM_W), jnp.float32),
        scratch_shapes=[pltpu.VMEM((SSM_GROUPS * SSM_N, SSM_W), jnp.float32)],
        compiler_params=pltpu.CompilerParams(dimension_semantics=("parallel", "parallel", "arbitrary")),
        name="ssd_scan",
    )(xs, cm, bt, cols, rows)


def _gdn_kernel(q_ref, k_ref, kt_ref, v_ref, cols_ref, rows_ref, o_ref, s_ref):
    is_f = pl.program_id(0) == 0

    @pl.when(pl.program_id(2) == 0)
    def _():
        s_ref[...] = jnp.zeros_like(s_ref)

    bf16 = jnp.bfloat16
    f32 = jnp.float32
    t = TOK_TILE
    cols = cols_ref[0]
    rows = rows_ref[0]
    incl, strict = _order_masks(is_f)
    lane = lax.broadcasted_iota(jnp.int32, (t, GDN_W), 1)
    sub = lax.broadcasted_iota(jnp.int32, (GDN_W, t), 0)
    head_of_lane = [(lane >= h * GDN_DK) & (lane < (h + 1) * GDN_DK) for h in range(GDN_HEADS)]
    head_of_sub = [(sub >= h * GDN_DK) & (sub < (h + 1) * GDN_DK) for h in range(GDN_HEADS)]

    def by_lane(pieces):
        out = pieces[0]
        for h in range(1, GDN_HEADS):
            out = jnp.where(head_of_lane[h] if pieces[h].shape[0] != 1 else head_of_lane[h][0:1, :], pieces[h], out)
        return out

    q = q_ref[0]
    k = k_ref[0]
    v = v_ref[0]
    kt = kt_ref[0]
    gc_col = [_pick(is_f, cols, L_GCUM, GDN_HEADS, h, 1) for h in range(GDN_HEADS)]
    gc_row = [_pick(is_f, rows, L_GCUM, GDN_HEADS, h, 0) for h in range(GDN_HEADS)]
    gc_end = [jnp.where(is_f, r[:, t - 1:t], r[:, 0:1]) for r in gc_row]
    beta_x = by_lane([_pick(is_f, cols, L_BETA, GDN_HEADS, h, 1) + jnp.zeros((t, GDN_W), f32) for h in range(GDN_HEADS)])
    egc = jnp.exp(by_lane([c + jnp.zeros((t, GDN_W), f32) for c in gc_col]))
    kb = k * beta_x
    vb = v * beta_x
    kbg = kb * egc
    qd = q * egc
    ri = lax.broadcasted_iota(jnp.int32, (t, t), 0)
    ci = lax.broadcasted_iota(jnp.int32, (t, t), 1)
    eye = jnp.where(ri == ci, 1.0, 0.0)
    off_levels = []
    size = 1
    while size < t:
        off_levels.append(((ri // (2 * size)) == (ci // (2 * size))) & ((ri // size) != (ci // size)))
        size *= 2

    a_mats = []
    qkd = []
    for h in range(GDN_HEADS):
        kbm = jnp.where(head_of_lane[h], kb, 0.0).astype(bf16)
        qm = jnp.where(head_of_lane[h], q, 0.0).astype(bf16)
        kk = jnp.dot(kbm, kt, preferred_element_type=f32)
        qk = jnp.dot(qm, kt, preferred_element_type=f32)
        dec = jnp.exp(jnp.where(incl, gc_col[h] - gc_row[h], -jnp.inf))
        a_mats.append(jnp.where(strict, kk * dec, 0.0))
        qkd.append((qk * dec).astype(bf16))
    t_inv = [eye - jnp.where(off_levels[0], a, 0.0) for a in a_mats]
    for lvl_mask in off_levels[1:]:
        tbs = [x.astype(bf16) for x in t_inv]
        es = [jnp.where(lvl_mask, a, 0.0).astype(bf16) for a in a_mats]
        tes = [jnp.dot(tb, e, preferred_element_type=f32).astype(bf16) for tb, e in zip(tbs, es)]
        t_inv = [x - jnp.dot(te, tb, preferred_element_type=f32) for x, te, tb in zip(t_inv, tes, tbs)]
    uw = None
    for h in range(GDN_HEADS):
        rhs = jnp.concatenate([jnp.where(head_of_lane[h], vb, 0.0), jnp.where(head_of_lane[h], kbg, 0.0)], axis=1)
        term = jnp.dot(t_inv[h].astype(bf16), rhs.astype(bf16), preferred_element_type=f32)
        uw = term if uw is None else uw + term

    s_old = s_ref[...]
    sb = s_old.astype(bf16)
    u = uw[:, :GDN_W]
    w = uw[:, GDN_W:]
    v_new = u - jnp.dot(w.astype(bf16), sb, preferred_element_type=f32)
    o = jnp.dot(qd.astype(bf16), sb, preferred_element_type=f32)
    for h in range(GDN_HEADS):
        o = o + jnp.dot(qkd[h], jnp.where(head_of_lane[h], v_new, 0.0).astype(bf16), preferred_element_type=f32)
    o_ref[0, 0] = o

    mult = jnp.exp(gc_end[0] - gc_row[0]) + jnp.zeros((GDN_W, t), f32)
    for h in range(1, GDN_HEADS):
        mult = jnp.where(head_of_sub[h], jnp.exp(gc_end[h] - gc_row[h]), mult)
    kend_t = (kt.astype(f32) * mult).astype(bf16)
    upd = jnp.dot(kend_t, v_new.astype(bf16), preferred_element_type=f32)
    g_end = jnp.exp(by_lane(gc_end))
    same_head = head_of_sub[0] & head_of_lane[0]
    for h in range(1, GDN_HEADS):
        same_head = same_head | (head_of_sub[h] & head_of_lane[h])
    s_ref[...] = jnp.where(same_head, s_old * g_end + upd, 0.0)


def _gdn_scan(q, k, kt, v, cols, rows):
    b, n, _ = q.shape
    nt = n // TOK_TILE

    def tok(w):
        return pl.BlockSpec((1, TOK_TILE, w), lambda d, i, s: (i, _scan_tile(d, s, nt), 0))

    def tr(w):
        return pl.BlockSpec((1, w, TOK_TILE), lambda d, i, s: (i, 0, _scan_tile(d, s, nt)))

    return pl.pallas_call(
        _gdn_kernel,
        grid=(2, b, nt),
        in_specs=[tok(GDN_W), tok(GDN_W), tr(GDN_W), tok(GDN_W), tok(SMALL_W), tr(SMALL_W)],
        out_specs=pl.BlockSpec((1, 1, TOK_TILE, GDN_W), lambda d, i, s: (d, i, _scan_tile(d, s, nt), 0)),
        out_shape=jax.ShapeDtypeStruct((2, b, n, GDN_W), jnp.float32),
        scratch_shapes=[pltpu.VMEM((GDN_W, GDN_W), jnp.float32)],
        compiler_params=pltpu.CompilerParams(dimension_semantics=("parallel", "parallel", "arbitrary")),
        name="gdn_scan",
    )(q, k, kt, v, cols, rows)


def _small_lane_params(ssm_dt_bias, ssm_a_log, gdn_dt_bias, gdn_a_log):
    f32 = jnp.float32
    z = lambda k: jnp.zeros((k,), f32)
    sb = ssm_dt_bias.astype(f32).reshape(-1)
    bias = jnp.concatenate([sb, sb, z(L_GCUM - L_BETA), gdn_dt_bias.astype(f32).reshape(-1),
                            z(SMALL_W - L_GCUM - 2 * GDN_HEADS)])
    scale = jnp.concatenate([z(L_SCUM), -jnp.exp(ssm_a_log.astype(f32)).reshape(-1), z(L_GCUM - L_BETA),
                             -jnp.exp(gdn_a_log.astype(f32)).reshape(-1), z(SMALL_W - L_GCUM - 2 * GDN_HEADS)])
    return bias[None, :], scale[None, :]


def _permuted_w_in(w_in):
    offs = np.cumsum((0,) + SPLIT_SIZES)
    dt0, beta0, a0 = int(offs[3]), int(offs[6]), int(offs[7])
    dt = list(range(dt0, dt0 + SSM_HEADS))
    small = dt + dt + dt + dt + list(range(beta0, beta0 + 2 * GDN_HEADS)) + list(range(a0, a0 + 2 * GDN_HEADS))
    cols = list(range(0, dt0)) + list(range(int(offs[4]), beta0)) + small
    w = w_in[:, np.asarray(cols)]
    return jnp.pad(w, ((0, 0), (0, SMALL_W - len(small))))


def mixers_pallas(xbc, gqkv, small, ssm_conv_w, ssm_conv_b, ssm_dt_bias, ssm_a_log, gdn_conv_w, gdn_dt_bias, gdn_a_log):
    bias_vec, scale_vec = _small_lane_params(ssm_dt_bias, ssm_a_log, gdn_dt_bias, gdn_a_log)
    pad = ((0, 8 - CONV_K), (0, 0))
    xs, cm, bt, q, k, kt, v, cols, rows = _mixer_prep(
        xbc, gqkv, small, jnp.pad(ssm_conv_w, pad), ssm_conv_b[None, :], jnp.pad(gdn_conv_w, pad), bias_vec, scale_vec)
    y = _ssd_scan(xs, cm, bt, cols, rows)
    o = _gdn_scan(q, k, kt, v, cols, rows)
    return xs, y, o


def _expert_kernel(be_ref, nu_ref, x_ref, rg_ref, wg_ref, wu_ref, wd_ref, o_ref, wgb_ref, wub_ref, wdb_ref):
    i = pl.program_id(0)

    @pl.when((i == 0) | (be_ref[i] != be_ref[jnp.maximum(i - 1, 0)]))
    def _():
        wgb_ref[...] = wg_ref[0, 0].astype(jnp.bfloat16)
        wub_ref[...] = wu_ref[0, 0].astype(jnp.bfloat16)
        wdb_ref[...] = wd_ref[0, 0].astype(jnp.bfloat16)

    @pl.when(i < nu_ref[0])
    def _():
        x = x_ref[...]
        g = jnp.dot(x, wgb_ref[...], preferred_element_type=jnp.float32)
        u = jnp.dot(x, wub_ref[...], preferred_element_type=jnp.float32)
        h = (_silu(g) * u).astype(jnp.bfloat16)
        o_ref[...] = (jnp.dot(h, wdb_ref[...], preferred_element_type=jnp.float32) * rg_ref[...]).astype(o_ref.dtype)

    @pl.when(i >= nu_ref[0])
    def _():
        o_ref[...] = jnp.zeros_like(o_ref)


def _expert_blocks(layer, block_e, n_used, buf, row_gate, w_gate, w_up, w_down):
    n_rows, d = buf.shape
    n_blocks = n_rows // MOE_BLOCK
    grid_spec = pltpu.PrefetchScalarGridSpec(
        num_scalar_prefetch=2,
        grid=(n_blocks,),
        in_specs=[pl.BlockSpec((MOE_BLOCK, d), lambda i, be, nu: (i, 0)),
                  pl.BlockSpec((MOE_BLOCK, 1), lambda i, be, nu: (i, 0)),
                  pl.BlockSpec((1, 1, d, D_EXPERT), lambda i, be, nu: (layer, be[i], 0, 0)),
                  pl.BlockSpec((1, 1, d, D_EXPERT), lambda i, be, nu: (layer, be[i], 0, 0)),
                  pl.BlockSpec((1, 1, D_EXPERT, d), lambda i, be, nu: (layer, be[i], 0, 0))],
        out_specs=pl.BlockSpec((MOE_BLOCK, d), lambda i, be, nu: (i, 0)),
        scratch_shapes=[pltpu.VMEM((d, D_EXPERT), jnp.bfloat16), pltpu.VMEM((d, D_EXPERT), jnp.bfloat16),
                        pltpu.VMEM((D_EXPERT, d), jnp.bfloat16)],
    )
    return pl.pallas_call(
        _expert_kernel,
        grid_spec=grid_spec,
        out_shape=jax.ShapeDtypeStruct((n_rows, d), jnp.bfloat16),
        compiler_params=pltpu.CompilerParams(dimension_semantics=("arbitrary",), vmem_limit_bytes=MOE_VMEM_BYTES),
        name="moe_experts",
    )(block_e, n_used, buf, row_gate, w_gate, w_up, w_down)


def routed_experts(layer, t, experts, gates, w_gate, w_up, w_down):
    n, d = t.shape
    n_assign = n * TOP_K
    flat_e = experts.reshape(n_assign)
    order = jnp.argsort(flat_e).astype(jnp.int32)
    slot = jnp.argsort(order).astype(jnp.int32)
    counts = jax.ops.segment_sum(jnp.ones((n_assign,), jnp.int32), flat_e, num_segments=N_EXPERTS)
    starts = jnp.cumsum(counts) - counts
    padded = (counts + MOE_BLOCK - 1) // MOE_BLOCK * MOE_BLOCK
    pad_ends = jnp.cumsum(padded)
    pad_starts = pad_ends - padded
    n_blocks = -(-n_assign // MOE_BLOCK) + N_EXPERTS
    block_start = jnp.arange(n_blocks, dtype=jnp.int32) * MOE_BLOCK
    block_e = jnp.minimum(jnp.sum(block_start[:, None] >= pad_ends[None, :], axis=1), N_EXPERTS - 1).astype(jnp.int32)
    n_used = (pad_ends[-1] // MOE_BLOCK).astype(jnp.int32).reshape(1)
    within = (block_start - pad_starts[block_e])[:, None] + jnp.arange(MOE_BLOCK, dtype=jnp.int32)[None, :]
    valid = (within < counts[block_e][:, None]).reshape(-1)
    src_slot = jnp.minimum(starts[block_e][:, None] + within, n_assign - 1).reshape(-1)
    src_assign = order[src_slot]
    src_tok = jnp.where(valid, src_assign // TOP_K, 0)
    row_gate = jnp.where(valid, gates.reshape(n_assign).astype(jnp.float32)[src_assign], 0.0)[:, None]
    out = _expert_blocks(layer, block_e, n_used, t[src_tok], row_gate, w_gate, w_up, w_down)
    pos = (pad_starts[flat_e] + slot - starts[flat_e]).reshape(n, TOP_K)
    y = out[pos[:, 0]].astype(jnp.float32)
    for k in range(1, TOP_K):
        y = y + out[pos[:, k]].astype(jnp.float32)
    return y


def hier_moe(layer, t, t_rows, router_g_w, router_g_b, router_e_w, router_e_b, w_gate, w_up, w_down):
    n = t.shape[0]
    grp_prob = jax.nn.softmax(jnp.dot(t, router_g_w, preferred_element_type=jnp.float32)
                              + router_g_b.astype(jnp.float32), axis=-1)
    grp = jnp.argmax(grp_prob, axis=-1).astype(jnp.int32)
    p_grp = jnp.max(grp_prob, axis=-1)
    e_logits = (jnp.dot(t, router_e_w, preferred_element_type=jnp.float32)
                + router_e_b.astype(jnp.float32)).reshape(n, N_EGROUPS, EXPERTS_PER_GROUP)
    in_grp = grp[:, None, None] == jnp.arange(N_EGROUPS, dtype=jnp.int32)[None, :, None]
    probs = jax.nn.softmax(jnp.sum(jnp.where(in_grp, e_logits, 0.0), axis=1), axis=-1)
    lanes = jnp.arange(EXPERTS_PER_GROUP, dtype=jnp.int32)[None, :]
    idx, p_top = [], []
    for _ in range(TOP_K):
        i = jnp.argmax(probs, axis=-1).astype(jnp.int32)
        idx.append(i)
        p_top.append(jnp.max(probs, axis=-1))
        probs = jnp.where(lanes == i[:, None], -jnp.inf, probs)
    idx = jnp.stack(idx, axis=-1)
    p_top = jnp.stack(p_top, axis=-1)
    gates = p_grp[:, None] * p_top / jnp.sum(p_top, axis=-1, keepdims=True)
    experts = grp[:, None] * EXPERTS_PER_GROUP + idx
    return routed_experts(layer, t_rows, experts, gates, w_gate, w_up, w_down)


def _matmul_kernel(a_ref, b_ref, o_ref):
    o_ref[...] = jnp.dot(a_ref[...].astype(jnp.bfloat16), b_ref[...],
                         preferred_element_type=jnp.float32)


def _matmul(a, b, tm, tn):
    m, k = a.shape
    n = b.shape[1]
    return pl.pallas_call(
        _matmul_kernel,
        grid=(m // tm, n // tn),
        in_specs=[pl.BlockSpec((tm, k), lambda i, j: (i, 0)),
                  pl.BlockSpec((k, tn), lambda i, j: (0, j))],
        out_specs=pl.BlockSpec((tm, tn), lambda i, j: (i, j)),
        out_shape=jax.ShapeDtypeStruct((m, n), jnp.float32),
        compiler_params=pltpu.CompilerParams(dimension_semantics=("parallel", "parallel")),
        name="dense",
    )(a, b)


def _dense(a, w, tm=512):
    lead = a.shape[:-1]
    k = a.shape[-1]
    n = w.shape[1]
    a2 = a.reshape(-1, k)
    m = a2.shape[0]
    m_pad = -(-m // 8) * 8
    if m_pad != m:
        a2 = jnp.pad(a2, ((0, m_pad - m), (0, 0)))
    tm = min(tm, m_pad)
    n_pad = -(-n // 128) * 128
    tn = next(c for c in (1024, 768, 640, 512, 384, 256, 128) if n_pad % c == 0)
    wb = jnp.pad(w, ((0, 0), (0, n_pad - n))).astype(jnp.bfloat16)
    out = _matmul(a2, wb, tm, tn)[:m, :n]
    return out.reshape(lead + (n,))


IN_SLABS = (3 * DIFF_W, SSM_W, SSM_XBC, GDN_QKV, GDN_W, SMALL_W)
PROJ_VMEM_BYTES = 44 * 1024 * 1024


def _mod_row_spec(bsz):
    return pl.BlockSpec((1, MOD_CHUNKS, D_MODEL), lambda i, j: (jnp.where(j == 0, bsz, i), 0, 0))


def _in_proj_kernel(x_ref, nw_ref, mod_ref, w_ref, *out_refs):
    x = x_ref[0]
    xn = x * lax.rsqrt(jnp.mean(x * x, axis=-1, keepdims=True) + EPS) * nw_ref[...]
    mod = mod_ref[0]
    xm = (xn * (1.0 + mod[1:2, :]) + mod[0:1, :]).astype(jnp.bfloat16)
    c0 = 0
    for o_ref, width in zip(out_refs, IN_SLABS):
        o_ref[0] = jnp.dot(xm, w_ref[:, c0:c0 + width], preferred_element_type=jnp.float32)
        c0 += width


def _in_proj(x, norm_w, mod, w):
    b, n, d = x.shape
    tok = lambda width: pl.BlockSpec((1, TOK_TILE, width), lambda i, j: (i, j, 0))
    return pl.pallas_call(
        _in_proj_kernel,
        grid=(b, n // TOK_TILE),
        in_specs=[tok(d), pl.BlockSpec((1, d), lambda i, j: (0, 0)), _mod_row_spec(b),
                  pl.BlockSpec(w.shape, lambda i, j: (0, 0))],
        out_specs=[tok(width) for width in IN_SLABS],
        out_shape=[jax.ShapeDtypeStruct((b, n, width), jnp.float32) for width in IN_SLABS],
        compiler_params=pltpu.CompilerParams(dimension_semantics=("parallel", "parallel"),
                                             vmem_limit_bytes=PROJ_VMEM_BYTES),
        name="in_proj",
    )(x, norm_w, mod, w)


def _out_proj_kernel(att_ref, y0_ref, y1_ref, xs_ref, z_ref, o0_ref, o1_ref, gate_ref, x_ref, mod_ref,
                     dvec_ref, snw_ref, gnw_ref, n2w_ref, g_ref, w_ref, xo_ref, f_ref, fb_ref):
    ys = (y0_ref[0, 0] + y1_ref[0, 0] + dvec_ref[...] * xs_ref[0]) * _silu(z_ref[0])
    gw = SSM_W // SSM_GROUPS
    parts = [att_ref[0]]
    for g in range(SSM_GROUPS):
        yg = ys[:, g * gw:(g + 1) * gw]
        parts.append(yg * lax.rsqrt(jnp.mean(yg * yg, axis=-1, keepdims=True) + EPS) * snw_ref[:, g * gw:(g + 1) * gw])
    o = o0_ref[0, 0] + o1_ref[0, 0]
    parts.append(o * lax.rsqrt(_group_sum_sq(o, g_ref) + EPS) * gnw_ref[...] * _silu(gate_ref[0]))
    ml = jnp.concatenate(parts, axis=1).astype(jnp.bfloat16)
    mod = mod_ref[0]
    xn = x_ref[0] + mod[2:3, :] * jnp.dot(ml, w_ref[...], preferred_element_type=jnp.float32)
    xo_ref[0] = xn
    f = xn * lax.rsqrt(jnp.mean(xn * xn, axis=-1, keepdims=True) + EPS) * n2w_ref[...]
    f = f * (1.0 + mod[4:5, :]) + mod[3:4, :]
    f_ref[0] = f
    fb_ref[0] = f.astype(jnp.bfloat16)


def _out_proj(att, y, xs, z, o, gate, x, mod, dvec, snw, gnw, n2w, g64, w, latent_only):
    b, n, d = x.shape
    t0 = 1 if latent_only else 0
    n_out = n - t0 * TOK_TILE
    tok = lambda width: pl.BlockSpec((1, TOK_TILE, width), lambda i, j: (i, j + t0, 0))
    dirs = lambda width, k: pl.BlockSpec((1, 1, TOK_TILE, width), lambda i, j: (k, i, j + t0, 0))
    vec = lambda width: pl.BlockSpec((1, width), lambda i, j: (0, 0))
    out_tok = pl.BlockSpec((1, TOK_TILE, d), lambda i, j: (i, j, 0))
    mod_spec = pl.BlockSpec((1, MOD_CHUNKS, D_MODEL), lambda i, j: (i, 0, 0)) if latent_only else _mod_row_spec(b)
    return pl.pallas_call(
        _out_proj_kernel,
        grid=(b, n_out // TOK_TILE),
        in_specs=[tok(DIFF_W), dirs(SSM_W, 0), dirs(SSM_W, 1), tok(SSM_W), tok(SSM_W), dirs(GDN_W, 0), dirs(GDN_W, 1),
                  tok(GDN_W), tok(d), mod_spec, vec(SSM_W), vec(SSM_W), vec(GDN_W), vec(d),
                  pl.BlockSpec(g64.shape, lambda i, j: (0, 0)), pl.BlockSpec(w.shape, lambda i, j: (0, 0))],
        out_specs=[out_tok, out_tok, out_tok],
        out_shape=[jax.ShapeDtypeStruct((b, n_out, d), jnp.float32), jax.ShapeDtypeStruct((b, n_out, d), jnp.float32),
                   jax.ShapeDtypeStruct((b, n_out, d), jnp.bfloat16)],
        compiler_params=pltpu.CompilerParams(dimension_semantics=("parallel", "parallel"),
                                             vmem_limit_bytes=PROJ_VMEM_BYTES),
        name="out_proj",
    )(att, y, y, xs, z, o, o, gate, x, mod, dvec, snw, gnw, n2w, g64, w)


def kernel(x, c, ctx, c_ctx, w_mod, b_mod, norm1_w, norm2_w, w_in, w_out,
           diff_qn_w, diff_kn_w, diff_lq1, diff_lk1, diff_lq2, diff_lk2, diff_norm_w,
           ssm_conv_w, ssm_conv_b, ssm_dt_bias, ssm_a_log, ssm_d, ssm_norm_w,
           gdn_conv_w, gdn_dt_bias, gdn_a_log, gdn_norm_w,
           router_g_w, router_g_b, router_e_w, router_e_b, exp_w_gate, exp_w_up, exp_w_down):
    assert ctx.shape[1] == CTX_LEN == TOK_TILE and x.shape[1] % TOK_TILE == 0
    bsz = x.shape[0]
    xs_all = jnp.concatenate([ctx, x], axis=1)
    n_tok = xs_all.shape[1]
    is_lat = (jnp.arange(n_tok) >= CTX_LEN)[None, :, None]
    cc = jnp.concatenate([c, c_ctx[None, :]], axis=0)
    bf16 = jnp.bfloat16
    g64 = _group_matrix(GDN_W, GDN_DV, 1.0 / GDN_DV)
    for l in range(DEPTH):
        last = l == DEPTH - 1
        lam_init = 0.8 - 0.6 * math.exp(-0.3 * l)
        mod = (_dense(jax.nn.silu(cc), w_mod[l]) + b_mod[l]).reshape(bsz + 1, MOD_CHUNKS, D_MODEL)
        qkv, z, xbc, gqkv, gate, small = _in_proj(xs_all, norm1_w[l][None, :], mod, _permuted_w_in(w_in[l]).astype(bf16))
        att = diff_attention_pallas(qkv, lam_init, diff_qn_w[l], diff_kn_w[l],
                                    diff_lq1[l], diff_lk1[l], diff_lq2[l], diff_lk2[l], diff_norm_w[l])
        xs, y, o = mixers_pallas(xbc, gqkv, small, ssm_conv_w[l], ssm_conv_b[l],
                                 ssm_dt_bias[l], ssm_a_log[l], gdn_conv_w[l], gdn_dt_bias[l], gdn_a_log[l])
        xs_all, f, fb = _out_proj(att, y, xs, z, o, gate, xs_all, mod, jnp.repeat(ssm_d[l], SSM_P)[None, :],
                                  ssm_norm_w[l][None, :], jnp.tile(gdn_norm_w[l], GDN_HEADS)[None, :],
                                  norm2_w[l][None, :], g64, w_out[l].astype(bf16), latent_only=last)
        moe_w = (router_g_w[l], router_g_b[l], router_e_w[l], router_e_b[l], exp_w_gate, exp_w_up, exp_w_down)
        if last:
            yl = hier_moe(l, f.reshape(-1, D_MODEL), fb.reshape(-1, D_MODEL), *moe_w)
            return xs_all + mod[:bsz, None, 5, :] * yl.reshape(xs_all.shape)
        ym = hier_moe(l, f.reshape(-1, D_MODEL), fb.reshape(-1, D_MODEL), *moe_w)
        mod5 = jnp.where(is_lat, mod[:bsz, None, 5, :], mod[bsz, 5, :])
        xs_all = xs_all + mod5 * ym.reshape(xs_all.shape)
```

```python
import functools
import math
import jax
import jax.numpy as jnp
from jax import lax
import numpy as np
from jax.experimental import pallas as pl
from jax.experimental.pallas import tpu as pltpu

D_MODEL = 1024
DEPTH = 2
CTX_LEN = 256
GRID_W = 64
EPS = 1e-6
MOD_CHUNKS = 6

DIFF_HEADS = 4
DIFF_QK = 32
DIFF_V = 2 * DIFF_QK
DIFF_W = DIFF_HEADS * DIFF_V
ROPE_BASE = 10000.0

SSM_HEADS = 8
SSM_P = 64
SSM_GROUPS = 2
SSM_N = 64
SSM_W = SSM_HEADS * SSM_P
SSM_XBC = SSM_W + 2 * SSM_GROUPS * SSM_N
CONV_K = 5

GDN_HEADS = 4
GDN_DK = 64
GDN_DV = 64
GDN_QKV = GDN_HEADS * (2 * GDN_DK + GDN_DV)
GDN_W = GDN_HEADS * GDN_DV

D_MIX = DIFF_W + SSM_W + GDN_W
SPLIT_SIZES = (3 * DIFF_W, SSM_W, SSM_XBC, SSM_HEADS, GDN_QKV, GDN_W, 2 * GDN_HEADS, 2 * GDN_HEADS)

N_EGROUPS = 4
EXPERTS_PER_GROUP = 8
N_EXPERTS = N_EGROUPS * EXPERTS_PER_GROUP
TOP_K = 2
D_EXPERT = 512
MOE_BLOCK = 512

TOK_TILE = 256
LOG2E = 1.4426950408889634
EXP_ROWS = 64
VT_ROWS = DIFF_W + 16
ATTN_VMEM_BYTES = 52 * 1024 * 1024
MOE_VMEM_BYTES = 40 * 1024 * 1024
HALO = 8
SMALL_W = 128
L_DT = 0
L_SCUM = 16
L_BETA = 32
L_GCUM = 40


def rmsnorm(x, w):
    x32 = x.astype(jnp.float32)
    y = x32 * lax.rsqrt(jnp.mean(x32 * x32, axis=-1, keepdims=True) + EPS)
    return y.astype(x.dtype) * w


def modulate(x, shift, scale):
    return x * (1.0 + scale) + shift


def _bf16_terms(x, n):
    out = []
    for _ in range(n):
        t = x.astype(jnp.bfloat16)
        out.append(t)
        x = x - t.astype(jnp.float32)
    return out


def _group_sum_sq(x, g_ref):
    hi, lo = _bf16_terms(x * x, 2)
    g = g_ref[...]
    return (jnp.dot(hi, g, preferred_element_type=jnp.float32)
            + jnp.dot(lo, g, preferred_element_type=jnp.float32))


def _group_matrix(width, group, value):
    idx = np.arange(width) // group
    return jnp.asarray((idx[:, None] == idx[None, :]).astype(np.float32) * value, jnp.bfloat16)


def _attn_prep_kernel(qkv_ref, cos_ref, sin_ref, wq_ref, wk_ref, g_ref, qt_out, k_out, vt_out):
    x = qkv_ref[0]
    cos = cos_ref[...]
    sin = sin_ref[...]
    lane = lax.broadcasted_iota(jnp.int32, cos.shape, 1)
    lo_half = (lane % 16) < 8

    def norm_rope(t, w):
        y = t * lax.rsqrt(_group_sum_sq(t, g_ref) + EPS) * w
        rot = jnp.where(lo_half, pltpu.roll(y, DIFF_W - 8, 1), pltpu.roll(y, 8, 1))
        return y * cos + rot * sin

    q = norm_rope(x[:, 0:DIFF_W], wq_ref[...]) * (DIFF_QK ** -0.5 * LOG2E)
    qt_out[0] = q.T
    k_out[0] = norm_rope(x[:, DIFF_W:2 * DIFF_W], wk_ref[...]).astype(jnp.bfloat16)
    vt_out[0, :DIFF_W, :] = x[:, 2 * DIFF_W:3 * DIFF_W].T.astype(jnp.bfloat16)
    vt_out[0, DIFF_W:, :] = jnp.ones((VT_ROWS - DIFF_W, TOK_TILE), jnp.bfloat16)


def _attn_prep(qkv, cos_t, sin_t, wq, wk, gmat):
    b, n, _ = qkv.shape
    nt = n // TOK_TILE
    tok = pl.BlockSpec((1, TOK_TILE, DIFF_W), lambda i, j: (i, j, 0))
    tr = pl.BlockSpec((1, DIFF_W, TOK_TILE), lambda i, j: (i, 0, j))
    return pl.pallas_call(
        _attn_prep_kernel,
        grid=(b, nt),
        in_specs=[pl.BlockSpec((1, TOK_TILE, 3 * DIFF_W), lambda i, j: (i, j, 0)),
                  pl.BlockSpec((TOK_TILE, DIFF_W), lambda i, j: (j, 0)),
                  pl.BlockSpec((TOK_TILE, DIFF_W), lambda i, j: (j, 0)),
                  pl.BlockSpec((1, DIFF_W), lambda i, j: (0, 0)),
                  pl.BlockSpec((1, DIFF_W), lambda i, j: (0, 0)),
                  pl.BlockSpec((DIFF_W, DIFF_W), lambda i, j: (0, 0))],
        out_specs=[tr, tok, pl.BlockSpec((1, VT_ROWS, TOK_TILE), lambda i, j: (i, 0, j))],
        out_shape=[jax.ShapeDtypeStruct((b, DIFF_W, n), jnp.float32),
                   jax.ShapeDtypeStruct((b, n, DIFF_W), jnp.bfloat16),
                   jax.ShapeDtypeStruct((b, VT_ROWS, n), jnp.bfloat16)],
        compiler_params=pltpu.CompilerParams(dimension_semantics=("parallel", "parallel")),
        name="attn_prep",
    )(qkv, cos_t, sin_t, wq, wk, gmat)


def _attn_kernel(lam_ref, qt_ref, k_ref, vt_ref, wo_ref, g_ref, o_ref, s_ref, p_ref, *, n_keys, out_scale):
    tile = pl.program_id(1)
    lam = lam_ref[0]
    sub = lax.broadcasted_iota(jnp.int32, (DIFF_W, TOK_TILE), 0)

    def attend(nk):
        n_chunks = nk // TOK_TILE
        qt = qt_ref[0]

        def masked_qt(head):
            return jnp.concatenate(
                [jnp.where((sub >= (2 * head + mp) * DIFF_QK) & (sub < (2 * head + mp + 1) * DIFF_QK), qt, 0.0)
                 for mp in range(2)], axis=1).astype(jnp.bfloat16)

        def score_chunk(qt2, buf, c, mx8):
            s = jnp.dot(k_ref[0, c * TOK_TILE:(c + 1) * TOK_TILE, :], qt2, preferred_element_type=jnp.float32)
            s_ref[buf, c * TOK_TILE:(c + 1) * TOK_TILE, :] = s
            cm = jnp.max(s.reshape(TOK_TILE // 8, 8, 2 * TOK_TILE), axis=0)
            return cm if mx8 is None else jnp.maximum(mx8, cm)

        mx8 = None
        qt2 = masked_qt(0)
        for c in range(n_chunks):
            mx8 = score_chunk(qt2, 0, c, mx8)
        acc = None
        for head in range(DIFF_HEADS):
            buf = head % 2
            mx = jnp.max(mx8, axis=0, keepdims=True)
            nxt = head + 1 < DIFF_HEADS
            if nxt:
                qt2 = masked_qt(head + 1)
            mx8 = None
            for c in range(n_chunks):
                if nxt:
                    mx8 = score_chunk(qt2, 1 - buf, c, mx8)
                for r in range(c * TOK_TILE // EXP_ROWS, (c + 1) * TOK_TILE // EXP_ROWS):
                    e = jnp.exp2(s_ref[buf, r * EXP_ROWS:(r + 1) * EXP_ROWS, :] - mx)
                    p_ref[r * EXP_ROWS:(r + 1) * EXP_ROWS, :] = e.astype(jnp.bfloat16)
            ov = jnp.dot(vt_ref[0, :, :nk], p_ref[:nk, :], preferred_element_type=jnp.float32)
            ot = ov[:DIFF_W] / ov[DIFF_W:DIFF_W + 1]
            in_head = (sub >= head * DIFF_V) & (sub < (head + 1) * DIFF_V)
            part = jnp.where(in_head, ot[:, :TOK_TILE] - lam * ot[:, TOK_TILE:], 0.0)
            acc = part if acc is None else acc + part
        o = acc.T
        y = o * lax.rsqrt(_group_sum_sq(o, g_ref) + EPS)
        o_ref[0] = y * wo_ref[...] * out_scale

    @pl.when(tile == 0)
    def _():
        attend(TOK_TILE)

    @pl.when(tile > 0)
    def _():
        attend(n_keys)


def _attention(lam, qt, k, vt, wo, gmat, out_scale):
    b, n, _ = k.shape
    nt = n // TOK_TILE
    kern = functools.partial(_attn_kernel, n_keys=n, out_scale=out_scale)
    return pl.pallas_call(
        kern,
        grid=(b, nt),
        in_specs=[pl.BlockSpec(memory_space=pltpu.SMEM),
                  pl.BlockSpec((1, DIFF_W, TOK_TILE), lambda i, j: (i, 0, j)),
                  pl.BlockSpec((1, n, DIFF_W), lambda i, j: (i, 0, 0)),
                  pl.BlockSpec((1, VT_ROWS, n), lambda i, j: (i, 0, 0)),
                  pl.BlockSpec((1, DIFF_W), lambda i, j: (0, 0)),
                  pl.BlockSpec((DIFF_W, DIFF_W), lambda i, j: (0, 0))],
        out_specs=pl.BlockSpec((1, TOK_TILE, DIFF_W), lambda i, j: (i, j, 0)),
        out_shape=jax.ShapeDtypeStruct((b, n, DIFF_W), jnp.float32),
        scratch_shapes=[pltpu.VMEM((2, n, 2 * TOK_TILE), jnp.float32),
                        pltpu.VMEM((n, 2 * TOK_TILE), jnp.bfloat16)],
        compiler_params=pltpu.CompilerParams(dimension_semantics=("parallel", "arbitrary"),
                                             vmem_limit_bytes=ATTN_VMEM_BYTES),
        name="diff_attn",
    )(lam, qt, k, vt, wo, gmat)


def _rope_tables(n_latent):
    rows = n_latent // GRID_W
    row = np.repeat(np.arange(rows, dtype=np.float32), GRID_W)
    col = np.tile(np.arange(GRID_W, dtype=np.float32), rows)
    half = DIFF_QK // 2
    inv = (ROPE_BASE ** (-np.arange(0, half, 2, dtype=np.float32) / half)).astype(np.float32)
    ang = np.concatenate([row[:, None] * inv, row[:, None] * inv, col[:, None] * inv, col[:, None] * inv], axis=-1)
    cos = np.concatenate([np.ones((TOK_TILE, DIFF_QK), np.float32), np.cos(ang)], axis=0)
    sin = np.concatenate([np.zeros((TOK_TILE, DIFF_QK), np.float32), np.sin(ang)], axis=0)
    sign = np.where((np.arange(DIFF_QK) % 16) < 8, -1.0, 1.0).astype(np.float32)
    reps = DIFF_W // DIFF_QK
    return jnp.asarray(np.tile(cos, (1, reps))), jnp.asarray(np.tile(sin * sign, (1, reps)))


def diff_attention_pallas(qkv, lam_init, qn_w, kn_w, lq1, lk1, lq2, lk2, out_w):
    n = qkv.shape[1]
    cos_t, sin_t = _rope_tables(n - TOK_TILE)
    g32 = _group_matrix(DIFF_W, DIFF_QK, 1.0 / DIFF_QK)
    g64 = _group_matrix(DIFF_W, DIFF_V, 1.0 / DIFF_V)
    reps = DIFF_W // DIFF_QK
    qt, k, vt = _attn_prep(qkv, cos_t, sin_t, jnp.tile(qn_w, reps)[None, :], jnp.tile(kn_w, reps)[None, :], g32)
    lam = (jnp.exp(jnp.sum(lq1 * lk1)) - jnp.exp(jnp.sum(lq2 * lk2)) + lam_init).reshape(1).astype(jnp.float32)
    return _attention(lam, qt, k, vt, jnp.tile(out_w, DIFF_HEADS)[None, :], g64, 1.0 - lam_init)


def _dwconv5(cur_ref, prev_ref, next_ref, w_ref, ext_ref, has_prev, has_next):
    t = TOK_TILE
    ext_ref[0:HALO, :] = jnp.where(has_prev, prev_ref[0], 0.0)
    ext_ref[HALO:HALO + t, :] = cur_ref[0]
    ext_ref[HALO + t:2 * HALO + t, :] = jnp.where(has_next, next_ref[0], 0.0)
    acc = None
    for j in range(CONV_K):
        term = ext_ref[pl.ds(HALO - CONV_K // 2 + j, t), :] * w_ref[j:j + 1, :]
        acc = term if acc is None else acc + term
    return acc


def _silu(x):
    return x * jax.nn.sigmoid(x)


def _mixer_prep_kernel(xbc_ref, xbc_p, xbc_n, gq_ref, gq_p, gq_n, sm_ref, sw_ref, sb_ref, gw_ref,
                       bias_ref, scale_ref, ones_ref,
                       xs_out, cm_out, bt_out, q_out, k_out, kt_out, v_out, cols_out, rows_out, ext_ref):
    tile = pl.program_id(1)
    nt = pl.num_programs(1)
    has_prev = tile >= 2
    has_next = (tile >= 1) & (tile < nt - 1)

    u = _silu(_dwconv5(xbc_ref, xbc_p, xbc_n, sw_ref, ext_ref, has_prev, has_next) + sb_ref[...])
    xs_out[0] = u[:, :SSM_W]
    bt_out[0] = u[:, SSM_W:SSM_W + SSM_GROUPS * SSM_N].T.astype(jnp.bfloat16)
    cm_out[0] = u[:, SSM_W + SSM_GROUPS * SSM_N:]

    g = _silu(_dwconv5(gq_ref, gq_p, gq_n, gw_ref, ext_ref, has_prev, has_next))

    def l2n(t):
        return t * lax.rsqrt(_group_sum_sq(t, ones_ref) + EPS)

    q_out[0] = l2n(g[:, :GDN_W]) * (GDN_DK ** -0.5)
    k = l2n(g[:, GDN_W:2 * GDN_W])
    k_out[0] = k
    kt_out[0] = k.T.astype(jnp.bfloat16)
    v_out[0] = g[:, 2 * GDN_W:]

    sm = sm_ref[0] + bias_ref[...]
    sp = jnp.maximum(sm, 0.0) + jnp.log1p(jnp.exp(-jnp.abs(sm)))
    sg = jax.nn.sigmoid(sm)
    vals = sp * scale_ref[...]
    ri = lax.broadcasted_iota(jnp.int32, (TOK_TILE, TOK_TILE), 0)
    ci = lax.broadcasted_iota(jnp.int32, (TOK_TILE, TOK_TILE), 1)
    tri_pre = jnp.where(ri >= ci, 1.0, 0.0).astype(jnp.bfloat16)
    tri_suf = jnp.where(ri <= ci, 1.0, 0.0).astype(jnp.bfloat16)
    pre = None
    suf = None
    for term in _bf16_terms(vals, 3):
        a = jnp.dot(tri_pre, term, preferred_element_type=jnp.float32)
        b = jnp.dot(tri_suf, term, preferred_element_type=jnp.float32)
        pre = a if pre is None else pre + a
        suf = b if suf is None else suf + b
    lane = lax.broadcasted_iota(jnp.int32, (TOK_TILE, SMALL_W), 1)
    bwd_lane = ((lane >= L_SCUM + SSM_HEADS) & (lane < L_BETA)) | (lane >= L_GCUM + GDN_HEADS)
    cum = jnp.where(bwd_lane, suf, pre)
    cols = jnp.where(lane < L_SCUM, sp, jnp.where((lane >= L_BETA) & (lane < L_GCUM), sg, cum))
    cols_out[0] = cols
    rows_out[0] = cols.T


def _mixer_prep(xbc, gqkv, small, ssm_w, ssm_b, gdn_w, bias_vec, scale_vec):
    b, n, _ = xbc.shape
    nt = n // TOK_TILE
    hb = TOK_TILE // HALO
    last = n // HALO - 1
    ones_blk = _group_matrix(GDN_W, GDN_DK, 1.0)

    def cur(w):
        return pl.BlockSpec((1, TOK_TILE, w), lambda i, j: (i, j, 0))

    def prev(w):
        return pl.BlockSpec((1, HALO, w), lambda i, j: (i, jnp.maximum(j * hb - 1, 0), 0))

    def nxt(w):
        return pl.BlockSpec((1, HALO, w), lambda i, j: (i, jnp.minimum((j + 1) * hb, last), 0))

    def const(shape):
        return pl.BlockSpec(shape, lambda i, j: (0,) * len(shape))

    f32 = jnp.float32
    return pl.pallas_call(
        _mixer_prep_kernel,
        grid=(b, nt),
        in_specs=[cur(SSM_XBC), prev(SSM_XBC), nxt(SSM_XBC), cur(GDN_QKV), prev(GDN_QKV), nxt(GDN_QKV), cur(SMALL_W),
                  const((8, SSM_XBC)), const((1, SSM_XBC)), const((8, GDN_QKV)),
                  const((1, SMALL_W)), const((1, SMALL_W)), const((GDN_W, GDN_W))],
        out_specs=[cur(SSM_W), cur(SSM_GROUPS * SSM_N),
                   pl.BlockSpec((1, SSM_GROUPS * SSM_N, TOK_TILE), lambda i, j: (i, 0, j)),
                   cur(GDN_W), cur(GDN_W),
                   pl.BlockSpec((1, GDN_W, TOK_TILE), lambda i, j: (i, 0, j)),
                   cur(GDN_W), cur(SMALL_W),
                   pl.BlockSpec((1, SMALL_W, TOK_TILE), lambda i, j: (i, 0, j))],
        out_shape=[jax.ShapeDtypeStruct((b, n, SSM_W), f32),
                   jax.ShapeDtypeStruct((b, n, SSM_GROUPS * SSM_N), f32),
                   jax.ShapeDtypeStruct((b, SSM_GROUPS * SSM_N, n), jnp.bfloat16),
                   jax.ShapeDtypeStruct((b, n, GDN_W), f32),
                   jax.ShapeDtypeStruct((b, n, GDN_W), f32),
                   jax.ShapeDtypeStruct((b, GDN_W, n), jnp.bfloat16),
                   jax.ShapeDtypeStruct((b, n, GDN_W), f32),
                   jax.ShapeDtypeStruct((b, n, SMALL_W), f32),
                   jax.ShapeDtypeStruct((b, SMALL_W, n), f32)],
        scratch_shapes=[pltpu.VMEM((TOK_TILE + 2 * HALO, SSM_XBC), f32)],
        compiler_params=pltpu.CompilerParams(dimension_semantics=("parallel", "parallel")),
        name="mixer_prep",
    )(xbc, xbc, xbc, gqkv, gqkv, gqkv, small, ssm_w, ssm_b, gdn_w, bias_vec, scale_vec, ones_blk)


def _scan_tile(d, s, nt):
    return jnp.where(d == 0, s, jnp.where(s == 0, 0, nt - s))


def _pick(is_f, arr, base, stride, i, axis):
    a, b = base + i, base + stride + i
    if axis == 1:
        return jnp.where(is_f, arr[:, a:a + 1], arr[:, b:b + 1])
    return jnp.where(is_f, arr[a:a + 1, :], arr[b:b + 1, :])


def _order_masks(is_f):
    ri = lax.broadcasted_iota(jnp.int32, (TOK_TILE, TOK_TILE), 0)
    ci = lax.broadcasted_iota(jnp.int32, (TOK_TILE, TOK_TILE), 1)
    diff = (ri - ci) * jnp.where(is_f, 1, -1)
    return diff >= 0, diff > 0


def _ssd_kernel(xs_ref, cm_ref, bt_ref, cols_ref, rows_ref, y_ref, st_ref):
    is_f = pl.program_id(0) == 0

    @pl.when(pl.program_id(2) == 0)
    def _():
        st_ref[...] = jnp.zeros_like(st_ref)

    bf16 = jnp.bfloat16
    f32 = jnp.float32
    t = TOK_TILE
    cols = cols_ref[0]
    rows = rows_ref[0]
    incl, _ = _order_masks(is_f)
    lane128 = lax.broadcasted_iota(jnp.int32, (t, 128), 1)
    cm = cm_ref[0]
    bt = bt_ref[0]
    hpg = SSM_HEADS // SSM_GROUPS
    for g in range(SSM_GROUPS):
        cg = jnp.where((lane128 >= g * SSM_N) & (lane128 < (g + 1) * SSM_N), cm, 0.0).astype(bf16)
        cb = jnp.dot(cg, bt, preferred_element_type=f32)
        st_g = st_ref[:, g * hpg * SSM_P:(g + 1) * hpg * SSM_P]
        yoff = jnp.dot(cg, st_g.astype(bf16), preferred_element_type=f32)
        btg = bt[g * SSM_N:(g + 1) * SSM_N, :].astype(f32)
        for pair in range(hpg // 2):
            xp = xs_ref[0, :, (g * hpg + 2 * pair) * SSM_P:(g * hpg + 2 * pair + 2) * SSM_P]
            acc_y = None
            acc_s = None
            e_col = None
            dec = None
            for half in range(2):
                hd = g * hpg + 2 * pair + half
                cum_col = _pick(is_f, cols, L_SCUM, SSM_HEADS, hd, 1)
                cum_row = _pick(is_f, rows, L_SCUM, SSM_HEADS, hd, 0)
                dt_col = _pick(is_f, cols, L_DT, SSM_HEADS, hd, 1)
                cum_tot = jnp.where(is_f, cum_row[:, t - 1:t], cum_row[:, 0:1])
                lm = jnp.exp(jnp.where(incl, cum_col - cum_row, -jnp.inf))
                m = (cb * lm).astype(bf16)
                in_half = (lane128 >= half * SSM_P) & (lane128 < (half + 1) * SSM_P)
                xm = jnp.where(in_half, xp * dt_col, 0.0).astype(bf16)
                ty = jnp.dot(m, xm, preferred_element_type=f32)
                ts = jnp.dot((btg * jnp.exp(cum_tot - cum_row)).astype(bf16), xm, preferred_element_type=f32)
                acc_y = ty if acc_y is None else acc_y + ty
                acc_s = ts if acc_s is None else acc_s + ts
                ec = jnp.exp(cum_col)
                dc = jnp.exp(cum_tot)
                e_col = ec if e_col is None else jnp.where(in_half, ec, e_col)
                dec = dc if dec is None else jnp.where(in_half[0:1, :], dc, dec)
            lo = pair * 2 * SSM_P
            c0 = g * hpg * SSM_P + lo
            y_ref[0, 0, :, c0:c0 + 2 * SSM_P] = acc_y + yoff[:, lo:lo + 2 * SSM_P] * e_col
            r0 = g * SSM_N
            st_ref[r0:r0 + SSM_N, c0:c0 + 2 * SSM_P] = st_ref[r0:r0 + SSM_N, c0:c0 + 2 * SSM_P] * dec + acc_s


def _ssd_scan(xs, cm, bt, cols, rows):
    b, n, _ = xs.shape
    nt = n // TOK_TILE

    def tok(w):
        return pl.BlockSpec((1, TOK_TILE, w), lambda d, i, s: (i, _scan_tile(d, s, nt), 0))

    def tr(w):
        return pl.BlockSpec((1, w, TOK_TILE), lambda d, i, s: (i, 0, _scan_tile(d, s, nt)))

    return pl.pallas_call(
        _ssd_kernel,
        grid=(2, b, nt),
        in_specs=[tok(SSM_W), tok(SSM_GROUPS * SSM_N), tr(SSM_GROUPS * SSM_N), tok(SMALL_W), tr(SMALL_W)],
        out_specs=pl.BlockSpec((1, 1, TOK_TILE, SSM_W), lambda d, i, s: (d, i, _scan_tile(d, s, nt), 0)),
        out_shape=jax.ShapeDtypeStruct((2, b, n, SSM_W), jnp.float32),
        scratch_shapes=[pltpu.VMEM((SSM_GROUPS * SSM_N, SSM_W), jnp.float32)],
        compiler_params=pltpu.CompilerParams(dimension_semantics=("parallel", "parallel", "arbitrary")),
        name="ssd_scan",
    )(xs, cm, bt, cols, rows)


def _gdn_kernel(q_ref, k_ref, kt_ref, v_ref, cols_ref, rows_ref, o_ref, s_ref):
    is_f = pl.program_id(0) == 0

    @pl.when(pl.program_id(2) == 0)
    def _():
        s_ref[...] = jnp.zeros_like(s_ref)

    bf16 = jnp.bfloat16
    f32 = jnp.float32
    t = TOK_TILE
    cols = cols_ref[0]
    rows = rows_ref[0]
    incl, strict = _order_masks(is_f)
    lane = lax.broadcasted_iota(jnp.int32, (t, GDN_W), 1)
    sub = lax.broadcasted_iota(jnp.int32, (GDN_W, t), 0)
    head_of_lane = [(lane >= h * GDN_DK) & (lane < (h + 1) * GDN_DK) for h in range(GDN_HEADS)]
    head_of_sub = [(sub >= h * GDN_DK) & (sub < (h + 1) * GDN_DK) for h in range(GDN_HEADS)]

    def by_lane(pieces):
        out = pieces[0]
        for h in range(1, GDN_HEADS):
            out = jnp.where(head_of_lane[h] if pieces[h].shape[0] != 1 else head_of_lane[h][0:1, :], pieces[h], out)
        return out

    q = q_ref[0]
    k = k_ref[0]
    v = v_ref[0]
    kt = kt_ref[0]
    gc_col = [_pick(is_f, cols, L_GCUM, GDN_HEADS, h, 1) for h in range(GDN_HEADS)]
    gc_row = [_pick(is_f, rows, L_GCUM, GDN_HEADS, h, 0) for h in range(GDN_HEADS)]
    gc_end = [jnp.where(is_f, r[:, t - 1:t], r[:, 0:1]) for r in gc_row]
    beta_x = by_lane([_pick(is_f, cols, L_BETA, GDN_HEADS, h, 1) + jnp.zeros((t, GDN_W), f32) for h in range(GDN_HEADS)])
    egc = jnp.exp(by_lane([c + jnp.zeros((t, GDN_W), f32) for c in gc_col]))
    kb = k * beta_x
    vb = v * beta_x
    kbg = kb * egc
    qd = q * egc
    ri = lax.broadcasted_iota(jnp.int32, (t, t), 0)
    ci = lax.broadcasted_iota(jnp.int32, (t, t), 1)
    eye = jnp.where(ri == ci, 1.0, 0.0)
    off_levels = []
    size = 1
    while size < t:
        off_levels.append(((ri // (2 * size)) == (ci // (2 * size))) & ((ri // size) != (ci // size)))
        size *= 2

    a_mats = []
    qkd = []
    for h in range(GDN_HEADS):
        kbm = jnp.where(head_of_lane[h], kb, 0.0).astype(bf16)
        qm = jnp.where(head_of_lane[h], q, 0.0).astype(bf16)
        kk = jnp.dot(kbm, kt, preferred_element_type=f32)
        qk = jnp.dot(qm, kt, preferred_element_type=f32)
        dec = jnp.exp(jnp.where(incl, gc_col[h] - gc_row[h], -jnp.inf))
        a_mats.append(jnp.where(strict, kk * dec, 0.0))
        qkd.append((qk * dec).astype(bf16))
    t_inv = [eye - jnp.where(off_levels[0], a, 0.0) for a in a_mats]
    for lvl_mask in off_levels[1:]:
        tbs = [x.astype(bf16) for x in t_inv]
        es = [jnp.where(lvl_mask, a, 0.0).astype(bf16) for a in a_mats]
        tes = [jnp.dot(tb, e, preferred_element_type=f32).astype(bf16) for tb, e in zip(tbs, es)]
        t_inv = [x - jnp.dot(te, tb, preferred_element_type=f32) for x, te, tb in zip(t_inv, tes, tbs)]
    uw = None
    for h in range(GDN_HEADS):
        rhs = jnp.concatenate([jnp.where(head_of_lane[h], vb, 0.0), jnp.where(head_of_lane[h], kbg, 0.0)], axis=1)
        term = jnp.dot(t_inv[h].astype(bf16), rhs.astype(bf16), preferred_element_type=f32)
        uw = term if uw is None else uw + term

    s_old = s_ref[...]
    sb = s_old.astype(bf16)
    u = uw[:, :GDN_W]
    w = uw[:, GDN_W:]
    v_new = u - jnp.dot(w.astype(bf16), sb, preferred_element_type=f32)
    o = jnp.dot(qd.astype(bf16), sb, preferred_element_type=f32)
    for h in range(GDN_HEADS):
        o = o + jnp.dot(qkd[h], jnp.where(head_of_lane[h], v_new, 0.0).astype(bf16), preferred_element_type=f32)
    o_ref[0, 0] = o

    mult = jnp.exp(gc_end[0] - gc_row[0]) + jnp.zeros((GDN_W, t), f32)
    for h in range(1, GDN_HEADS):
        mult = jnp.where(head_of_sub[h], jnp.exp(gc_end[h] - gc_row[h]), mult)
    kend_t = (kt.astype(f32) * mult).astype(bf16)
    upd = jnp.dot(kend_t, v_new.astype(bf16), preferred_element_type=f32)
    g_end = jnp.exp(by_lane(gc_end))
    same_head = head_of_sub[0] & head_of_lane[0]
    for h in range(1, GDN_HEADS):
        same_head = same_head | (head_of_sub[h] & head_of_lane[h])
    s_ref[...] = jnp.where(same_head, s_old * g_end + upd, 0.0)


def _gdn_scan(q, k, kt, v, cols, rows):
    b, n, _ = q.shape
    nt = n // TOK_TILE

    def tok(w):
        return pl.BlockSpec((1, TOK_TILE, w), lambda d, i, s: (i, _scan_tile(d, s, nt), 0))

    def tr(w):
        return pl.BlockSpec((1, w, TOK_TILE), lambda d, i, s: (i, 0, _scan_tile(d, s, nt)))

    return pl.pallas_call(
        _gdn_kernel,
        grid=(2, b, nt),
        in_specs=[tok(GDN_W), tok(GDN_W), tr(GDN_W), tok(GDN_W), tok(SMALL_W), tr(SMALL_W)],
        out_specs=pl.BlockSpec((1, 1, TOK_TILE, GDN_W), lambda d, i, s: (d, i, _scan_tile(d, s, nt), 0)),
        out_shape=jax.ShapeDtypeStruct((2, b, n, GDN_W), jnp.float32),
        scratch_shapes=[pltpu.VMEM((GDN_W, GDN_W), jnp.float32)],
        compiler_params=pltpu.CompilerParams(dimension_semantics=("parallel", "parallel", "arbitrary")),
        name="gdn_scan",
    )(q, k, kt, v, cols, rows)


def _small_lane_params(ssm_dt_bias, ssm_a_log, gdn_dt_bias, gdn_a_log):
    f32 = jnp.float32
    z = lambda k: jnp.zeros((k,), f32)
    sb = ssm_dt_bias.astype(f32).reshape(-1)
    bias = jnp.concatenate([sb, sb, z(L_GCUM - L_BETA), gdn_dt_bias.astype(f32).reshape(-1),
                            z(SMALL_W - L_GCUM - 2 * GDN_HEADS)])
    scale = jnp.concatenate([z(L_SCUM), -jnp.exp(ssm_a_log.astype(f32)).reshape(-1), z(L_GCUM - L_BETA),
                             -jnp.exp(gdn_a_log.astype(f32)).reshape(-1), z(SMALL_W - L_GCUM - 2 * GDN_HEADS)])
    return bias[None, :], scale[None, :]


def _permuted_w_in(w_in):
    offs = np.cumsum((0,) + SPLIT_SIZES)
    dt0, beta0, a0 = int(offs[3]), int(offs[6]), int(offs[7])
    dt = list(range(dt0, dt0 + SSM_HEADS))
    small = dt + dt + dt + dt + list(range(beta0, beta0 + 2 * GDN_HEADS)) + list(range(a0, a0 + 2 * GDN_HEADS))
    cols = list(range(0, dt0)) + list(range(int(offs[4]), beta0)) + small
    w = w_in[:, np.asarray(cols)]
    return jnp.pad(w, ((0, 0), (0, SMALL_W - len(small))))


def mixers_pallas(xbc, gqkv, small, ssm_conv_w, ssm_conv_b, ssm_dt_bias, ssm_a_log, gdn_conv_w, gdn_dt_bias, gdn_a_log):
    bias_vec, scale_vec = _small_lane_params(ssm_dt_bias, ssm_a_log, gdn_dt_bias, gdn_a_log)
    pad = ((0, 8 - CONV_K), (0, 0))
    xs, cm, bt, q, k, kt, v, cols, rows = _mixer_prep(
        xbc, gqkv, small, jnp.pad(ssm_conv_w, pad), ssm_conv_b[None, :], jnp.pad(gdn_conv_w, pad), bias_vec, scale_vec)
    y = _ssd_scan(xs, cm, bt, cols, rows)
    o = _gdn_scan(q, k, kt, v, cols, rows)
    return xs, y, o


def _expert_kernel(be_ref, nu_ref, x_ref, rg_ref, wg_ref, wu_ref, wd_ref, o_ref, wgb_ref, wub_ref, wdb_ref):
    i = pl.program_id(0)

    @pl.when((i == 0) | (be_ref[i] != be_ref[jnp.maximum(i - 1, 0)]))
    def _():
        wgb_ref[...] = wg_ref[0, 0].astype(jnp.bfloat16)
        wub_ref[...] = wu_ref[0, 0].astype(jnp.bfloat16)
        wdb_ref[...] = wd_ref[0, 0].astype(jnp.bfloat16)

    @pl.when(i < nu_ref[0])
    def _():
        x = x_ref[...]
        g = jnp.dot(x, wgb_ref[...], preferred_element_type=jnp.float32)
        u = jnp.dot(x, wub_ref[...], preferred_element_type=jnp.float32)
        h = (_silu(g) * u).astype(jnp.bfloat16)
        o_ref[...] = (jnp.dot(h, wdb_ref[...], preferred_element_type=jnp.float32) * rg_ref[...]).astype(o_ref.dtype)

    @pl.when(i >= nu_ref[0])
    def _():
        o_ref[...] = jnp.zeros_like(o_ref)


def _expert_blocks(layer, block_e, n_used, buf, row_gate, w_gate, w_up, w_down):
    n_rows, d = buf.shape
    n_blocks = n_rows // MOE_BLOCK
    grid_spec = pltpu.PrefetchScalarGridSpec(
        num_scalar_prefetch=2,
        grid=(n_blocks,),
        in_specs=[pl.BlockSpec((MOE_BLOCK, d), lambda i, be, nu: (i, 0)),
                  pl.BlockSpec((MOE_BLOCK, 1), lambda i, be, nu: (i, 0)),
                  pl.BlockSpec((1, 1, d, D_EXPERT), lambda i, be, nu: (layer, be[i], 0, 0)),
                  pl.BlockSpec((1, 1, d, D_EXPERT), lambda i, be, nu: (layer, be[i], 0, 0)),
                  pl.BlockSpec((1, 1, D_EXPERT, d), lambda i, be, nu: (layer, be[i], 0, 0))],
        out_specs=pl.BlockSpec((MOE_BLOCK, d), lambda i, be, nu: (i, 0)),
        scratch_shapes=[pltpu.VMEM((d, D_EXPERT), jnp.bfloat16), pltpu.VMEM((d, D_EXPERT), jnp.bfloat16),
                        pltpu.VMEM((D_EXPERT, d), jnp.bfloat16)],
    )
    return pl.pallas_call(
        _expert_kernel,
        grid_spec=grid_spec,
        out_shape=jax.ShapeDtypeStruct((n_rows, d), jnp.bfloat16),
        compiler_params=pltpu.CompilerParams(dimension_semantics=("arbitrary",), vmem_limit_bytes=MOE_VMEM_BYTES),
        name="moe_experts",
    )(block_e, n_used, buf, row_gate, w_gate, w_up, w_down)


def routed_experts(layer, t, experts, gates, w_gate, w_up, w_down):
    n, d = t.shape
    n_assign = n * TOP_K
    flat_e = experts.reshape(n_assign)
    order = jnp.argsort(flat_e).astype(jnp.int32)
    slot = jnp.argsort(order).astype(jnp.int32)
    counts = jax.ops.segment_sum(jnp.ones((n_assign,), jnp.int32), flat_e, num_segments=N_EXPERTS)
    starts = jnp.cumsum(counts) - counts
    padded = (counts + MOE_BLOCK - 1) // MOE_BLOCK * MOE_BLOCK
    pad_ends = jnp.cumsum(padded)
    pad_starts = pad_ends - padded
    n_blocks = -(-n_assign // MOE_BLOCK) + N_EXPERTS
    block_start = jnp.arange(n_blocks, dtype=jnp.int32) * MOE_BLOCK
    block_e = jnp.minimum(jnp.sum(block_start[:, None] >= pad_ends[None, :], axis=1), N_EXPERTS - 1).astype(jnp.int32)
    n_used = (pad_ends[-1] // MOE_BLOCK).astype(jnp.int32).reshape(1)
    within = (block_start - pad_starts[block_e])[:, None] + jnp.arange(MOE_BLOCK, dtype=jnp.int32)[None, :]
    valid = (within < counts[block_e][:, None]).reshape(-1)
    src_slot = jnp.minimum(starts[block_e][:, None] + within, n_assign - 1).reshape(-1)
    src_assign = order[src_slot]
    src_tok = jnp.where(valid, src_assign // TOP_K, jnp.arange(n_blocks * MOE_BLOCK, dtype=jnp.int32) % n)
    row_gate = jnp.where(valid, gates.reshape(n_assign).astype(jnp.float32)[src_assign], 0.0)[:, None]
    out = _expert_blocks(layer, block_e, n_used, t[src_tok], row_gate, w_gate, w_up, w_down)
    pos = (pad_starts[flat_e] + slot - starts[flat_e]).reshape(n, TOP_K)
    y = out[pos[:, 0]].astype(jnp.float32)
    for k in range(1, TOP_K):
        y = y + out[pos[:, k]].astype(jnp.float32)
    return y


def hier_moe(layer, t, t_rows, router_g_w, router_g_b, router_e_w, router_e_b, w_gate, w_up, w_down):
    n = t.shape[0]
    grp_prob = jax.nn.softmax(jnp.dot(t, router_g_w, preferred_element_type=jnp.float32)
                              + router_g_b.astype(jnp.float32), axis=-1)
    grp = jnp.argmax(grp_prob, axis=-1).astype(jnp.int32)
    p_grp = jnp.max(grp_prob, axis=-1)
    e_logits = (jnp.dot(t, router_e_w, preferred_element_type=jnp.float32)
                + router_e_b.astype(jnp.float32)).reshape(n, N_EGROUPS, EXPERTS_PER_GROUP)
    in_grp = grp[:, None, None] == jnp.arange(N_EGROUPS, dtype=jnp.int32)[None, :, None]
    probs = jax.nn.softmax(jnp.sum(jnp.where(in_grp, e_logits, 0.0), axis=1), axis=-1)
    lanes = jnp.arange(EXPERTS_PER_GROUP, dtype=jnp.int32)[None, :]
    idx, p_top = [], []
    for _ in range(TOP_K):
        i = jnp.argmax(probs, axis=-1).astype(jnp.int32)
        idx.append(i)
        p_top.append(jnp.max(probs, axis=-1))
        probs = jnp.where(lanes == i[:, None], -jnp.inf, probs)
    idx = jnp.stack(idx, axis=-1)
    p_top = jnp.stack(p_top, axis=-1)
    gates = p_grp[:, None] * p_top / jnp.sum(p_top, axis=-1, keepdims=True)
    experts = grp[:, None] * EXPERTS_PER_GROUP + idx
    return routed_experts(layer, t_rows, experts, gates, w_gate, w_up, w_down)


def _matmul_kernel(a_ref, b_ref, o_ref):
    o_ref[...] = jnp.dot(a_ref[...].astype(jnp.bfloat16), b_ref[...],
                         preferred_element_type=jnp.float32)


def _matmul(a, b, tm, tn):
    m, k = a.shape
    n = b.shape[1]
    return pl.pallas_call(
        _matmul_kernel,
        grid=(m // tm, n // tn),
        in_specs=[pl.BlockSpec((tm, k), lambda i, j: (i, 0)),
                  pl.BlockSpec((k, tn), lambda i, j: (0, j))],
        out_specs=pl.BlockSpec((tm, tn), lambda i, j: (i, j)),
        out_shape=jax.ShapeDtypeStruct((m, n), jnp.float32),
        compiler_params=pltpu.CompilerParams(dimension_semantics=("parallel", "parallel")),
        name="dense",
    )(a, b)


def _dense(a, w, tm=512):
    lead = a.shape[:-1]
    k = a.shape[-1]
    n = w.shape[1]
    a2 = a.reshape(-1, k)
    m = a2.shape[0]
    m_pad = -(-m // 8) * 8
    if m_pad != m:
        a2 = jnp.pad(a2, ((0, m_pad - m), (0, 0)))
    tm = min(tm, m_pad)
    n_pad = -(-n // 128) * 128
    tn = next(c for c in (1024, 768, 640, 512, 384, 256, 128) if n_pad % c == 0)
    wb = jnp.pad(w, ((0, 0), (0, n_pad - n))).astype(jnp.bfloat16)
    out = _matmul(a2, wb, tm, tn)[:m, :n]
    return out.reshape(lead + (n,))


IN_SLABS = (3 * DIFF_W, SSM_W, SSM_XBC, GDN_QKV, GDN_W, SMALL_W)
PROJ_VMEM_BYTES = 44 * 1024 * 1024


def _mod_row_spec(bsz):
    return pl.BlockSpec((1, MOD_CHUNKS, D_MODEL), lambda i, j: (jnp.where(j == 0, bsz, i), 0, 0))


def _in_proj_kernel(x_ref, nw_ref, mod_ref, w_ref, *out_refs):
    x = x_ref[0]
    xn = x * lax.rsqrt(jnp.mean(x * x, axis=-1, keepdims=True) + EPS) * nw_ref[...]
    mod = mod_ref[0]
    xm = (xn * (1.0 + mod[1:2, :]) + mod[0:1, :]).astype(jnp.bfloat16)
    c0 = 0
    for o_ref, width in zip(out_refs, IN_SLABS):
        o_ref[0] = jnp.dot(xm, w_ref[:, c0:c0 + width], preferred_element_type=jnp.float32)
        c0 += width


def _in_proj(x, norm_w, mod, w):
    b, n, d = x.shape
    tok = lambda width: pl.BlockSpec((1, TOK_TILE, width), lambda i, j: (i, j, 0))
    return pl.pallas_call(
        _in_proj_kernel,
        grid=(b, n // TOK_TILE),
        in_specs=[tok(d), pl.BlockSpec((1, d), lambda i, j: (0, 0)), _mod_row_spec(b),
                  pl.BlockSpec(w.shape, lambda i, j: (0, 0))],
        out_specs=[tok(width) for width in IN_SLABS],
        out_shape=[jax.ShapeDtypeStruct((b, n, width), jnp.float32) for width in IN_SLABS],
        compiler_params=pltpu.CompilerParams(dimension_semantics=("parallel", "parallel"),
                                             vmem_limit_bytes=PROJ_VMEM_BYTES),
        name="in_proj",
    )(x, norm_w, mod, w)


def _out_proj_kernel(att_ref, y0_ref, y1_ref, xs_ref, z_ref, o0_ref, o1_ref, gate_ref, x_ref, mod_ref,
                     dvec_ref, snw_ref, gnw_ref, n2w_ref, g_ref, w_ref, xo_ref, f_ref, fb_ref):
    ys = (y0_ref[0, 0] + y1_ref[0, 0] + dvec_ref[...] * xs_ref[0]) * _silu(z_ref[0])
    gw = SSM_W // SSM_GROUPS
    parts = [att_ref[0]]
    for g in range(SSM_GROUPS):
        yg = ys[:, g * gw:(g + 1) * gw]
        parts.append(yg * lax.rsqrt(jnp.mean(yg * yg, axis=-1, keepdims=True) + EPS) * snw_ref[:, g * gw:(g + 1) * gw])
    o = o0_ref[0, 0] + o1_ref[0, 0]
    parts.append(o * lax.rsqrt(_group_sum_sq(o, g_ref) + EPS) * gnw_ref[...] * _silu(gate_ref[0]))
    ml = jnp.concatenate(parts, axis=1).astype(jnp.bfloat16)
    mod = mod_ref[0]
    xn = x_ref[0] + mod[2:3, :] * jnp.dot(ml, w_ref[...], preferred_element_type=jnp.float32)
    xo_ref[0] = xn
    f = xn * lax.rsqrt(jnp.mean(xn * xn, axis=-1, keepdims=True) + EPS) * n2w_ref[...]
    f = f * (1.0 + mod[4:5, :]) + mod[3:4, :]
    f_ref[0] = f
    fb_ref[0] = f.astype(jnp.bfloat16)


def _out_proj(att, y, xs, z, o, gate, x, mod, dvec, snw, gnw, n2w, g64, w, latent_only):
    b, n, d = x.shape
    t0 = 1 if latent_only else 0
    n_out = n - t0 * TOK_TILE
    tok = lambda width: pl.BlockSpec((1, TOK_TILE, width), lambda i, j: (i, j + t0, 0))
    dirs = lambda width, k: pl.BlockSpec((1, 1, TOK_TILE, width), lambda i, j: (k, i, j + t0, 0))
    vec = lambda width: pl.BlockSpec((1, width), lambda i, j: (0, 0))
    out_tok = pl.BlockSpec((1, TOK_TILE, d), lambda i, j: (i, j, 0))
    mod_spec = pl.BlockSpec((1, MOD_CHUNKS, D_MODEL), lambda i, j: (i, 0, 0)) if latent_only else _mod_row_spec(b)
    return pl.pallas_call(
        _out_proj_kernel,
        grid=(b, n_out // TOK_TILE),
        in_specs=[tok(DIFF_W), dirs(SSM_W, 0), dirs(SSM_W, 1), tok(SSM_W), tok(SSM_W), dirs(GDN_W, 0), dirs(GDN_W, 1),
                  tok(GDN_W), tok(d), mod_spec, vec(SSM_W), vec(SSM_W), vec(GDN_W), vec(d),
                  pl.BlockSpec(g64.shape, lambda i, j: (0, 0)), pl.BlockSpec(w.shape, lambda i, j: (0, 0))],
        out_specs=[out_tok, out_tok, out_tok],
        out_shape=[jax.ShapeDtypeStruct((b, n_out, d), jnp.float32), jax.ShapeDtypeStruct((b, n_out, d), jnp.float32),
                   jax.ShapeDtypeStruct((b, n_out, d), jnp.bfloat16)],
        compiler_params=pltpu.CompilerParams(dimension_semantics=("parallel", "parallel"),
                                             vmem_limit_bytes=PROJ_VMEM_BYTES),
        name="out_proj",
    )(att, y, y, xs, z, o, o, gate, x, mod, dvec, snw, gnw, n2w, g64, w)


def kernel(x, c, ctx, c_ctx, w_mod, b_mod, norm1_w, norm2_w, w_in, w_out,
           diff_qn_w, diff_kn_w, diff_lq1, diff_lk1, diff_lq2, diff_lk2, diff_norm_w,
           ssm_conv_w, ssm_conv_b, ssm_dt_bias, ssm_a_log, ssm_d, ssm_norm_w,
           gdn_conv_w, gdn_dt_bias, gdn_a_log, gdn_norm_w,
           router_g_w, router_g_b, router_e_w, router_e_b, exp_w_gate, exp_w_up, exp_w_down):
    assert ctx.shape[1] == CTX_LEN == TOK_TILE and x.shape[1] % TOK_TILE == 0
    bsz = x.shape[0]
    xs_all = jnp.concatenate([ctx, x], axis=1)
    n_tok = xs_all.shape[1]
    is_lat = (jnp.arange(n_tok) >= CTX_LEN)[None, :, None]
    cc = jnp.concatenate([c, c_ctx[None, :]], axis=0)
    bf16 = jnp.bfloat16
    g64 = _group_matrix(GDN_W, GDN_DV, 1.0 / GDN_DV)
    for l in range(DEPTH):
        last = l == DEPTH - 1
        lam_init = 0.8 - 0.6 * math.exp(-0.3 * l)
        mod = (_dense(jax.nn.silu(cc), w_mod[l]) + b_mod[l]).reshape(bsz + 1, MOD_CHUNKS, D_MODEL)
        qkv, z, xbc, gqkv, gate, small = _in_proj(xs_all, norm1_w[l][None, :], mod, _permuted_w_in(w_in[l]).astype(bf16))
        att = diff_attention_pallas(qkv, lam_init, diff_qn_w[l], diff_kn_w[l],
                                    diff_lq1[l], diff_lk1[l], diff_lq2[l], diff_lk2[l], diff_norm_w[l])
        xs, y, o = mixers_pallas(xbc, gqkv, small, ssm_conv_w[l], ssm_conv_b[l],
                                 ssm_dt_bias[l], ssm_a_log[l], gdn_conv_w[l], gdn_dt_bias[l], gdn_a_log[l])
        xs_all, f, fb = _out_proj(att, y, xs, z, o, gate, xs_all, mod, jnp.repeat(ssm_d[l], SSM_P)[None, :],
                                  ssm_norm_w[l][None, :], jnp.tile(gdn_norm_w[l], GDN_HEADS)[None, :],
                                  norm2_w[l][None, :], g64, w_out[l].astype(bf16), latent_only=last)
        moe_w = (router_g_w[l], router_g_b[l], router_e_w[l], router_e_b[l], exp_w_gate, exp_w_up, exp_w_down)
        if last:
            yl = hier_moe(l, f.reshape(-1, D_MODEL), fb.reshape(-1, D_MODEL), *moe_w)
            return xs_all + mod[:bsz, None, 5, :] * yl.reshape(xs_all.shape)
        ym = hier_moe(l, f.reshape(-1, D_MODEL), fb.reshape(-1, D_MODEL), *moe_w)
        mod5 = jnp.where(is_lat, mod[:bsz, None, 5, :], mod[bsz, 5, :])
        xs_all = xs_all + mod5 * ym.reshape(xs_all.shape)
```
